```python
import math
import jax, jax.numpy as jnp
from jax import lax
import numpy as np

D_MODEL = 2048
BATCH = 4
SEQ = 2048
DEPTH = 2
DEC_BATCH = 128
DEC_SEQ = 4
PAST_LEN = 8192
PAGE_SIZE = 128

N_A_LAYERS = DEPTH // 2
N_B_LAYERS = DEPTH - N_A_LAYERS
EPS = 1e-6
GDN_QK_HEADS = 16
GDN_V_HEADS = 32
GDN_DK = 128
GDN_DV = 128
GDN_KEY_DIM = GDN_QK_HEADS * GDN_DK
GDN_VAL_DIM = GDN_V_HEADS * GDN_DV
CONV_W = 4
CONV_DIM = 2 * GDN_KEY_DIM + GDN_VAL_DIM
GDN_IN_DIM = CONV_DIM + GDN_VAL_DIM + 2 * GDN_V_HEADS
GDN_CHUNK = 64
MLA_HEADS = D_MODEL // 128
Q_LORA = 512
KV_LORA = 512
QK_NOPE = 128
QK_ROPE = 64
V_HEAD = 128
ROPE_THETA = 10000.0
ATTN_BLOCK = 128
D_FF = -(-8 * D_MODEL // (3 * 256)) * 256

kernel_name = "yoco_gdn_mla_adaln_step"


def rmsnorm(x, g):
    xf = x.astype(jnp.float32)
    y = xf * lax.rsqrt(jnp.mean(xf * xf, axis=-1, keepdims=True) + EPS)
    return (y * g.astype(jnp.float32)).astype(x.dtype)


def l2norm(x):
    xf = x.astype(jnp.float32)
    return xf * lax.rsqrt(jnp.sum(xf * xf, axis=-1, keepdims=True) + EPS)


def ada_mods(c, w, b, n):
    m = jax.nn.silu(c) @ w + b
    return [t[:, None, :] for t in jnp.split(m, n, axis=-1)]


def modulate(x, g, shift, scale):
    return rmsnorm(x, g) * (1.0 + scale) + shift


def rope(x, pos):
    half = QK_ROPE // 2
    inv = ROPE_THETA ** (-jnp.arange(half, dtype=jnp.float32) / half)
    ang = pos.astype(jnp.float32)[:, None] * inv[None, :]
    shape = (1, pos.shape[0]) + (1,) * (x.ndim - 3) + (half,)
    cos = jnp.cos(ang).reshape(shape)
    sin = jnp.sin(ang).reshape(shape)
    xf = x.astype(jnp.float32)
    x1, x2 = xf[..., :half], xf[..., half:]
    return jnp.concatenate([x1 * cos - x2 * sin, x2 * cos + x1 * sin], axis=-1).astype(x.dtype)


def swiglu(h, w_gate_up, w_down):
    gate, up = jnp.split(h @ w_gate_up, 2, axis=-1)
    return (jax.nn.silu(gate) * up) @ w_down


def causal_conv(u, conv_prev, w):
    up = jnp.concatenate([conv_prev.astype(u.dtype), u], axis=1)
    L = u.shape[1]
    out = up[:, 0:L] * w[0]
    for j in range(1, CONV_W):
        out = out + up[:, j:j + L] * w[j]
    return jax.nn.silu(out), up[:, -(CONV_W - 1):]


def gated_delta_chunked(q, k, v, g, beta, s0):
    B, L, H, DK = q.shape
    C = GDN_CHUNK
    n = -(-L // C)
    pad = n * C - L

    def prep(t):
        t = jnp.pad(t, [(0, 0), (0, pad)] + [(0, 0)] * (t.ndim - 2))
        t = jnp.moveaxis(t, 2, 1)
        return t.reshape(t.shape[:2] + (n, C) + t.shape[3:])

    q, k, v, g, beta = [prep(t) for t in (q, k, v, g, beta)]
    q = q * (GDN_DK ** -0.5)
    gc = jnp.cumsum(g, axis=-1)
    idx = jnp.arange(C)
    causal = idx[:, None] >= idx[None, :]
    strict = idx[:, None] > idx[None, :]
    decay = jnp.exp(jnp.where(causal, gc[..., :, None] - gc[..., None, :], -jnp.inf))
    kb = k * beta[..., None]
    kk = jnp.einsum('bhncd,bhnsd->bhncs', kb, k)
    a_mat = jnp.eye(C, dtype=jnp.float32) + jnp.where(strict, kk * decay, 0.0)
    t_inv = lax.linalg.triangular_solve(a_mat, jnp.broadcast_to(jnp.eye(C, dtype=jnp.float32), a_mat.shape),
                                        left_side=True, lower=True, unit_diagonal=True)
    u = t_inv @ (v * beta[..., None])
    w = t_inv @ (kb * jnp.exp(gc)[..., None])
    qk = jnp.einsum('bhncd,bhnsd->bhncs', q, k) * decay

    def step(S, xs):
        q_c, k_c, u_c, w_c, g_c, qk_c = xs
        v_new = u_c - jnp.einsum('bhck,bhkv->bhcv', w_c, S)
        o = (jnp.einsum('bhck,bhkv->bhcv', q_c * jnp.exp(g_c)[..., None], S)
             + jnp.einsum('bhcs,bhsv->bhcv', qk_c, v_new))
        g_last = g_c[..., -1:]
        S = (S * jnp.exp(g_last)[..., None]
             + jnp.einsum('bhck,bhcv->bhkv', k_c * jnp.exp(g_last - g_c)[..., None], v_new))
        return S, o

    xs = tuple(jnp.moveaxis(t, 2, 0) for t in (q, k, u, w, gc, qk))
    s_fin, o = lax.scan(step, s0, xs)
    o = jnp.moveaxis(o, 0, 2).reshape(B, H, n * C, -1)
    return jnp.moveaxis(o, 1, 2)[:, :L], s_fin


def gdn_mixer(h, conv_prev, s0, w_in, w_conv, a_log, dt_bias, g_norm, w_out):
    B, L, _ = h.shape
    proj = h @ w_in
    qkv, z, b, a = jnp.split(proj, [CONV_DIM, CONV_DIM + GDN_VAL_DIM, CONV_DIM + GDN_VAL_DIM + GDN_V_HEADS], axis=-1)
    qkv, conv_new = causal_conv(qkv, conv_prev, w_conv)
    q, k, v = jnp.split(qkv, [GDN_KEY_DIM, 2 * GDN_KEY_DIM], axis=-1)
    rep = GDN_V_HEADS // GDN_QK_HEADS
    q = jnp.repeat(l2norm(q.reshape(B, L, GDN_QK_HEADS, GDN_DK)), rep, axis=2)
    k = jnp.repeat(l2norm(k.reshape(B, L, GDN_QK_HEADS, GDN_DK)), rep, axis=2)
    v = v.reshape(B, L, GDN_V_HEADS, GDN_DV).astype(jnp.float32)
    beta = jax.nn.sigmoid(b.astype(jnp.float32))
    g = -jnp.exp(a_log.astype(jnp.float32)) * jax.nn.softplus(a.astype(jnp.float32) + dt_bias.astype(jnp.float32))
    o, s_new = gated_delta_chunked(q, k, v, g, beta, s0.astype(jnp.float32))
    o = rmsnorm(o, g_norm) * jax.nn.silu(z.reshape(B, L, GDN_V_HEADS, GDN_DV).astype(jnp.float32))
    y = o.reshape(B, L, GDN_VAL_DIM).astype(h.dtype) @ w_out
    return y, conv_new, s_new.astype(s0.dtype)


def mla_shared_kv(x, c, pos, w_ada, b_ada, g_in, w_down, g_kv):
    shift, scale = ada_mods(c, w_ada, b_ada, 2)
    hn = modulate(x, g_in, shift, scale)
    ckv, kpe = jnp.split(hn @ w_down, [KV_LORA], axis=-1)
    return rmsnorm(ckv, g_kv), rope(kpe, pos)


def mla_attend(q_lat, q_pe, ckv_new, kpe_new, ckv_past, kpe_past):
    B, T, H, R = q_lat.shape
    scale = (QK_NOPE + QK_ROPE) ** -0.5
    blk = ATTN_BLOCK if T % ATTN_BLOCK == 0 else T
    nb = T // blk
    kpos = jnp.arange(T)

    def one_block(args):
        i0, ql, qp = args
        s = (jnp.einsum('bqhr,bkr->bhqk', ql, ckv_new)
             + jnp.einsum('bqhp,bkp->bhqk', qp, kpe_new)).astype(jnp.float32) * scale
        qpos = i0 + jnp.arange(blk)
        s = jnp.where(kpos[None, :] <= qpos[:, None], s, -jnp.inf)
        if ckv_past is None:
            p = jax.nn.softmax(s, axis=-1).astype(ckv_new.dtype)
            return jnp.einsum('bhqk,bkr->bqhr', p, ckv_new)
        sp = (jnp.einsum('bqhr,bkr->bhqk', ql, ckv_past)
              + jnp.einsum('bqhp,bkp->bhqk', qp, kpe_past)).astype(jnp.float32) * scale
        P = ckv_past.shape[1]
        p = jax.nn.softmax(jnp.concatenate([sp, s], axis=-1), axis=-1).astype(ckv_new.dtype)
        return (jnp.einsum('bhqk,bkr->bqhr', p[..., :P], ckv_past)
                + jnp.einsum('bhqk,bkr->bqhr', p[..., P:], ckv_new))

    qlb = jnp.moveaxis(q_lat.reshape(B, nb, blk, H, R), 1, 0)
    qpb = jnp.moveaxis(q_pe.reshape(B, nb, blk, H, QK_ROPE), 1, 0)
    o = lax.map(one_block, (jnp.arange(nb) * blk, qlb, qpb))
    return jnp.moveaxis(o, 0, 1).reshape(B, T, H, R)


def mla_mixer(h, pos, ckv, kpe, ckv_past, kpe_past, w_dq, g_q, w_uq, w_uk, w_uv, w_o):
    B, L, _ = h.shape
    cq = rmsnorm(h @ w_dq, g_q)
    qf = (cq @ w_uq).reshape(B, L, MLA_HEADS, QK_NOPE + QK_ROPE)
    q_nope = qf[..., :QK_NOPE]
    q_pe = rope(qf[..., QK_NOPE:], pos)
    q_lat = jnp.einsum('blhd,rhd->blhr', q_nope, w_uk.reshape(KV_LORA, MLA_HEADS, QK_NOPE))
    o_lat = mla_attend(q_lat, q_pe, ckv, kpe, ckv_past, kpe_past)
    o = jnp.einsum('blhr,rhv->blhv', o_lat, w_uv.reshape(KV_LORA, MLA_HEADS, V_HEAD))
    return o.reshape(B, L, MLA_HEADS * V_HEAD) @ w_o


def trunk(x, c, pos, conv_in, ssm_in, ckv_past, kpe_past, p):
    conv_out, ssm_out = [], []
    ckv = kpe = None
    for l in range(DEPTH):
        sh1, sc1, gt1, sh2, sc2, gt2 = ada_mods(c, p['w_ada'][l], p['b_ada'][l], 6)
        h = modulate(x, p['g_mix'][l], sh1, sc1)
        if l < N_A_LAYERS:
            y, cs, ss = gdn_mixer(h, conv_in[l], ssm_in[l], p['gdn_w_in'][l], p['gdn_w_conv'][l],
                                  p['gdn_a_log'][l], p['gdn_dt_bias'][l], p['gdn_g_norm'][l], p['gdn_w_out'][l])
            conv_out.append(cs)
            ssm_out.append(ss)
        else:
            j = l - N_A_LAYERS
            y = mla_mixer(h, pos, ckv, kpe, ckv_past, kpe_past, p['mla_w_dq'][j], p['mla_g_q'][j],
                          p['mla_w_uq'][j], p['kv_w_uk'], p['kv_w_uv'], p['mla_w_o'][j])
        x = x + gt1 * y
        h = modulate(x, p['g_ffn'][l], sh2, sc2)
        x = x + gt2 * swiglu(h, p['w_gate_up'][l], p['w_down'][l])
        if l == N_A_LAYERS - 1:
            ckv, kpe = mla_shared_kv(x, c, pos, p['kv_w_ada'], p['kv_b_ada'], p['kv_g_in'],
                                     p['kv_w_down'], p['kv_g_norm'])
    shf, scf = ada_mods(c, p['final_w_ada'], p['final_b_ada'], 2)
    y = modulate(x, p['final_g'], shf, scf)
    return y, jnp.stack(conv_out), jnp.stack(ssm_out), ckv, kpe


def setup_inputs(seed: int = 0) -> dict:
    key = jax.random.key(seed)
    ks = iter(jax.random.split(key, 48))

    def nrm(shape, s):
        return s * jax.random.normal(next(ks), shape, jnp.float32)

    def gain(shape):
        return 1.0 + nrm(shape, 0.05)

    D = D_MODEL
    n_pages = PAST_LEN // PAGE_SIZE
    n_used = DEC_BATCH * n_pages
    n_phys = n_used + -(-n_used // 4)
    page_table = jax.random.permutation(next(ks), n_phys)[:n_used].reshape(DEC_BATCH, n_pages).astype(jnp.int32)
    dt = jax.random.uniform(next(ks), (N_A_LAYERS, GDN_V_HEADS), jnp.float32, 0.001, 0.1)
    a_log = jnp.log(jax.random.uniform(next(ks), (N_A_LAYERS, GDN_V_HEADS), jnp.float32, 1.0, 16.0))
    ada_s = 0.5 * D ** -0.5
    return {
        "x_prompt": nrm((BATCH, SEQ, D), 1.0),
        "x_sample": nrm((DEC_BATCH, DEC_SEQ, D), 1.0),
        "c_prompt": nrm((BATCH, D), 1.0),
        "c_sample": nrm((DEC_BATCH, D), 1.0),
        "cache_ckv": nrm((n_phys, PAGE_SIZE, KV_LORA), 1.0),
        "cache_kpe": nrm((n_phys, PAGE_SIZE, QK_ROPE), 1.0),
        "page_table": page_table,
        "state_ssm": nrm((N_A_LAYERS, DEC_BATCH, GDN_V_HEADS, GDN_DK, GDN_DV), GDN_DK ** -0.5),
        "state_conv": nrm((N_A_LAYERS, DEC_BATCH, CONV_W - 1, CONV_DIM), 1.0),
        "w_ada": nrm((DEPTH, D, 6 * D), ada_s),
        "b_ada": nrm((DEPTH, 6 * D), 0.01),
        "g_mix": gain((DEPTH, D)),
        "g_ffn": gain((DEPTH, D)),
        "w_gate_up": nrm((DEPTH, D, 2 * D_FF), D ** -0.5),
        "w_down": nrm((DEPTH, D_FF, D), D_FF ** -0.5),
        "gdn_w_in": nrm((N_A_LAYERS, D, GDN_IN_DIM), D ** -0.5),
        "gdn_w_conv": nrm((N_A_LAYERS, CONV_W, CONV_DIM), CONV_W ** -0.5),
        "gdn_a_log": a_log,
        "gdn_dt_bias": jnp.log(jnp.expm1(dt)),
        "gdn_g_norm": gain((N_A_LAYERS, GDN_DV)),
        "gdn_w_out": nrm((N_A_LAYERS, GDN_VAL_DIM, D), GDN_VAL_DIM ** -0.5),
        "kv_w_ada": nrm((D, 2 * D), ada_s),
        "kv_b_ada": nrm((2 * D,), 0.01),
        "kv_g_in": gain((D,)),
        "kv_w_down": nrm((D, KV_LORA + QK_ROPE), D ** -0.5),
        "kv_g_norm": gain((KV_LORA,)),
        "kv_w_uk": nrm((KV_LORA, MLA_HEADS * QK_NOPE), KV_LORA ** -0.5),
        "kv_w_uv": nrm((KV_LORA, MLA_HEADS * V_HEAD), KV_LORA ** -0.5),
        "mla_w_dq": nrm((N_B_LAYERS, D, Q_LORA), D ** -0.5),
        "mla_g_q": gain((N_B_LAYERS, Q_LORA)),
        "mla_w_uq": nrm((N_B_LAYERS, Q_LORA, MLA_HEADS * (QK_NOPE + QK_ROPE)), Q_LORA ** -0.5),
        "mla_w_o": nrm((N_B_LAYERS, MLA_HEADS * V_HEAD, D), (MLA_HEADS * V_HEAD) ** -0.5),
        "final_w_ada": nrm((D, 2 * D), ada_s),
        "final_b_ada": nrm((2 * D,), 0.01),
        "final_g": gain((D,)),
    }


def reference(x_prompt, x_sample, c_prompt, c_sample, cache_ckv, cache_kpe, page_table, state_ssm, state_conv,
              w_ada, b_ada, g_mix, g_ffn, w_gate_up, w_down,
              gdn_w_in, gdn_w_conv, gdn_a_log, gdn_dt_bias, gdn_g_norm, gdn_w_out,
              kv_w_ada, kv_b_ada, kv_g_in, kv_w_down, kv_g_norm, kv_w_uk, kv_w_uv,
              mla_w_dq, mla_g_q, mla_w_uq, mla_w_o, final_w_ada, final_b_ada, final_g):
    p = dict(w_ada=w_ada, b_ada=b_ada, g_mix=g_mix, g_ffn=g_ffn, w_gate_up=w_gate_up, w_down=w_down,
             gdn_w_in=gdn_w_in, gdn_w_conv=gdn_w_conv, gdn_a_log=gdn_a_log, gdn_dt_bias=gdn_dt_bias,
             gdn_g_norm=gdn_g_norm, gdn_w_out=gdn_w_out, kv_w_ada=kv_w_ada, kv_b_ada=kv_b_ada,
             kv_g_in=kv_g_in, kv_w_down=kv_w_down, kv_g_norm=kv_g_norm, kv_w_uk=kv_w_uk, kv_w_uv=kv_w_uv,
             mla_w_dq=mla_w_dq, mla_g_q=mla_g_q, mla_w_uq=mla_w_uq, mla_w_o=mla_w_o,
             final_w_ada=final_w_ada, final_b_ada=final_b_ada, final_g=final_g)
    B, S, _ = x_prompt.shape
    conv0 = jnp.zeros((N_A_LAYERS, B, CONV_W - 1, CONV_DIM), x_prompt.dtype)
    ssm0 = jnp.zeros((N_A_LAYERS, B, GDN_V_HEADS, GDN_DK, GDN_DV), state_ssm.dtype)
    y_prompt, conv_prompt, ssm_prompt, ckv_prompt, kpe_prompt = trunk(
        x_prompt, c_prompt, jnp.arange(S), conv0, ssm0, None, None, p)
    Bd, T, _ = x_sample.shape
    past_len = page_table.shape[1] * cache_ckv.shape[1]
    ckv_past = cache_ckv[page_table].reshape(Bd, past_len, KV_LORA)
    kpe_past = cache_kpe[page_table].reshape(Bd, past_len, QK_ROPE)
    y_sample, conv_sample, ssm_sample, ckv_sample, kpe_sample = trunk(
        x_sample, c_sample, past_len + jnp.arange(T), state_conv, state_ssm, ckv_past, kpe_past, p)
    return (y_prompt, y_sample, ssm_prompt, conv_prompt, ckv_prompt, kpe_prompt,
            ssm_sample, conv_sample, ckv_sample, kpe_sample)
```

```python
import functools

import jax
import jax.numpy as jnp
from jax import lax
from jax.experimental import pallas as pl
from jax.experimental.pallas import tpu as pltpu

EPS = 1e-6
ROPE_THETA = 10000.0
CONV_W = 4
GDN_CHUNK = 64
F32 = jnp.float32
BF = jnp.bfloat16

V7X_VMEM_BYTES = 64 * 1024 * 1024
VMEM_LIMIT = V7X_VMEM_BYTES - 8 * 1024 * 1024
LANES = 128
SUBLANES = 8
BF16_ROWS = 16
TRI_BLOCK = 16


def _params(n_axes):
    return pltpu.CompilerParams(dimension_semantics=("arbitrary",) * n_axes, vmem_limit_bytes=VMEM_LIMIT)


def _pick(n, cands):
    for c in cands:
        if n % c == 0:
            return c
    return n


def _silu(x):
    return x * jax.nn.sigmoid(x)


def _dot(a, b):
    return jnp.dot(a, b, preferred_element_type=F32)


def _dot_nt(a, b):
    return lax.dot_general(a, b, (((1,), (1,)), ((), ())), preferred_element_type=F32)


def _dot_tn(a, b):
    return lax.dot_general(a, b, (((0,), (0,)), ((), ())), preferred_element_type=F32)


def _split_bf16(a):
    hi = a.astype(BF)
    lo = (a - hi.astype(F32)).astype(BF)
    return hi, lo


def _dot3(a, b):
    ah, al = _split_bf16(a)
    bh, bl = _split_bf16(b)
    return _dot(ah, bh) + _dot(ah, bl) + _dot(al, bh)


def _modulate_rows(x_ref, g_ref, sh_ref, sc_ref, h_ref, rows):
    tm = x_ref.shape[0]
    per_row = sh_ref.shape[0] != 1

    def body(r, carry):
        sl = pl.ds(pl.multiple_of(r * rows, rows), rows)
        x = x_ref[sl, :]
        y = x * lax.rsqrt(jnp.mean(x * x, axis=-1, keepdims=True) + EPS) * g_ref[...]
        sc = sc_ref[sl, :] if per_row else sc_ref[...]
        sh = sh_ref[sl, :] if per_row else sh_ref[...]
        h_ref[sl, :] = (y * (1.0 + sc) + sh).astype(h_ref.dtype)
        return carry

    lax.fori_loop(0, tm // rows, body, 0)


def _mod_specs(shift, K, tm, rows_per_group):
    if shift.ndim == 3:
        tiles_per_group = rows_per_group // tm
        return pl.BlockSpec((None, 1, K), lambda i, *_: (i // tiles_per_group, 0, 0))
    return pl.BlockSpec((tm, K), lambda i, *_: (i, 0))


def _w_spec(w, layer, K, tn, col_off_blocks=0):
    if w.ndim == 3:
        return pl.BlockSpec((None, K, tn), lambda i, j: (layer, 0, j + col_off_blocks))
    return pl.BlockSpec((K, tn), lambda i, j: (0, j + col_off_blocks))


def _modmm_kernel(x_ref, g_ref, sh_ref, sc_ref, w_ref, o_ref, h_ref, *, rows):
    @pl.when(pl.program_id(1) == 0)
    def _():
        _modulate_rows(x_ref, g_ref, sh_ref, sc_ref, h_ref, rows)

    o_ref[...] = _dot(h_ref[...], w_ref[...].astype(BF)).astype(o_ref.dtype)


def _modmm_swiglu_kernel(x_ref, g_ref, sh_ref, sc_ref, wg_ref, wu_ref, o_ref, h_ref, *, rows):
    @pl.when(pl.program_id(1) == 0)
    def _():
        _modulate_rows(x_ref, g_ref, sh_ref, sc_ref, h_ref, rows)

    h = h_ref[...]
    gate = _dot(h, wg_ref[...].astype(BF))
    up = _dot(h, wu_ref[...].astype(BF))
    o_ref[...] = (_silu(gate) * up).astype(o_ref.dtype)


def _row_tile(M, rows_per_group, grouped):
    base = rows_per_group if grouped else M
    return _pick(base, (1024, 512, 256, 128, 64, 32, 16, 8))


def mod_matmul(x, g, shift, scale, w, layer, n_out, rows_per_group, *, swiglu=False, out_dtype=F32, name):
    M, K = x.shape
    tm = _row_tile(M, rows_per_group, shift.ndim == 3)
    tn = _pick(n_out, (256, 128)) if swiglu else _pick(n_out, (512, 256, 128))
    rows = _pick(tm, (128, 64, 32, 16, 8))
    mod_spec = _mod_specs(shift, K, tm, rows_per_group)
    in_specs = [pl.BlockSpec((tm, K), lambda i, j: (i, 0)),
                pl.BlockSpec((1, K), lambda i, j: (0, 0)),
                mod_spec, mod_spec,
                _w_spec(w, layer, K, tn)]
    args = [x, g.reshape(1, K), shift, scale, w]
    if swiglu:
        in_specs.append(_w_spec(w, layer, K, tn, n_out // tn))
        args.append(w)
        body = functools.partial(_modmm_swiglu_kernel, rows=rows)
    else:
        body = functools.partial(_modmm_kernel, rows=rows)
    return pl.pallas_call(
        body,
        grid=(M // tm, n_out // tn),
        in_specs=in_specs,
        out_specs=pl.BlockSpec((tm, tn), lambda i, j: (i, j)),
        out_shape=jax.ShapeDtypeStruct((M, n_out), out_dtype),
        scratch_shapes=[pltpu.VMEM((tm, K), BF)],
        compiler_params=_params(2),
        name=name,
    )(*args)


def _mmres_kernel(a_ref, w_ref, res_ref, gate_ref, o_ref):
    y = _dot(a_ref[...], w_ref[...].astype(BF))
    o_ref[...] = res_ref[...] + gate_ref[...] * y


def matmul_residual(a, w, layer, res, gate, rows_per_group, *, name):
    M, K = a.shape
    N = res.shape[1]
    tm = _row_tile(M, rows_per_group, gate.ndim == 3)
    tn = _pick(N, (256, 128)) if K > 4096 else _pick(N, (512, 256, 128))
    if gate.ndim == 3:
        tiles_per_group = rows_per_group // tm
        gate_spec = pl.BlockSpec((None, 1, tn), lambda i, j: (i // tiles_per_group, 0, j))
    else:
        gate_spec = pl.BlockSpec((tm, tn), lambda i, j: (i, j))
    return pl.pallas_call(
        _mmres_kernel,
        grid=(M // tm, N // tn),
        in_specs=[pl.BlockSpec((tm, K), lambda i, j: (i, 0)),
                  _w_spec(w, layer, K, tn),
                  pl.BlockSpec((tm, tn), lambda i, j: (i, j)),
                  gate_spec],
        out_specs=pl.BlockSpec((tm, tn), lambda i, j: (i, j)),
        out_shape=jax.ShapeDtypeStruct((M, N), F32),
        compiler_params=_params(2),
        name=name,
    )(a, w, res, gate)


def _ada_kernel(c_ref, w_ref, b_ref, o_ref):
    a = _silu(c_ref[...]).astype(BF)
    o_ref[...] = _dot(a, w_ref[...].astype(BF)) + b_ref[...]


def ada_dense(c, w, b, layer, *, name):
    M, K = c.shape
    N = w.shape[-1]
    tn = _pick(N, (512, 256, 128))
    if b.ndim == 2:
        b_spec = pl.BlockSpec((None, 1, tn), lambda i, j: (layer, 0, j))
        b = b.reshape(b.shape[0], 1, N)
    else:
        b_spec = pl.BlockSpec((1, tn), lambda i, j: (0, j))
        b = b.reshape(1, N)
    return pl.pallas_call(
        _ada_kernel,
        grid=(1, N // tn),
        in_specs=[pl.BlockSpec((M, K), lambda i, j: (0, 0)), _w_spec(w, layer, K, tn), b_spec],
        out_specs=pl.BlockSpec((M, tn), lambda i, j: (0, j)),
        out_shape=jax.ShapeDtypeStruct((M, N), F32),
        compiler_params=_params(2),
        name=name,
    )(c, w, b)


def _modulate_kernel(x_ref, g_ref, sh_ref, sc_ref, o_ref, *, rows):
    _modulate_rows(x_ref, g_ref, sh_ref, sc_ref, o_ref, rows)


def modulate_rows(x, g, shift, scale, rows_per_group, *, name):
    M, K = x.shape
    tm = _row_tile(M, rows_per_group, shift.ndim == 3)
    rows = _pick(tm, (128, 64, 32, 16, 8))
    mod_spec = _mod_specs(shift, K, tm, rows_per_group)
    return pl.pallas_call(
        functools.partial(_modulate_kernel, rows=rows),
        grid=(M // tm,),
        in_specs=[pl.BlockSpec((tm, K), lambda i: (i, 0)), pl.BlockSpec((1, K), lambda i: (0, 0)),
                  mod_spec, mod_spec],
        out_specs=pl.BlockSpec((tm, K), lambda i: (i, 0)),
        out_shape=jax.ShapeDtypeStruct((M, K), F32),
        compiler_params=_params(1),
        name=name,
    )(x, g.reshape(1, K), shift, scale)


def _tri_inverse(n_low, ii, jj):
    cp = n_low.shape[0]
    eye = (ii == jj).astype(F32)
    shift = TRI_BLOCK.bit_length() - 1
    nd = jnp.where((ii >> shift) == (jj >> shift), n_low, 0.0)
    p = eye - nd
    npow = nd
    width = 1
    while 2 * width < TRI_BLOCK:
        npow = _dot3(npow, npow)
        p = p + _dot3(p, npow)
        width *= 2
    size = TRI_BLOCK
    while size < cp:
        s = size.bit_length() - 1
        off = ((ii >> (s + 1)) == (jj >> (s + 1))) & (((ii >> s) & 1) == 1) & (((jj >> s) & 1) == 0)
        x = _dot3(jnp.where(off, n_low, 0.0), p)
        p = p - _dot3(p, x)
        size *= 2
    return p


def _gdn_kernel(*refs, C, Cp, hb, rep, DK, DV, has_state):
    alog_ref, dtb_ref, q_ref, k_ref, v_ref, z_ref, bT_ref, aT_ref, wq_ref, wk_ref, wv_ref, gn_ref = refs[:12]
    pos = 12
    if has_state:
        cq_ref, ck_ref, cv_ref, s0_ref = refs[pos:pos + 4]
        pos += 4
    o_ref, cqo_ref, cko_ref, cvo_ref, so_ref = refs[pos:pos + 5]
    s_scr, cbq, cbk, cbv = refs[pos + 5:]

    hg = pl.program_id(1)
    c = pl.program_id(2)
    last = pl.num_programs(2) - 1
    tail_lo = SUBLANES - (CONV_W - 1)

    @pl.when(c == 0)
    def _():
        for cb in (cbq, cbk, cbv):
            cb[...] = jnp.zeros_like(cb)
        if has_state:
            cbq[tail_lo:SUBLANES, :] = cq_ref[...]
            cbk[tail_lo:SUBLANES, :] = ck_ref[...]
            cbv[tail_lo:SUBLANES, :] = cv_ref[...]
            s_scr[...] = s0_ref[...]
        else:
            s_scr[...] = jnp.zeros_like(s_scr)

    def conv(cb, x_ref, w_ref, tail_out_ref):
        cb[SUBLANES:SUBLANES + C, :] = x_ref[...]
        acc = cb[tail_lo:tail_lo + Cp, :] * w_ref[0:1, :]
        for j in range(1, CONV_W):
            acc = acc + cb[tail_lo + j:tail_lo + j + Cp, :] * w_ref[j:j + 1, :]
        tail = cb[C + tail_lo:C + SUBLANES, :]
        cb[tail_lo:SUBLANES, :] = tail

        @pl.when(c == last)
        def _():
            tail_out_ref[...] = tail

        return _silu(acc)

    qc = conv(cbq, q_ref, wq_ref, cqo_ref)
    kc = conv(cbk, k_ref, wk_ref, cko_ref)
    vc = conv(cbv, v_ref, wv_ref, cvo_ref)

    ii = lax.broadcasted_iota(jnp.int32, (Cp, Cp), 0)
    jj = lax.broadcasted_iota(jnp.int32, (Cp, Cp), 1)
    eye = ii == jj
    causal = jj <= ii
    strict = jj < ii
    padded = Cp != C
    if padded:
        row_ok = lax.broadcasted_iota(jnp.int32, (Cp, 1), 0) < C
        zpad = jnp.zeros((1, Cp - C), F32)
    gn = gn_ref[...]

    for i in range(hb):
        qh = qc[:, i * DK:(i + 1) * DK]
        kh = kc[:, i * DK:(i + 1) * DK]
        qh = qh * lax.rsqrt(jnp.sum(qh * qh, axis=-1, keepdims=True) + EPS) * (DK ** -0.5)
        kh = kh * lax.rsqrt(jnp.sum(kh * kh, axis=-1, keepdims=True) + EPS)
        if padded:
            kh = jnp.where(row_ok, kh, 0.0)
        q_bf = qh.astype(BF)
        k_bf = kh.astype(BF)
        qk_raw = _dot_nt(q_bf, k_bf)
        for e in range(rep):
            hl = i * rep + e
            hv = (hg * hb + i) * rep + e
            vh = vc[:, hl * DV:(hl + 1) * DV]
            if padded:
                vh = jnp.where(row_ok, vh, 0.0)
            b_row = bT_ref[hl, pl.ds(c, 1), :]
            a_row = aT_ref[hl, pl.ds(c, 1), :]
            beta_row = jax.nn.sigmoid(b_row)
            a_neg = -jnp.exp(jnp.full((1, C), alog_ref[hv], F32))
            sp_in = a_row + dtb_ref[hv]
            softplus = jnp.maximum(sp_in, 0.0) + jnp.log1p(jnp.exp(-jnp.abs(sp_in)))
            g_row = a_neg * softplus
            if padded:
                beta_row = jnp.concatenate([beta_row, zpad], axis=1)
                g_row = jnp.concatenate([g_row, zpad], axis=1)
            g_mat = jnp.broadcast_to(g_row, (Cp, Cp))
            g_col = jnp.sum(jnp.where(eye, g_mat, 0.0), axis=1, keepdims=True)
            gc_col = jnp.sum(jnp.where(causal, g_mat, 0.0), axis=1, keepdims=True)
            gc_row = jnp.sum(jnp.where(ii <= jj, jnp.broadcast_to(g_col, (Cp, Cp)), 0.0), axis=0, keepdims=True)
            beta_col = jnp.sum(jnp.where(eye, jnp.broadcast_to(beta_row, (Cp, Cp)), 0.0), axis=1, keepdims=True)
            g_last = jnp.sum(g_row, axis=1, keepdims=True)
            decay = jnp.exp(jnp.where(causal, gc_col - gc_row, -jnp.inf))
            eg = jnp.exp(gc_col)
            kb = kh * beta_col
            kk = _dot_nt(kb.astype(BF), k_bf)
            t_inv = _tri_inverse(jnp.where(strict, kk * decay, 0.0), ii, jj).astype(BF)
            u = _dot(t_inv, (vh * beta_col).astype(BF))
            w = _dot(t_inv, (kb * eg).astype(BF))
            qk = (qk_raw * decay).astype(BF)
            s_old = s_scr[hl]
            s_bf = s_old.astype(BF)
            v_new = u - _dot(w.astype(BF), s_bf)
            v_new_bf = v_new.astype(BF)
            o = _dot((qh * eg).astype(BF), s_bf) + _dot(qk, v_new_bf)
            k_dec = (kh * jnp.exp(g_last - gc_col)).astype(BF)
            s_scr[hl] = s_old * jnp.exp(g_last) + _dot_tn(k_dec, v_new_bf)
            on = o * lax.rsqrt(jnp.mean(o * o, axis=-1, keepdims=True) + EPS) * gn
            zg = z_ref[:, hl * DV:(hl + 1) * DV]
            o_ref[:, hl * DV:(hl + 1) * DV] = (on[:C] * _silu(zg)).astype(o_ref.dtype)

    @pl.when(c == last)
    def _():
        so_ref[...] = s_scr[...]


def gdn_scan(proj, baT, conv_prev, s0, w_conv, a_log, dt_bias, g_norm, *, B, L, HQK, HV, DK, DV, name):
    rep = HV // HQK
    KEY, VAL = HQK * DK, HV * DV
    C = _pick(L, (GDN_CHUNK,))
    n = L // C
    Cp = -(-C // BF16_ROWS) * BF16_ROWS
    hb = _pick(HQK, (2, 1))
    has_state = s0 is not None
    bT, aT = baT
    qw, vw = hb * DK, hb * rep * DV
    k_off, v_off, z_off = KEY // qw, 2 * KEY // vw, (2 * KEY + VAL) // vw

    smem = pl.BlockSpec(memory_space=pltpu.SMEM)
    in_specs = [
        smem, smem,
        pl.BlockSpec((None, C, qw), lambda b, h, c: (b, c, h)),
        pl.BlockSpec((None, C, qw), lambda b, h, c: (b, c, k_off + h)),
        pl.BlockSpec((None, C, vw), lambda b, h, c: (b, c, v_off + h)),
        pl.BlockSpec((None, C, vw), lambda b, h, c: (b, c, z_off + h)),
        pl.BlockSpec((None, hb * rep, n, C), lambda b, h, c: (b, h, 0, 0)),
        pl.BlockSpec((None, hb * rep, n, C), lambda b, h, c: (b, h, 0, 0)),
        pl.BlockSpec((CONV_W, qw), lambda b, h, c: (0, h)),
        pl.BlockSpec((CONV_W, qw), lambda b, h, c: (0, k_off + h)),
        pl.BlockSpec((CONV_W, vw), lambda b, h, c: (0, v_off + h)),
        pl.BlockSpec((1, DV), lambda b, h, c: (0, 0)),
    ]
    args = [a_log, dt_bias, proj, proj, proj, proj, bT, aT, w_conv, w_conv, w_conv, g_norm.reshape(1, DV)]
    if has_state:
        in_specs += [
            pl.BlockSpec((None, CONV_W - 1, qw), lambda b, h, c: (b, 0, h)),
            pl.BlockSpec((None, CONV_W - 1, qw), lambda b, h, c: (b, 0, k_off + h)),
            pl.BlockSpec((None, CONV_W - 1, vw), lambda b, h, c: (b, 0, v_off + h)),
            pl.BlockSpec((None, hb * rep, DK, DV), lambda b, h, c: (b, h, 0, 0)),
        ]
        args += [conv_prev, conv_prev, conv_prev, s0]
    out_specs = [
        pl.BlockSpec((None, C, vw), lambda b, h, c: (b, c, h)),
        pl.BlockSpec((None, CONV_W - 1, qw), lambda b, h, c: (b, 0, h)),
        pl.BlockSpec((None, CONV_W - 1, qw), lambda b, h, c: (b, 0, h)),
        pl.BlockSpec((None, CONV_W - 1, vw), lambda b, h, c: (b, 0, h)),
        pl.BlockSpec((None, hb * rep, DK, DV), lambda b, h, c: (b, h, 0, 0)),
    ]
    out_shape = [
        jax.ShapeDtypeStruct((B, L, VAL), BF),
        jax.ShapeDtypeStruct((B, CONV_W - 1, KEY), F32),
        jax.ShapeDtypeStruct((B, CONV_W - 1, KEY), F32),
        jax.ShapeDtypeStruct((B, CONV_W - 1, VAL), F32),
        jax.ShapeDtypeStruct((B, HV, DK, DV), F32),
    ]
    body = functools.partial(_gdn_kernel, C=C, Cp=Cp, hb=hb, rep=rep, DK=DK, DV=DV, has_state=has_state)
    return pl.pallas_call(
        body,
        grid=(B, HQK // hb, n),
        in_specs=in_specs,
        out_specs=out_specs,
        out_shape=out_shape,
        scratch_shapes=[pltpu.VMEM((hb * rep, DK, DV), F32),
                        pltpu.VMEM((SUBLANES + Cp, qw), F32),
                        pltpu.VMEM((SUBLANES + Cp, qw), F32),
                        pltpu.VMEM((SUBLANES + Cp, vw), F32)],
        compiler_params=_params(3),
        name=name,
    )(*args)


def _kv_kernel(x_ref, g_ref, sh_ref, sc_ref, w_ref, gkv_ref, cos_ref, sin_ref,
               ckv_ref, kpe_ref, kcat_ref, h_ref, *, rows, R, P):
    _modulate_rows(x_ref, g_ref, sh_ref, sc_ref, h_ref, rows)
    y = _dot(h_ref[...], w_ref[...].astype(BF))
    c = y[:, :R]
    ckv = c * lax.rsqrt(jnp.mean(c * c, axis=-1, keepdims=True) + EPS) * gkv_ref[...]
    rot = y[:, R:R + LANES] * cos_ref[...] + y[:, R + LANES:R + 2 * LANES] * sin_ref[...]
    ckv_ref[...] = ckv
    kpe_ref[...] = rot[:, :P]
    kcat_ref[:, :R] = ckv.astype(BF)
    kcat_ref[:, R:] = rot.astype(BF)


def _rope_weight_cols(w_pe):
    K, P = w_pe.shape
    half = P // 2
    zeros = jnp.zeros((K, LANES - P), w_pe.dtype)
    swapped = jnp.concatenate([w_pe[:, half:], w_pe[:, :half]], axis=1)
    return jnp.concatenate([w_pe, zeros, swapped, zeros], axis=1)


def shared_kv(x, g, shift, scale, w_down, g_kv, cos_t, sin_t, rows_per_group, *, R, P, name):
    M, K = x.shape
    tm = _row_tile(M, rows_per_group, shift.ndim == 3)
    tm = min(tm, 512)
    rows = _pick(tm, (128, 64, 32, 16, 8))
    w_ext = jnp.concatenate([w_down[:, :R], _rope_weight_cols(w_down[:, R:])], axis=1)
    NW = R + 2 * LANES
    mod_spec = _mod_specs(shift, K, tm, rows_per_group)
    row = lambda i: (i, 0)
    fixed = lambda i: (0, 0)
    return pl.pallas_call(
        functools.partial(_kv_kernel, rows=rows, R=R, P=P),
        grid=(M // tm,),
        in_specs=[pl.BlockSpec((tm, K), row), pl.BlockSpec((1, K), fixed), mod_spec, mod_spec,
                  pl.BlockSpec((K, NW), fixed), pl.BlockSpec((1, R), fixed),
                  pl.BlockSpec((tm, LANES), row), pl.BlockSpec((tm, LANES), row)],
        out_specs=[pl.BlockSpec((tm, R), row), pl.BlockSpec((tm, P), row), pl.BlockSpec((tm, R + LANES), row)],
        out_shape=[jax.ShapeDtypeStruct((M, R), F32), jax.ShapeDtypeStruct((M, P), F32),
                   jax.ShapeDtypeStruct((M, R + LANES), BF)],
        scratch_shapes=[pltpu.VMEM((tm, K), BF)],
        compiler_params=_params(1),
        name=name,
    )(x, g.reshape(1, K), shift, scale, w_ext, g_kv.reshape(1, R), cos_t, sin_t)


def _q_kernel(x_ref, g_ref, sh_ref, sc_ref, wdq_ref, gq_ref, wuq_ref, wuk_ref, cos_ref, sin_ref,
              o_ref, cq_ref, h_ref, *, rows, NOPE, R):
    @pl.when(pl.program_id(1) == 0)
    def _():
        _modulate_rows(x_ref, g_ref, sh_ref, sc_ref, h_ref, rows)
        c = _dot(h_ref[...], wdq_ref[...].astype(BF))
        cq_ref[...] = (c * lax.rsqrt(jnp.mean(c * c, axis=-1, keepdims=True) + EPS) * gq_ref[...]).astype(BF)

    qf = _dot(cq_ref[...], wuq_ref[...].astype(BF))
    q_lat = _dot_nt(qf[:, :NOPE].astype(BF), wuk_ref[...].astype(BF))
    rot = qf[:, NOPE:NOPE + LANES] * cos_ref[...] + qf[:, NOPE + LANES:NOPE + 2 * LANES] * sin_ref[...]
    o_ref[:, :R] = q_lat.astype(o_ref.dtype)
    o_ref[:, R:] = rot.astype(o_ref.dtype)


def mla_queries(x, g, shift, scale, w_dq, g_q, w_uq, w_uk, cos_t, sin_t, rows_per_group, *,
                H, NOPE, P, R, head_major, name):
    M, K = x.shape
    QL = w_dq.shape[1]
    tm = _row_tile(M, rows_per_group, True) if head_major else _row_tile(M, rows_per_group, shift.ndim == 3)
    tm = min(tm, 512)
    rows = _pick(tm, (128, 64, 32, 16, 8))
    w_heads = w_uq.reshape(QL, H, NOPE + P)
    w_uq_r = jnp.concatenate(
        [jnp.concatenate([w_heads[:, h, :NOPE], _rope_weight_cols(w_heads[:, h, NOPE:])], axis=1)[None]
         for h in range(H)], axis=0)
    NQ = NOPE + 2 * LANES
    W = R + LANES
    mod_spec = _mod_specs(shift, K, tm, rows_per_group)
    row = lambda i, h: (i, 0)
    fixed = lambda i, h: (0, 0)
    if head_major:
        tiles = rows_per_group // tm
        out_spec = pl.BlockSpec((None, None, tm, W), lambda i, h: (i // tiles, h, i % tiles, 0))
        out_shape = jax.ShapeDtypeStruct((M // rows_per_group, H, rows_per_group, W), BF)
    else:
        out_spec = pl.BlockSpec((tm, W), lambda i, h: (i, h))
        out_shape = jax.ShapeDtypeStruct((M, H * W), BF)
    return pl.pallas_call(
        functools.partial(_q_kernel, rows=rows, NOPE=NOPE, R=R),
        grid=(M // tm, H),
        in_specs=[pl.BlockSpec((tm, K), row), pl.BlockSpec((1, K), fixed), mod_spec, mod_spec,
                  pl.BlockSpec((K, QL), fixed), pl.BlockSpec((1, QL), fixed),
                  pl.BlockSpec((None, QL, NQ), lambda i, h: (h, 0, 0)),
                  pl.BlockSpec((R, NOPE), lambda i, h: (0, h)),
                  pl.BlockSpec((tm, LANES), row), pl.BlockSpec((tm, LANES), row)],
        out_specs=out_spec,
        out_shape=out_shape,
        scratch_shapes=[pltpu.VMEM((tm, QL), BF), pltpu.VMEM((tm, K), BF)],
        compiler_params=_params(2),
        name=name,
    )(x, g.reshape(1, K), shift, scale, w_dq, g_q.reshape(1, QL), w_uq_r, w_uk, cos_t, sin_t)


def _attn_prompt_kernel(q_ref, k_ref, o_ref, m_scr, l_scr, acc_scr, *, tq, tk, H, R, scale):
    qi = pl.program_id(1)
    kj = pl.program_id(2)

    @pl.when(kj == 0)
    def _():
        m_scr[...] = jnp.full_like(m_scr, -jnp.inf)
        l_scr[...] = jnp.zeros_like(l_scr)
        acc_scr[...] = jnp.zeros_like(acc_scr)

    @pl.when(kj * tk <= qi * tq + tq - 1)
    def _():
        q = q_ref[...].reshape(H * tq, q_ref.shape[-1])
        k = k_ref[...]
        s = _dot_nt(q, k) * scale
        qpos = qi * tq + lax.rem(lax.broadcasted_iota(jnp.int32, (H * tq, tk), 0), tq)
        kpos = kj * tk + lax.broadcasted_iota(jnp.int32, (H * tq, tk), 1)
        s = jnp.where(kpos <= qpos, s, -jnp.inf)
        m_old = m_scr[...]
        m_new = jnp.maximum(m_old, jnp.max(s, axis=-1, keepdims=True))
        alpha = jnp.exp(m_old - m_new)
        p = jnp.exp(s - m_new)
        l_scr[...] = alpha * l_scr[...] + jnp.sum(p, axis=-1, keepdims=True)
        acc_scr[...] = alpha * acc_scr[...] + _dot(p.astype(BF), k[:, :R])
        m_scr[...] = m_new

    @pl.when(kj == pl.num_programs(2) - 1)
    def _():
        o = acc_scr[...] / l_scr[...]
        o_ref[...] = o.reshape(H, tq, R).astype(o_ref.dtype)


def attention_prompt(q4, kcat, *, R, scale, name):
    B, H, L, W = q4.shape
    tq = _pick(L, (128, 64, 32, 16))
    tk = _pick(L, (512, 256, 128, 64, 32, 16))
    nq, nk = L // tq, L // tk

    def k_index(b, i, j):
        return (b, jnp.minimum(j, (i * tq + tq - 1) // tk), 0)

    return pl.pallas_call(
        functools.partial(_attn_prompt_kernel, tq=tq, tk=tk, H=H, R=R, scale=scale),
        grid=(B, nq, nk),
        in_specs=[pl.BlockSpec((None, H, tq, W), lambda b, i, j: (b, 0, i, 0)),
                  pl.BlockSpec((None, tk, W), k_index)],
        out_specs=pl.BlockSpec((None, H, tq, R), lambda b, i, j: (b, 0, i, 0)),
        out_shape=jax.ShapeDtypeStruct((B, H, L, R), BF),
        scratch_shapes=[pltpu.VMEM((H * tq, 1), F32), pltpu.VMEM((H * tq, 1), F32),
                        pltpu.VMEM((H * tq, R), F32)],
        compiler_params=_params(3),
        name=name,
    )(q4, kcat)


def _attn_sample_kernel(pt_ref, q_ref, knew_ref, *refs, NP, T, H, R, P, scale):
    ckv_refs = refs[:NP]
    kpe_refs = refs[NP:2 * NP]
    o_ref, m_scr, l_scr, acc_scr = refs[2 * NP:]
    j = pl.program_id(1)
    rows = T * H
    q = q_ref[...]
    q_lat = q[:, :R]
    q_pe = q[:, R:R + P]

    def update(s, values):
        m_old = m_scr[...]
        m_new = jnp.maximum(m_old, jnp.max(s, axis=-1, keepdims=True))
        alpha = jnp.exp(m_old - m_new)
        p = jnp.exp(s - m_new)
        l_scr[...] = alpha * l_scr[...] + jnp.sum(p, axis=-1, keepdims=True)
        acc_scr[...] = alpha * acc_scr[...] + values(p.astype(BF))
        m_scr[...] = m_new

    @pl.when(j == 0)
    def _():
        m_scr[...] = jnp.full_like(m_scr, -jnp.inf)
        l_scr[...] = jnp.zeros_like(l_scr)
        acc_scr[...] = jnp.zeros_like(acc_scr)
        tp = -(-T // BF16_ROWS) * BF16_ROWS
        knew = jnp.concatenate([knew_ref[...], jnp.zeros((tp - T, knew_ref.shape[1]), BF)], axis=0)
        s = _dot_nt(q, knew) * scale
        qt = lax.broadcasted_iota(jnp.int32, (rows, tp), 0) // H
        kt = lax.broadcasted_iota(jnp.int32, (rows, tp), 1)
        s = jnp.where(kt <= qt, s, -jnp.inf)
        update(s, lambda p: _dot(p, knew[:, :R]))

    pages = [ckv_refs[i][...].astype(BF) for i in range(NP)]
    s = jnp.concatenate(
        [_dot_nt(q_lat, pages[i]) + _dot_nt(q_pe, kpe_refs[i][...].astype(BF)) for i in range(NP)], axis=1) * scale
    page = pages[0].shape[0]

    def values(p):
        out = _dot(p[:, :page], pages[0])
        for i in range(1, NP):
            out = out + _dot(p[:, i * page:(i + 1) * page], pages[i])
        return out

    update(s, values)

    @pl.when(j == pl.num_programs(1) - 1)
    def _():
        o_ref[...] = (acc_scr[...] / l_scr[...]).astype(o_ref.dtype)


def attention_sample(q3, knew, cache_ckv, cache_kpe, page_table, *, T, H, R, P, scale, name):
    Bd, rows, W = q3.shape
    n_pages = page_table.shape[1]
    page = cache_ckv.shape[1]
    NP = _pick(n_pages, (8, 4, 2, 1))

    def page_spec(width, i):
        return pl.BlockSpec((None, page, width), lambda b, j, pt: (pt[b, j * NP + i], 0, 0))

    grid_spec = pltpu.PrefetchScalarGridSpec(
        num_scalar_prefetch=1,
        grid=(Bd, n_pages // NP),
        in_specs=[pl.BlockSpec((None, rows, W), lambda b, j, pt: (b, 0, 0)),
                  pl.BlockSpec((None, T, W), lambda b, j, pt: (b, 0, 0))]
                 + [page_spec(R, i) for i in range(NP)] + [page_spec(P, i) for i in range(NP)],
        out_specs=pl.BlockSpec((None, rows, R), lambda b, j, pt: (b, 0, 0)),
        scratch_shapes=[pltpu.VMEM((rows, 1), F32), pltpu.VMEM((rows, 1), F32), pltpu.VMEM((rows, R), F32)],
    )
    return pl.pallas_call(
        functools.partial(_attn_sample_kernel, NP=NP, T=T, H=H, R=R, P=P, scale=scale),
        grid_spec=grid_spec,
        out_shape=jax.ShapeDtypeStruct((Bd, rows, R), BF),
        compiler_params=_params(2),
        name=name,
    )(page_table, q3, knew, *([cache_ckv] * NP), *([cache_kpe] * NP))


def _mm_kernel(a_ref, w_ref, o_ref):
    o_ref[...] = _dot(a_ref[...], w_ref[...].astype(BF)).astype(o_ref.dtype)


def value_up(o_lat, w_uv, *, H, R, V, head_major, name):
    if head_major:
        G, _, L, _ = o_lat.shape
        M = G * L
        tm = _pick(L, (1024, 512, 256, 128, 64, 32, 16, 8))
        tiles = L // tm
        a_spec = pl.BlockSpec((None, None, tm, R), lambda i, h: (i // tiles, h, i % tiles, 0))
    else:
        M = o_lat.shape[0]
        tm = _pick(M, (1024, 512, 256, 128, 64, 32, 16, 8))
        a_spec = pl.BlockSpec((tm, R), lambda i, h: (i, h))
    return pl.pallas_call(
        _mm_kernel,
        grid=(M // tm, H),
        in_specs=[a_spec, pl.BlockSpec((R, V), lambda i, h: (0, h))],
        out_specs=pl.BlockSpec((tm, V), lambda i, h: (i, h)),
        out_shape=jax.ShapeDtypeStruct((M, H * V), BF),
        compiler_params=_params(2),
        name=name,
    )(o_lat, w_uv)


def _rope_tables(pos, P, reps):
    half = P // 2
    inv = ROPE_THETA ** (-jnp.arange(half, dtype=F32) / half)
    ang = pos.astype(F32)[:, None] * inv[None, :]
    cos, sin = jnp.cos(ang), jnp.sin(ang)
    zeros = jnp.zeros((pos.shape[0], LANES - P), F32)
    cos_t = jnp.concatenate([cos, cos, zeros], axis=1)
    sin_t = jnp.concatenate([-sin, sin, zeros], axis=1)
    return jnp.tile(cos_t, (reps, 1)), jnp.tile(sin_t, (reps, 1))


def _trunk(x3, mods, pos, conv_in, ssm_in, past, p, tag):
    G, L, D = x3.shape
    M = G * L
    grouped = L % SUBLANES == 0
    x = x3.reshape(M, D)

    def split_mods(m, n):
        parts = jnp.split(m, n, axis=-1)
        if grouped:
            return [t[:, None, :] for t in parts]
        return [jnp.repeat(t, L, axis=0) for t in parts]

    HV, DK, DV = p['HV'], p['DK'], p['DV']
    HQK = p['HQK']
    KEY, VAL = HQK * DK, HV * DV
    H, NOPE, P, R, V = p['H'], p['NOPE'], p['P'], p['R'], p['V']
    FF = p['w_down'].shape[1]
    cos_t, sin_t = _rope_tables(pos, P, G)
    scale = (NOPE + P) ** -0.5

    sh1, sc1, gt1, sh2, sc2, gt2 = split_mods(mods['l0'], 6)
    w_in = p['gdn_w_in']
    proj = mod_matmul(x, p['g_mix'][0], sh1, sc1, w_in, 0, 2 * KEY + 2 * VAL, L, name=f'{tag}_gdn_in')
    w_ba = jnp.concatenate([w_in[0, :, 2 * KEY + 2 * VAL:], jnp.zeros((D, LANES - 2 * HV), F32)], axis=1)
    ba = mod_matmul(x, p['g_mix'][0], sh1, sc1, w_ba, 0, LANES, L, name=f'{tag}_gdn_ba')
    C = _pick(L, (GDN_CHUNK,))
    ba4 = ba.reshape(G, L // C, C, LANES)
    bT = jnp.transpose(ba4[..., :HV], (0, 3, 1, 2))
    aT = jnp.transpose(ba4[..., HV:2 * HV], (0, 3, 1, 2))
    o_g, cq, ck, cv, ssm_new = gdn_scan(
        proj.reshape(G, L, -1), (bT, aT), conv_in, ssm_in, p['gdn_w_conv'][0], p['gdn_a_log'][0],
        p['gdn_dt_bias'][0], p['gdn_g_norm'][0], B=G, L=L, HQK=HQK, HV=HV, DK=DK, DV=DV, name=f'{tag}_gdn_scan')
    conv_new = jnp.concatenate([cq, ck, cv], axis=-1)
    x = matmul_residual(o_g.reshape(M, VAL), p['gdn_w_out'], 0, x, gt1, L, name=f'{tag}_gdn_out')
    hff = mod_matmul(x, p['g_ffn'][0], sh2, sc2, p['w_gate_up'], 0, FF, L, swiglu=True, out_dtype=BF,
                     name=f'{tag}_ffn0_up')
    x = matmul_residual(hff, p['w_down'], 0, x, gt2, L, name=f'{tag}_ffn0_down')

    shk, sck = split_mods(mods['kv'], 2)
    ckv, kpe, kcat = shared_kv(x, p['kv_g_in'], shk, sck, p['kv_w_down'], p['kv_g_norm'], cos_t, sin_t, L,
                               R=R, P=P, name=f'{tag}_kv')

    sh1, sc1, gt1, sh2, sc2, gt2 = split_mods(mods['l1'], 6)
    head_major = past is None
    q = mla_queries(x, p['g_mix'][1], sh1, sc1, p['mla_w_dq'][0], p['mla_g_q'][0], p['mla_w_uq'][0], p['kv_w_uk'],
                    cos_t, sin_t, L, H=H, NOPE=NOPE, P=P, R=R, head_major=head_major, name=f'{tag}_q')
    if head_major:
        o_lat = attention_prompt(q, kcat.reshape(G, L, R + LANES), R=R, scale=scale, name=f'{tag}_attn')
    else:
        cache_ckv, cache_kpe, page_table = past
        o_lat = attention_sample(q.reshape(G, L * H, R + LANES), kcat.reshape(G, L, R + LANES), cache_ckv,
                                 cache_kpe, page_table, T=L, H=H, R=R, P=P, scale=scale, name=f'{tag}_attn')
        o_lat = o_lat.reshape(M, H * R)
    o = value_up(o_lat, p['kv_w_uv'], H=H, R=R, V=V, head_major=head_major, name=f'{tag}_uv')
    x = matmul_residual(o, p['mla_w_o'], 0, x, gt1, L, name=f'{tag}_attn_out')
    hff = mod_matmul(x, p['g_ffn'][1], sh2, sc2, p['w_gate_up'], 1, FF, L, swiglu=True, out_dtype=BF,
                     name=f'{tag}_ffn1_up')
    x = matmul_residual(hff, p['w_down'], 1, x, gt2, L, name=f'{tag}_ffn1_down')

    shf, scf = split_mods(mods['final'], 2)
    y = modulate_rows(x, p['final_g'], shf, scf, L, name=f'{tag}_final')
    return (y.reshape(G, L, D), conv_new[None], ssm_new[None], ckv.reshape(G, L, R), kpe.reshape(G, L, P))


def kernel(x_prompt, x_sample, c_prompt, c_sample, cache_ckv, cache_kpe, page_table, state_ssm, state_conv, w_ada, b_ada, g_mix, g_ffn, w_gate_up, w_down, gdn_w_in, gdn_w_conv, gdn_a_log, gdn_dt_bias, gdn_g_norm, gdn_w_out, kv_w_ada, kv_b_ada, kv_g_in, kv_w_down, kv_g_norm, kv_w_uk, kv_w_uv, mla_w_dq, mla_g_q, mla_w_uq, mla_w_o, final_w_ada, final_b_ada, final_g):
    B, S, D = x_prompt.shape
    Bd, T, _ = x_sample.shape
    HV, DK, DV = state_ssm.shape[2:]
    KEY = (state_conv.shape[-1] - HV * DV) // 2
    R = cache_ckv.shape[-1]
    P = cache_kpe.shape[-1]
    QL = mla_w_dq.shape[-1]
    nope_total = kv_w_uk.shape[1]
    H = (mla_w_uq.shape[-1] - nope_total) // P
    p = dict(w_ada=w_ada, b_ada=b_ada, g_mix=g_mix, g_ffn=g_ffn, w_gate_up=w_gate_up, w_down=w_down,
             gdn_w_in=gdn_w_in, gdn_w_conv=gdn_w_conv, gdn_a_log=gdn_a_log, gdn_dt_bias=gdn_dt_bias,
             gdn_g_norm=gdn_g_norm, gdn_w_out=gdn_w_out, kv_g_in=kv_g_in, kv_w_down=kv_w_down,
             kv_g_norm=kv_g_norm, kv_w_uk=kv_w_uk, kv_w_uv=kv_w_uv, mla_w_dq=mla_w_dq, mla_g_q=mla_g_q,
             mla_w_uq=mla_w_uq, mla_w_o=mla_w_o, final_g=final_g,
             HV=HV, DK=DK, DV=DV, HQK=KEY // DK, H=H, NOPE=nope_total // H, P=P, R=R,
             V=kv_w_uv.shape[1] // H)

    n_c = B + Bd
    pad = -n_c % SUBLANES
    c_all = jnp.concatenate([c_prompt, c_sample, jnp.zeros((pad, D), F32)], axis=0)
    m_l0 = ada_dense(c_all, w_ada, b_ada, 0, name='ada_l0')
    m_l1 = ada_dense(c_all, w_ada, b_ada, 1, name='ada_l1')
    m_kv = ada_dense(c_all, kv_w_ada, kv_b_ada, 0, name='ada_kv')
    m_f = ada_dense(c_all, final_w_ada, final_b_ada, 0, name='ada_final')

    def mods(lo, hi):
        return dict(l0=m_l0[lo:hi], l1=m_l1[lo:hi], kv=m_kv[lo:hi], final=m_f[lo:hi])

    y_p, conv_p, ssm_p, ckv_p, kpe_p = _trunk(x_prompt, mods(0, B), jnp.arange(S), None, None, None, p, 'p')
    past_len = page_table.shape[1] * cache_ckv.shape[1]
    y_s, conv_s, ssm_s, ckv_s, kpe_s = _trunk(x_sample, mods(B, n_c), past_len + jnp.arange(T), state_conv[0],
                                              state_ssm[0], (cache_ckv, cache_kpe, page_table), p, 's')
    return (y_p, y_s, ssm_p, conv_p, ckv_p, kpe_p, ssm_s, conv_s, ckv_s, kpe_s)
```

```python
import functools

import jax
import jax.numpy as jnp
from jax import lax
from jax.experimental import pallas as pl
from jax.experimental.pallas import tpu as pltpu

EPS = 1e-6
ROPE_THETA = 10000.0
CONV_W = 4
GDN_CHUNK = 64
F32 = jnp.float32
BF = jnp.bfloat16

V7X_VMEM_BYTES = 64 * 1024 * 1024
VMEM_LIMIT = V7X_VMEM_BYTES - 8 * 1024 * 1024
LANES = 128
SUBLANES = 8
BF16_ROWS = 16
TRI_BLOCK = 16
STACK_ROWS = 256
STACKS_PER_STEP = 4


def _params(n_axes):
    return pltpu.CompilerParams(dimension_semantics=("arbitrary",) * n_axes, vmem_limit_bytes=VMEM_LIMIT)


def _pick(n, cands):
    for c in cands:
        if n % c == 0:
            return c
    return n


def _silu(x):
    return x * jax.nn.sigmoid(x)


def _contract(a, b, ca, cb):
    batch = tuple(range(a.ndim - 2))
    dims = (((a.ndim - 2 + ca,), (b.ndim - 2 + cb,)), (batch, batch))
    return lax.dot_general(a, b, dims, preferred_element_type=F32)


def _dot(a, b):
    return _contract(a, b, 1, 0)


def _dot_nt(a, b):
    return _contract(a, b, 1, 1)


def _dot_tn(a, b):
    return _contract(a, b, 0, 0)


def _split_bf16(a):
    hi = a.astype(BF)
    lo = (a - hi.astype(F32)).astype(BF)
    return hi, lo


def _dot3(a, b):
    ah, al = _split_bf16(a)
    bh, bl = _split_bf16(b)
    return _dot(ah, bh) + _dot(ah, bl) + _dot(al, bh)


def _modulate_rows(x_ref, g_ref, sh_ref, sc_ref, h_ref, rows):
    tm = x_ref.shape[0]
    per_row = sh_ref.shape[0] != 1

    def body(r, carry):
        sl = pl.ds(pl.multiple_of(r * rows, rows), rows)
        x = x_ref[sl, :]
        y = x * lax.rsqrt(jnp.mean(x * x, axis=-1, keepdims=True) + EPS) * g_ref[...]
        sc = sc_ref[sl, :] if per_row else sc_ref[...]
        sh = sh_ref[sl, :] if per_row else sh_ref[...]
        h_ref[sl, :] = (y * (1.0 + sc) + sh).astype(h_ref.dtype)
        return carry

    lax.fori_loop(0, tm // rows, body, 0)


def _mod_specs(shift, K, tm, rows_per_group):
    if shift.ndim == 3:
        tiles_per_group = rows_per_group // tm
        return pl.BlockSpec((None, 1, K), lambda i, *_: (i // tiles_per_group, 0, 0))
    return pl.BlockSpec((tm, K), lambda i, *_: (i, 0))


def _w_spec(w, layer, K, tn, col_off_blocks=0):
    if w.ndim == 3:
        return pl.BlockSpec((None, K, tn), lambda i, j: (layer, 0, j + col_off_blocks))
    return pl.BlockSpec((K, tn), lambda i, j: (0, j + col_off_blocks))


def _modmm_kernel(x_ref, g_ref, sh_ref, sc_ref, w_ref, o_ref, h_ref, *, rows, w_is_nk):
    @pl.when(pl.program_id(1) == 0)
    def _():
        _modulate_rows(x_ref, g_ref, sh_ref, sc_ref, h_ref, rows)

    dot = _dot_nt if w_is_nk else _dot
    o_ref[...] = dot(h_ref[...], w_ref[...].astype(BF)).astype(o_ref.dtype)


def _modmm_swiglu_kernel(x_ref, g_ref, sh_ref, sc_ref, wg_ref, wu_ref, o_ref, h_ref, *, rows):
    @pl.when(pl.program_id(1) == 0)
    def _():
        _modulate_rows(x_ref, g_ref, sh_ref, sc_ref, h_ref, rows)

    h = h_ref[...]
    gate = _dot(h, wg_ref[...].astype(BF))
    up = _dot(h, wu_ref[...].astype(BF))
    o_ref[...] = (_silu(gate) * up).astype(o_ref.dtype)


def _row_tile(M, rows_per_group, grouped):
    base = rows_per_group if grouped else M
    return _pick(base, (1024, 512, 256, 128, 64, 32, 16, 8))


def mod_matmul(x, g, shift, scale, w, layer, n_out, rows_per_group, *, swiglu=False, w_is_nk=False,
               out_dtype=F32, name):
    M, K = x.shape
    tm = _row_tile(M, rows_per_group, shift.ndim == 3)
    tn = _pick(n_out, (256, 128)) if swiglu else _pick(n_out, (512, 256, 128))
    rows = _pick(tm, (128, 64, 32, 16, 8))
    mod_spec = _mod_specs(shift, K, tm, rows_per_group)
    if not w_is_nk:
        w_spec = _w_spec(w, layer, K, tn)
    elif w.ndim == 3:
        w_spec = pl.BlockSpec((None, tn, K), lambda i, j: (layer, j, 0))
    else:
        w_spec = pl.BlockSpec((tn, K), lambda i, j: (j, 0))
    in_specs = [pl.BlockSpec((tm, K), lambda i, j: (i, 0)),
                pl.BlockSpec((1, K), lambda i, j: (0, 0)),
                mod_spec, mod_spec, w_spec]
    args = [x, g.reshape(1, K), shift, scale, w]
    if swiglu:
        in_specs.append(_w_spec(w, layer, K, tn, n_out // tn))
        args.append(w)
        body = functools.partial(_modmm_swiglu_kernel, rows=rows)
    else:
        body = functools.partial(_modmm_kernel, rows=rows, w_is_nk=w_is_nk)
    return pl.pallas_call(
        body,
        grid=(M // tm, n_out // tn),
        in_specs=in_specs,
        out_specs=pl.BlockSpec((tm, tn), lambda i, j: (i, j)),
        out_shape=jax.ShapeDtypeStruct((M, n_out), out_dtype),
        scratch_shapes=[pltpu.VMEM((tm, K), BF)],
        compiler_params=_params(2),
        name=name,
    )(*args)


def _mmres_kernel(a_ref, w_ref, res_ref, gate_ref, o_ref):
    y = _dot(a_ref[...], w_ref[...].astype(BF))
    o_ref[...] = res_ref[...] + gate_ref[...] * y


def matmul_residual(a, w, layer, res, gate, rows_per_group, *, name):
    M, K = a.shape
    N = res.shape[1]
    tm = _row_tile(M, rows_per_group, gate.ndim == 3)
    tn = _pick(N, (256, 128)) if K > 4096 else _pick(N, (512, 256, 128))
    if gate.ndim == 3:
        tiles_per_group = rows_per_group // tm
        gate_spec = pl.BlockSpec((None, 1, tn), lambda i, j: (i // tiles_per_group, 0, j))
    else:
        gate_spec = pl.BlockSpec((tm, tn), lambda i, j: (i, j))
    return pl.pallas_call(
        _mmres_kernel,
        grid=(M // tm, N // tn),
        in_specs=[pl.BlockSpec((tm, K), lambda i, j: (i, 0)),
                  _w_spec(w, layer, K, tn),
                  pl.BlockSpec((tm, tn), lambda i, j: (i, j)),
                  gate_spec],
        out_specs=pl.BlockSpec((tm, tn), lambda i, j: (i, j)),
        out_shape=jax.ShapeDtypeStruct((M, N), F32),
        compiler_params=_params(2),
        name=name,
    )(a, w, res, gate)


def _ada_kernel(c_ref, w_ref, b_ref, o_ref):
    a = _silu(c_ref[...]).astype(BF)
    o_ref[...] = _dot(a, w_ref[...].astype(BF)) + b_ref[...]


def ada_dense(c, w, b, layer, *, name):
    M, K = c.shape
    N = w.shape[-1]
    tn = _pick(N, (512, 256, 128))
    if b.ndim == 2:
        b_spec = pl.BlockSpec((None, 1, tn), lambda i, j: (layer, 0, j))
        b = b.reshape(b.shape[0], 1, N)
    else:
        b_spec = pl.BlockSpec((1, tn), lambda i, j: (0, j))
        b = b.reshape(1, N)
    return pl.pallas_call(
        _ada_kernel,
        grid=(1, N // tn),
        in_specs=[pl.BlockSpec((M, K), lambda i, j: (0, 0)), _w_spec(w, layer, K, tn), b_spec],
        out_specs=pl.BlockSpec((M, tn), lambda i, j: (0, j)),
        out_shape=jax.ShapeDtypeStruct((M, N), F32),
        compiler_params=_params(2),
        name=name,
    )(c, w, b)


def _modulate_kernel(x_ref, g_ref, sh_ref, sc_ref, o_ref, *, rows):
    _modulate_rows(x_ref, g_ref, sh_ref, sc_ref, o_ref, rows)


def modulate_rows(x, g, shift, scale, rows_per_group, *, name):
    M, K = x.shape
    tm = _row_tile(M, rows_per_group, shift.ndim == 3)
    rows = _pick(tm, (128, 64, 32, 16, 8))
    mod_spec = _mod_specs(shift, K, tm, rows_per_group)
    return pl.pallas_call(
        functools.partial(_modulate_kernel, rows=rows),
        grid=(M // tm,),
        in_specs=[pl.BlockSpec((tm, K), lambda i: (i, 0)), pl.BlockSpec((1, K), lambda i: (0, 0)),
                  mod_spec, mod_spec],
        out_specs=pl.BlockSpec((tm, K), lambda i: (i, 0)),
        out_shape=jax.ShapeDtypeStruct((M, K), F32),
        compiler_params=_params(1),
        name=name,
    )(x, g.reshape(1, K), shift, scale)


def _dot1(a, b):
    return _dot(a.astype(BF), b.astype(BF))


def _tri_inverse(n_low, ii, jj, cp, nil):
    rn = n_low.shape[-1]
    eye = (ii == jj).astype(F32)
    base = min(TRI_BLOCK, cp)
    shift = base.bit_length() - 1
    nd = jnp.where((ii >> shift) == (jj >> shift), n_low, 0.0)
    p = eye - nd
    if nil > 2:
        npow = _dot1(nd, nd)
        pw = 2
        while 2 * pw - 1 < nil - 1:
            both = _dot1(jnp.concatenate([p, npow], axis=-2), npow)
            p = p + both[:, :rn]
            npow = both[:, rn:]
            pw *= 2
        p = p + _dot1(p, npow)
    size = base
    while size < cp:
        s = size.bit_length() - 1
        off = ((ii >> (s + 1)) == (jj >> (s + 1))) & (((ii >> s) & 1) == 1) & (((jj >> s) & 1) == 0)
        x = _dot1(jnp.where(off, n_low, 0.0), p)
        p = p - _dot1(p, x)
        size *= 2
    resid = (eye - p) - _dot3(n_low, p)
    return p + _dot1(p, resid)


def _delta_chunk(q_st, k_st, v_st, beta_row, g_row, s_scr, *, G, Cp, nil):
    S, rn, dv = v_st.shape
    ii = lax.broadcasted_iota(jnp.int32, (S, rn, rn), 1)
    jj = lax.broadcasted_iota(jnp.int32, (S, rn, rn), 2)
    sh = Cp.bit_length() - 1
    same = (ii >> sh) == (jj >> sh)
    eye = ii == jj
    causal = same & (jj <= ii)
    g_mat = jnp.broadcast_to(g_row, (S, rn, rn))
    g_col = jnp.sum(jnp.where(eye, g_mat, 0.0), axis=2, keepdims=True)
    gc_col = jnp.sum(jnp.where(causal, g_mat, 0.0), axis=2, keepdims=True)
    gl_col = jnp.sum(jnp.where(same, g_mat, 0.0), axis=2, keepdims=True)
    gc_row = jnp.sum(jnp.where(same & (ii <= jj), jnp.broadcast_to(g_col, (S, rn, rn)), 0.0), axis=1, keepdims=True)
    beta_col = jnp.sum(jnp.where(eye, jnp.broadcast_to(beta_row, (S, rn, rn)), 0.0), axis=2, keepdims=True)
    decay = jnp.exp(jnp.where(causal, gc_col - gc_row, -jnp.inf))
    eg = jnp.exp(gc_col)
    kb = k_st * beta_col
    both = _dot_nt(jnp.concatenate([kb, q_st], axis=1).astype(BF), k_st.astype(BF))
    n_low = jnp.where(same & (jj < ii), both[:, :rn] * decay, 0.0)
    t_inv = _tri_inverse(n_low, ii, jj, Cp, nil).astype(BF)
    uw = _dot(t_inv, jnp.concatenate([v_st * beta_col, kb * eg], axis=2).astype(BF))
    u, w = uw[:, :, :dv], uw[:, :, dv:]
    qe = q_st * eg
    k_dec = (k_st * jnp.exp(gl_col - gc_col)).astype(BF)
    v_new, q_s = [], []
    for h in range(G):
        r = slice(h * Cp, (h + 1) * Cp)
        s_old = jnp.stack([s_scr[s * G + h] for s in range(S)], axis=0)
        ws_qs = _dot(jnp.concatenate([w[:, r], qe[:, r]], axis=1).astype(BF), s_old.astype(BF))
        v_new_h = u[:, r] - ws_qs[:, :Cp]
        q_s.append(ws_qs[:, Cp:])
        v_new.append(v_new_h)
        s_new = s_old * jnp.exp(gl_col[:, h * Cp:h * Cp + 1]) + _dot_tn(k_dec[:, r], v_new_h.astype(BF))
        for s in range(S):
            s_scr[s * G + h] = s_new[s]
    v_new = jnp.concatenate(v_new, axis=1).astype(BF)
    return jnp.concatenate(q_s, axis=1) + _dot((both[:, rn:] * decay).astype(BF), v_new)


def _gdn_kernel(*refs, C, Cp, G, NS, rep, DK, DV, has_state):
    q_ref, k_ref, v_ref, z_ref, b_ref, a_ref, alog_ref, dtb_ref, wq_ref, wk_ref, wv_ref, gn_ref = refs[:12]
    pos = 12
    if has_state:
        cq_ref, ck_ref, cv_ref, s0_ref = refs[pos:pos + 4]
        pos += 4
    o_ref, so_ref = refs[pos:pos + 2]
    s_scr, cbq, cbk, cbv = refs[pos + 2:]

    c = pl.program_id(2)
    last = pl.num_programs(2) - 1
    tail_lo = SUBLANES - (CONV_W - 1)

    @pl.when(c == 0)
    def _():
        for cb in (cbq, cbk, cbv):
            cb[...] = jnp.zeros_like(cb)
        if has_state:
            cbq[tail_lo:SUBLANES, :] = cq_ref[...]
            cbk[tail_lo:SUBLANES, :] = ck_ref[...]
            cbv[tail_lo:SUBLANES, :] = cv_ref[...]
            s_scr[...] = s0_ref[...]
        else:
            s_scr[...] = jnp.zeros_like(s_scr)

    def conv(cb, x_ref, w_ref):
        cb[SUBLANES:SUBLANES + C, :] = x_ref[...]
        acc = cb[tail_lo:tail_lo + Cp, :] * w_ref[0:1, :]
        for j in range(1, CONV_W):
            acc = acc + cb[tail_lo + j:tail_lo + j + Cp, :] * w_ref[j:j + 1, :]
        cb[tail_lo:SUBLANES, :] = cb[C + tail_lo:C + SUBLANES, :]
        return _silu(acc)

    qc = conv(cbq, q_ref, wq_ref)
    kc = conv(cbk, k_ref, wk_ref)
    vc = conv(cbv, v_ref, wv_ref)

    padded = Cp != C
    if padded:
        row_ok = lax.broadcasted_iota(jnp.int32, (Cp, 1), 0) < C
    q_parts, k_parts, v_parts = [], [], []
    for i in range(G // rep):
        qh = qc[:, i * DK:(i + 1) * DK]
        kh = kc[:, i * DK:(i + 1) * DK]
        qh = qh * lax.rsqrt(jnp.sum(qh * qh, axis=-1, keepdims=True) + EPS) * (DK ** -0.5)
        kh = kh * lax.rsqrt(jnp.sum(kh * kh, axis=-1, keepdims=True) + EPS)
        if padded:
            kh = jnp.where(row_ok, kh, 0.0)
        for e in range(rep):
            hl = i * rep + e
            vh = vc[:, hl * DV:(hl + 1) * DV]
            if padded:
                vh = jnp.where(row_ok, vh, 0.0)
            q_parts.append(qh)
            k_parts.append(kh)
            v_parts.append(vh)
    beta_row = jax.nn.sigmoid(b_ref[c])
    sp_in = a_ref[c] + dtb_ref[...]
    softplus = jnp.maximum(sp_in, 0.0) + jnp.log1p(jnp.exp(-jnp.abs(sp_in)))
    g_row = -jnp.exp(alog_ref[...]) * softplus
    if padded:
        lane_ok = (lax.broadcasted_iota(jnp.int32, (1, G * Cp), 1) & (Cp - 1)) < C
        beta_row = jnp.where(lane_ok, beta_row, 0.0)
        g_row = jnp.where(lane_ok, g_row, 0.0)

    GS = G // NS
    rn = GS * Cp

    def stacks(parts):
        return jnp.stack([jnp.concatenate(parts[s * GS:(s + 1) * GS], axis=0) for s in range(NS)], axis=0)

    def stack_lanes(row):
        return jnp.stack([row[:, s * rn:(s + 1) * rn] for s in range(NS)], axis=0)

    o = _delta_chunk(stacks(q_parts), stacks(k_parts), stacks(v_parts), stack_lanes(beta_row), stack_lanes(g_row),
                     s_scr, G=GS, Cp=Cp, nil=min(C, TRI_BLOCK))
    on = o * lax.rsqrt(jnp.mean(o * o, axis=-1, keepdims=True) + EPS) * gn_ref[...]
    for s in range(NS):
        for h in range(GS):
            hl = s * GS + h
            zg = z_ref[:, hl * DV:(hl + 1) * DV]
            o_ref[:, hl * DV:(hl + 1) * DV] = (on[s, h * Cp:h * Cp + C] * _silu(zg)).astype(o_ref.dtype)

    @pl.when(c == last)
    def _():
        so_ref[...] = s_scr[...]


def gdn_scan(proj, b_logit, a_logit, conv_prev, s0, w_conv, a_log, dt_bias, g_norm, *, B, L, HQK, HV, DK, DV, name):
    rep = HV // HQK
    KEY, VAL = HQK * DK, HV * DV
    C = _pick(L, (GDN_CHUNK,))
    n = L // C
    Cp = -(-C // BF16_ROWS) * BF16_ROWS
    assert Cp & (Cp - 1) == 0 and L >= CONV_W - 1
    GS = min(HV, max(rep, STACK_ROWS // Cp))
    NS = _pick(HV // GS, (STACKS_PER_STEP, 2, 1))
    G = NS * GS
    assert HV % G == 0 and GS % rep == 0
    has_state = s0 is not None
    qw, vw = (G // rep) * DK, G * DV
    k_off, v_off, z_off = KEY // qw, 2 * KEY // vw, (2 * KEY + VAL) // vw

    def stack_rows(t):
        t = t.reshape(B, n, C, HV // G, G)
        t = jnp.pad(t, ((0, 0), (0, 0), (0, Cp - C), (0, 0), (0, 0)))
        return jnp.transpose(t, (0, 3, 1, 4, 2)).reshape(B, HV // G, n, 1, G * Cp)

    def stack_heads(v):
        return jnp.repeat(v.reshape(HV // G, 1, G), Cp, axis=-1)

    gate_spec = pl.BlockSpec((None, None, n, 1, G * Cp), lambda b, h, c: (b, h, 0, 0, 0))
    head_spec = pl.BlockSpec((None, 1, G * Cp), lambda b, h, c: (h, 0, 0))
    in_specs = [
        pl.BlockSpec((None, C, qw), lambda b, h, c: (b, c, h)),
        pl.BlockSpec((None, C, qw), lambda b, h, c: (b, c, k_off + h)),
        pl.BlockSpec((None, C, vw), lambda b, h, c: (b, c, v_off + h)),
        pl.BlockSpec((None, C, vw), lambda b, h, c: (b, c, z_off + h)),
        gate_spec, gate_spec, head_spec, head_spec,
        pl.BlockSpec((CONV_W, qw), lambda b, h, c: (0, h)),
        pl.BlockSpec((CONV_W, qw), lambda b, h, c: (0, k_off + h)),
        pl.BlockSpec((CONV_W, vw), lambda b, h, c: (0, v_off + h)),
        pl.BlockSpec((1, DV), lambda b, h, c: (0, 0)),
    ]
    args = [proj, proj, proj, proj, stack_rows(b_logit), stack_rows(a_logit), stack_heads(a_log),
            stack_heads(dt_bias), w_conv, w_conv, w_conv, g_norm.reshape(1, DV)]
    if has_state:
        in_specs += [
            pl.BlockSpec((None, CONV_W - 1, qw), lambda b, h, c: (b, 0, h)),
            pl.BlockSpec((None, CONV_W - 1, qw), lambda b, h, c: (b, 0, k_off + h)),
            pl.BlockSpec((None, CONV_W - 1, vw), lambda b, h, c: (b, 0, v_off + h)),
            pl.BlockSpec((None, G, DK, DV), lambda b, h, c: (b, h, 0, 0)),
        ]
        args += [conv_prev, conv_prev, conv_prev, s0]
    body = functools.partial(_gdn_kernel, C=C, Cp=Cp, G=G, NS=NS, rep=rep, DK=DK, DV=DV, has_state=has_state)
    return pl.pallas_call(
        body,
        grid=(B, HV // G, n),
        in_specs=in_specs,
        out_specs=[pl.BlockSpec((None, C, vw), lambda b, h, c: (b, c, h)),
                   pl.BlockSpec((None, G, DK, DV), lambda b, h, c: (b, h, 0, 0))],
        out_shape=[jax.ShapeDtypeStruct((B, L, VAL), BF), jax.ShapeDtypeStruct((B, HV, DK, DV), F32)],
        scratch_shapes=[pltpu.VMEM((G, DK, DV), F32),
                        pltpu.VMEM((SUBLANES + Cp, qw), F32),
                        pltpu.VMEM((SUBLANES + Cp, qw), F32),
                        pltpu.VMEM((SUBLANES + Cp, vw), F32)],
        compiler_params=_params(3),
        name=name,
    )(*args)


def _kv_kernel(x_ref, g_ref, sh_ref, sc_ref, w_ref, gkv_ref, cos_ref, sin_ref,
               ckv_ref, kpe_ref, kcat_ref, h_ref, *, rows, R, P):
    _modulate_rows(x_ref, g_ref, sh_ref, sc_ref, h_ref, rows)
    y = _dot(h_ref[...], w_ref[...].astype(BF))
    c = y[:, :R]
    ckv = c * lax.rsqrt(jnp.mean(c * c, axis=-1, keepdims=True) + EPS) * gkv_ref[...]
    rot = y[:, R:R + LANES] * cos_ref[...] + y[:, R + LANES:R + 2 * LANES] * sin_ref[...]
    ckv_ref[...] = ckv
    kpe_ref[...] = rot[:, :P]
    kcat_ref[:, :R] = ckv.astype(BF)
    kcat_ref[:, R:] = rot.astype(BF)


def _rope_weight_cols(w_pe):
    K, P = w_pe.shape
    half = P // 2
    zeros = jnp.zeros((K, LANES - P), w_pe.dtype)
    swapped = jnp.concatenate([w_pe[:, half:], w_pe[:, :half]], axis=1)
    return jnp.concatenate([w_pe, zeros, swapped, zeros], axis=1)


def shared_kv(x, g, shift, scale, w_down, g_kv, cos_t, sin_t, rows_per_group, *, R, P, name):
    M, K = x.shape
    tm = _row_tile(M, rows_per_group, shift.ndim == 3)
    tm = min(tm, 512)
    rows = _pick(tm, (128, 64, 32, 16, 8))
    w_ext = jnp.concatenate([w_down[:, :R], _rope_weight_cols(w_down[:, R:])], axis=1)
    NW = R + 2 * LANES
    mod_spec = _mod_specs(shift, K, tm, rows_per_group)
    row = lambda i: (i, 0)
    fixed = lambda i: (0, 0)
    return pl.pallas_call(
        functools.partial(_kv_kernel, rows=rows, R=R, P=P),
        grid=(M // tm,),
        in_specs=[pl.BlockSpec((tm, K), row), pl.BlockSpec((1, K), fixed), mod_spec, mod_spec,
                  pl.BlockSpec((K, NW), fixed), pl.BlockSpec((1, R), fixed),
                  pl.BlockSpec((tm, LANES), row), pl.BlockSpec((tm, LANES), row)],
        out_specs=[pl.BlockSpec((tm, R), row), pl.BlockSpec((tm, P), row), pl.BlockSpec((tm, R + LANES), row)],
        out_shape=[jax.ShapeDtypeStruct((M, R), F32), jax.ShapeDtypeStruct((M, P), F32),
                   jax.ShapeDtypeStruct((M, R + LANES), BF)],
        scratch_shapes=[pltpu.VMEM((tm, K), BF)],
        compiler_params=_params(1),
        name=name,
    )(x, g.reshape(1, K), shift, scale, w_ext, g_kv.reshape(1, R), cos_t, sin_t)


def _q_kernel(x_ref, g_ref, sh_ref, sc_ref, wdq_ref, gq_ref, wuq_ref, wuk_ref, cos_ref, sin_ref,
              o_ref, cq_ref, h_ref, *, rows, NOPE, R):
    @pl.when(pl.program_id(1) == 0)
    def _():
        _modulate_rows(x_ref, g_ref, sh_ref, sc_ref, h_ref, rows)
        c = _dot(h_ref[...], wdq_ref[...].astype(BF))
        cq_ref[...] = (c * lax.rsqrt(jnp.mean(c * c, axis=-1, keepdims=True) + EPS) * gq_ref[...]).astype(BF)

    qf = _dot(cq_ref[...], wuq_ref[...].astype(BF))
    q_lat = _dot_nt(qf[:, :NOPE].astype(BF), wuk_ref[...].astype(BF))
    rot = qf[:, NOPE:NOPE + LANES] * cos_ref[...] + qf[:, NOPE + LANES:NOPE + 2 * LANES] * sin_ref[...]
    o_ref[:, :R] = q_lat.astype(o_ref.dtype)
    o_ref[:, R:] = rot.astype(o_ref.dtype)


def mla_queries(x, g, shift, scale, w_dq, g_q, w_uq, w_uk, cos_t, sin_t, rows_per_group, *,
                H, NOPE, P, R, head_major, name):
    M, K = x.shape
    QL = w_dq.shape[1]
    tm = _row_tile(M, rows_per_group, True) if head_major else _row_tile(M, rows_per_group, shift.ndim == 3)
    tm = min(tm, 512)
    rows = _pick(tm, (128, 64, 32, 16, 8))
    w_heads = w_uq.reshape(QL, H, NOPE + P)
    w_uq_r = jnp.concatenate(
        [jnp.concatenate([w_heads[:, h, :NOPE], _rope_weight_cols(w_heads[:, h, NOPE:])], axis=1)[None]
         for h in range(H)], axis=0)
    NQ = NOPE + 2 * LANES
    W = R + LANES
    mod_spec = _mod_specs(shift, K, tm, rows_per_group)
    row = lambda i, h: (i, 0)
    fixed = lambda i, h: (0, 0)
    if head_major:
        tiles = rows_per_group // tm
        out_spec = pl.BlockSpec((None, None, tm, W), lambda i, h: (i // tiles, h, i % tiles, 0))
        out_shape = jax.ShapeDtypeStruct((M // rows_per_group, H, rows_per_group, W), BF)
    else:
        out_spec = pl.BlockSpec((tm, W), lambda i, h: (i, h))
        out_shape = jax.ShapeDtypeStruct((M, H * W), BF)
    return pl.pallas_call(
        functools.partial(_q_kernel, rows=rows, NOPE=NOPE, R=R),
        grid=(M // tm, H),
        in_specs=[pl.BlockSpec((tm, K), row), pl.BlockSpec((1, K), fixed), mod_spec, mod_spec,
                  pl.BlockSpec((K, QL), fixed), pl.BlockSpec((1, QL), fixed),
                  pl.BlockSpec((None, QL, NQ), lambda i, h: (h, 0, 0)),
                  pl.BlockSpec((R, NOPE), lambda i, h: (0, h)),
                  pl.BlockSpec((tm, LANES), row), pl.BlockSpec((tm, LANES), row)],
        out_specs=out_spec,
        out_shape=out_shape,
        scratch_shapes=[pltpu.VMEM((tm, QL), BF), pltpu.VMEM((tm, K), BF)],
        compiler_params=_params(2),
        name=name,
    )(x, g.reshape(1, K), shift, scale, w_dq, g_q.reshape(1, QL), w_uq_r, w_uk, cos_t, sin_t)


def _attn_prompt_kernel(q_ref, k_ref, o_ref, m_scr, l_scr, acc_scr, *, tq, tk, H, R, scale):
    qi = pl.program_id(1)
    kj = pl.program_id(2)

    @pl.when(kj == 0)
    def _():
        m_scr[...] = jnp.full_like(m_scr, -jnp.inf)
        l_scr[...] = jnp.zeros_like(l_scr)
        acc_scr[...] = jnp.zeros_like(acc_scr)

    @pl.when(kj * tk <= qi * tq + tq - 1)
    def _():
        q = q_ref[...].reshape(H * tq, q_ref.shape[-1])
        k = k_ref[...]
        s = _dot_nt(q, k) * scale
        qpos = qi * tq + lax.rem(lax.broadcasted_iota(jnp.int32, (H * tq, tk), 0), tq)
        kpos = kj * tk + lax.broadcasted_iota(jnp.int32, (H * tq, tk), 1)
        s = jnp.where(kpos <= qpos, s, -jnp.inf)
        m_old = m_scr[...]
        m_new = jnp.maximum(m_old, jnp.max(s, axis=-1, keepdims=True))
        alpha = jnp.exp(m_old - m_new)
        p = jnp.exp(s - m_new)
        l_scr[...] = alpha * l_scr[...] + jnp.sum(p, axis=-1, keepdims=True)
        acc_scr[...] = alpha * acc_scr[...] + _dot(p.astype(BF), k[:, :R])
        m_scr[...] = m_new

    @pl.when(kj == pl.num_programs(2) - 1)
    def _():
        o = acc_scr[...] / l_scr[...]
        o_ref[...] = o.reshape(H, tq, R).astype(o_ref.dtype)


def attention_prompt(q4, kcat, *, R, scale, name):
    B, H, L, W = q4.shape
    tq = _pick(L, (128, 64, 32, 16))
    tk = _pick(L, (512, 256, 128, 64, 32, 16))
    nq, nk = L // tq, L // tk

    def k_index(b, i, j):
        return (b, jnp.minimum(j, (i * tq + tq - 1) // tk), 0)

    return pl.pallas_call(
        functools.partial(_attn_prompt_kernel, tq=tq, tk=tk, H=H, R=R, scale=scale),
        grid=(B, nq, nk),
        in_specs=[pl.BlockSpec((None, H, tq, W), lambda b, i, j: (b, 0, i, 0)),
                  pl.BlockSpec((None, tk, W), k_index)],
        out_specs=pl.BlockSpec((None, H, tq, R), lambda b, i, j: (b, 0, i, 0)),
        out_shape=jax.ShapeDtypeStruct((B, H, L, R), BF),
        scratch_shapes=[pltpu.VMEM((H * tq, 1), F32), pltpu.VMEM((H * tq, 1), F32),
                        pltpu.VMEM((H * tq, R), F32)],
        compiler_params=_params(3),
        name=name,
    )(q4, kcat)


def _attn_sample_kernel(pt_ref, q_ref, knew_ref, *refs, NP, T, H, R, P, scale):
    ckv_refs = refs[:NP]
    kpe_refs = refs[NP:2 * NP]
    o_ref, m_scr, l_scr, acc_scr = refs[2 * NP:]
    j = pl.program_id(1)
    rows = T * H
    q = q_ref[...]
    q_lat = q[:, :R]
    q_pe = q[:, R:R + P]

    def update(s, values):
        m_old = m_scr[...]
        m_new = jnp.maximum(m_old, jnp.max(s, axis=-1, keepdims=True))
        alpha = jnp.exp(m_old - m_new)
        p = jnp.exp(s - m_new)
        l_scr[...] = alpha * l_scr[...] + jnp.sum(p, axis=-1, keepdims=True)
        acc_scr[...] = alpha * acc_scr[...] + values(p.astype(BF))
        m_scr[...] = m_new

    @pl.when(j == 0)
    def _():
        m_scr[...] = jnp.full_like(m_scr, -jnp.inf)
        l_scr[...] = jnp.zeros_like(l_scr)
        acc_scr[...] = jnp.zeros_like(acc_scr)
        tp = -(-T // BF16_ROWS) * BF16_ROWS
        knew = jnp.concatenate([knew_ref[...], jnp.zeros((tp - T, knew_ref.shape[1]), BF)], axis=0)
        s = _dot_nt(q, knew) * scale
        qt = lax.broadcasted_iota(jnp.int32, (rows, tp), 0) // H
        kt = lax.broadcasted_iota(jnp.int32, (rows, tp), 1)
        s = jnp.where(kt <= qt, s, -jnp.inf)
        update(s, lambda p: _dot(p, knew[:, :R]))

    pages = [ckv_refs[i][...].astype(BF) for i in range(NP)]
    s = jnp.concatenate(
        [_dot_nt(q_lat, pages[i]) + _dot(q_pe, kpe_refs[i][...].astype(BF)) for i in range(NP)], axis=1) * scale
    page = pages[0].shape[0]

    def values(p):
        out = _dot(p[:, :page], pages[0])
        for i in range(1, NP):
            out = out + _dot(p[:, i * page:(i + 1) * page], pages[i])
        return out

    update(s, values)

    @pl.when(j == pl.num_programs(1) - 1)
    def _():
        o_ref[...] = (acc_scr[...] / l_scr[...]).astype(o_ref.dtype)


def attention_sample(q3, knew, cache_ckv, cache_kpe_t, page_table, *, T, H, R, P, scale, name):
    Bd, rows, W = q3.shape
    n_pages = page_table.shape[1]
    page = cache_ckv.shape[1]
    NP = _pick(n_pages, (16, 8, 4, 2, 1))

    def ckv_spec(i):
        return pl.BlockSpec((None, page, R), lambda b, j, pt: (pt[b, j * NP + i], 0, 0))

    def kpe_spec(i):
        return pl.BlockSpec((None, P, page), lambda b, j, pt: (pt[b, j * NP + i], 0, 0))

    grid_spec = pltpu.PrefetchScalarGridSpec(
        num_scalar_prefetch=1,
        grid=(Bd, n_pages // NP),
        in_specs=[pl.BlockSpec((None, rows, W), lambda b, j, pt: (b, 0, 0)),
                  pl.BlockSpec((None, T, W), lambda b, j, pt: (b, 0, 0))]
                 + [ckv_spec(i) for i in range(NP)] + [kpe_spec(i) for i in range(NP)],
        out_specs=pl.BlockSpec((None, rows, R), lambda b, j, pt: (b, 0, 0)),
        scratch_shapes=[pltpu.VMEM((rows, 1), F32), pltpu.VMEM((rows, 1), F32), pltpu.VMEM((rows, R), F32)],
    )
    return pl.pallas_call(
        functools.partial(_attn_sample_kernel, NP=NP, T=T, H=H, R=R, P=P, scale=scale),
        grid_spec=grid_spec,
        out_shape=jax.ShapeDtypeStruct((Bd, rows, R), BF),
        compiler_params=_params(2),
        name=name,
    )(page_table, q3, knew, *([cache_ckv] * NP), *([cache_kpe_t] * NP))


def _mm_kernel(a_ref, w_ref, o_ref):
    o_ref[...] = _dot(a_ref[...], w_ref[...].astype(BF)).astype(o_ref.dtype)


def value_up(o_lat, w_uv, *, H, R, V, head_major, name):
    if head_major:
        G, _, L, _ = o_lat.shape
        M = G * L
        tm = _pick(L, (1024, 512, 256, 128, 64, 32, 16, 8))
        tiles = L // tm
        a_spec = pl.BlockSpec((None, None, tm, R), lambda i, h: (i // tiles, h, i % tiles, 0))
    else:
        M = o_lat.shape[0]
        tm = _pick(M, (1024, 512, 256, 128, 64, 32, 16, 8))
        a_spec = pl.BlockSpec((tm, R), lambda i, h: (i, h))
    return pl.pallas_call(
        _mm_kernel,
        grid=(M // tm, H),
        in_specs=[a_spec, pl.BlockSpec((R, V), lambda i, h: (0, h))],
        out_specs=pl.BlockSpec((tm, V), lambda i, h: (i, h)),
        out_shape=jax.ShapeDtypeStruct((M, H * V), BF),
        compiler_params=_params(2),
        name=name,
    )(o_lat, w_uv)


def _rope_tables(pos, P, reps):
    half = P // 2
    inv = ROPE_THETA ** (-jnp.arange(half, dtype=F32) / half)
    ang = pos.astype(F32)[:, None] * inv[None, :]
    cos, sin = jnp.cos(ang), jnp.sin(ang)
    zeros = jnp.zeros((pos.shape[0], LANES - P), F32)
    cos_t = jnp.concatenate([cos, cos, zeros], axis=1)
    sin_t = jnp.concatenate([-sin, sin, zeros], axis=1)
    return jnp.tile(cos_t, (reps, 1)), jnp.tile(sin_t, (reps, 1))


def _trunk(x3, mods, pos, conv_in, ssm_in, past, p, tag):
    G, L, D = x3.shape
    M = G * L
    grouped = L % SUBLANES == 0
    x = x3.reshape(M, D)

    def split_mods(m, n):
        parts = jnp.split(m, n, axis=-1)
        if grouped:
            return [t[:, None, :] for t in parts]
        return [jnp.repeat(t, L, axis=0) for t in parts]

    HV, DK, DV = p['HV'], p['DK'], p['DV']
    HQK = p['HQK']
    KEY, VAL = HQK * DK, HV * DV
    H, NOPE, P, R, V = p['H'], p['NOPE'], p['P'], p['R'], p['V']
    FF = p['w_down'].shape[1]
    cos_t, sin_t = _rope_tables(pos, P, G)
    scale = (NOPE + P) ** -0.5

    sh1, sc1, gt1, sh2, sc2, gt2 = split_mods(mods['l0'], 6)
    w_in_t = p['gdn_w_in_t']
    n_proj = 2 * KEY + 2 * VAL
    proj = mod_matmul(x, p['g_mix'][0], sh1, sc1, w_in_t, 0, n_proj, L, w_is_nk=True, name=f'{tag}_gdn_in')
    w_ba_t = jnp.concatenate([w_in_t[0, n_proj:], jnp.zeros((LANES - 2 * HV, D), F32)], axis=0)
    ba = mod_matmul(x, p['g_mix'][0], sh1, sc1, w_ba_t, 0, LANES, L, w_is_nk=True, name=f'{tag}_gdn_ba')
    proj3 = proj.reshape(G, L, n_proj)
    o_g, ssm_new = gdn_scan(
        proj3, ba[:, :HV], ba[:, HV:2 * HV], conv_in, ssm_in, p['gdn_w_conv'][0], p['gdn_a_log'][0],
        p['gdn_dt_bias'][0], p['gdn_g_norm'][0], B=G, L=L, HQK=HQK, HV=HV, DK=DK, DV=DV, name=f'{tag}_gdn_scan')
    conv_new = proj3[:, L - (CONV_W - 1):, :2 * KEY + VAL]
    x = matmul_residual(o_g.reshape(M, VAL), p['gdn_w_out'], 0, x, gt1, L, name=f'{tag}_gdn_out')
    hff = mod_matmul(x, p['g_ffn'][0], sh2, sc2, p['w_gate_up'], 0, FF, L, swiglu=True, out_dtype=BF,
                     name=f'{tag}_ffn0_up')
    x = matmul_residual(hff, p['w_down'], 0, x, gt2, L, name=f'{tag}_ffn0_down')

    shk, sck = split_mods(mods['kv'], 2)
    ckv, kpe, kcat = shared_kv(x, p['kv_g_in'], shk, sck, p['kv_w_down'], p['kv_g_norm'], cos_t, sin_t, L,
                               R=R, P=P, name=f'{tag}_kv')

    sh1, sc1, gt1, sh2, sc2, gt2 = split_mods(mods['l1'], 6)
    head_major = past is None
    q = mla_queries(x, p['g_mix'][1], sh1, sc1, p['mla_w_dq'][0], p['mla_g_q'][0], p['mla_w_uq'][0], p['kv_w_uk'],
                    cos_t, sin_t, L, H=H, NOPE=NOPE, P=P, R=R, head_major=head_major, name=f'{tag}_q')
    if head_major:
        o_lat = attention_prompt(q, kcat.reshape(G, L, R + LANES), R=R, scale=scale, name=f'{tag}_attn')
    else:
        cache_ckv, cache_kpe, page_table = past
        o_lat = attention_sample(q.reshape(G, L * H, R + LANES), kcat.reshape(G, L, R + LANES), cache_ckv,
                                 cache_kpe, page_table, T=L, H=H, R=R, P=P, scale=scale, name=f'{tag}_attn')
        o_lat = o_lat.reshape(M, H * R)
    o = value_up(o_lat, p['kv_w_uv'], H=H, R=R, V=V, head_major=head_major, name=f'{tag}_uv')
    x = matmul_residual(o, p['mla_w_o'], 0, x, gt1, L, name=f'{tag}_attn_out')
    hff = mod_matmul(x, p['g_ffn'][1], sh2, sc2, p['w_gate_up'], 1, FF, L, swiglu=True, out_dtype=BF,
                     name=f'{tag}_ffn1_up')
    x = matmul_residual(hff, p['w_down'], 1, x, gt2, L, name=f'{tag}_ffn1_down')

    shf, scf = split_mods(mods['final'], 2)
    y = modulate_rows(x, p['final_g'], shf, scf, L, name=f'{tag}_final')
    return (y.reshape(G, L, D), conv_new[None], ssm_new[None], ckv.reshape(G, L, R), kpe.reshape(G, L, P))


def kernel(x_prompt, x_sample, c_prompt, c_sample, cache_ckv, cache_kpe, page_table, state_ssm, state_conv, w_ada, b_ada, g_mix, g_ffn, w_gate_up, w_down, gdn_w_in, gdn_w_conv, gdn_a_log, gdn_dt_bias, gdn_g_norm, gdn_w_out, kv_w_ada, kv_b_ada, kv_g_in, kv_w_down, kv_g_norm, kv_w_uk, kv_w_uv, mla_w_dq, mla_g_q, mla_w_uq, mla_w_o, final_w_ada, final_b_ada, final_g):
    B, S, D = x_prompt.shape
    Bd, T, _ = x_sample.shape
    HV, DK, DV = state_ssm.shape[2:]
    KEY = (state_conv.shape[-1] - HV * DV) // 2
    R = cache_ckv.shape[-1]
    P = cache_kpe.shape[-1]
    QL = mla_w_dq.shape[-1]
    nope_total = kv_w_uk.shape[1]
    H = (mla_w_uq.shape[-1] - nope_total) // P
    p = dict(w_ada=w_ada, b_ada=b_ada, g_mix=g_mix, g_ffn=g_ffn, w_gate_up=w_gate_up, w_down=w_down,
             gdn_w_in_t=jnp.swapaxes(gdn_w_in, 1, 2), gdn_w_conv=gdn_w_conv, gdn_a_log=gdn_a_log, gdn_dt_bias=gdn_dt_bias,
             gdn_g_norm=gdn_g_norm, gdn_w_out=gdn_w_out, kv_g_in=kv_g_in, kv_w_down=kv_w_down,
             kv_g_norm=kv_g_norm, kv_w_uk=kv_w_uk, kv_w_uv=kv_w_uv, mla_w_dq=mla_w_dq, mla_g_q=mla_g_q,
             mla_w_uq=mla_w_uq, mla_w_o=mla_w_o, final_g=final_g,
             HV=HV, DK=DK, DV=DV, HQK=KEY // DK, H=H, NOPE=nope_total // H, P=P, R=R,
             V=kv_w_uv.shape[1] // H)

    n_c = B + Bd
    pad = -n_c % SUBLANES
    c_all = jnp.concatenate([c_prompt, c_sample, jnp.zeros((pad, D), F32)], axis=0)
    m_l0 = ada_dense(c_all, w_ada, b_ada, 0, name='ada_l0')
    m_l1 = ada_dense(c_all, w_ada, b_ada, 1, name='ada_l1')
    m_kv = ada_dense(c_all, kv_w_ada, kv_b_ada, 0, name='ada_kv')
    m_f = ada_dense(c_all, final_w_ada, final_b_ada, 0, name='ada_final')

    def mods(lo, hi):
        return dict(l0=m_l0[lo:hi], l1=m_l1[lo:hi], kv=m_kv[lo:hi], final=m_f[lo:hi])

    y_p, conv_p, ssm_p, ckv_p, kpe_p = _trunk(x_prompt, mods(0, B), jnp.arange(S), None, None, None, p, 'p')
    past_len = page_table.shape[1] * cache_ckv.shape[1]
    y_s, conv_s, ssm_s, ckv_s, kpe_s = _trunk(x_sample, mods(B, n_c), past_len + jnp.arange(T), state_conv[0],
                                              state_ssm[0], (cache_ckv, jnp.swapaxes(cache_kpe, 1, 2), page_table),
                                              p, 's')
    return (y_p, y_s, ssm_p, conv_p, ckv_p, kpe_p, ssm_s, conv_s, ckv_s, kpe_s)
```

```python
import functools

import jax
import jax.numpy as jnp
from jax import lax
from jax.experimental import pallas as pl
from jax.experimental.pallas import tpu as pltpu

EPS = 1e-6
ROPE_THETA = 10000.0
CONV_W = 4
GDN_CHUNK = 64
F32 = jnp.float32
BF = jnp.bfloat16

V7X_VMEM_BYTES = 64 * 1024 * 1024
VMEM_LIMIT = V7X_VMEM_BYTES - 8 * 1024 * 1024
LANES = 128
SUBLANES = 8
BF16_ROWS = 16
TRI_BLOCK = 16
STACK_ROWS = 256
STACKS_PER_STEP = 4
ATTN_GROUP_ROWS = 256
ATTN_SOFTMAX_ROWS = 64
ATTN_PAGE_GROUP = 4
LOG2_E = 1.4426950408889634


def _params(n_axes):
    return pltpu.CompilerParams(dimension_semantics=("arbitrary",) * n_axes, vmem_limit_bytes=VMEM_LIMIT)


def _pick(n, cands):
    for c in cands:
        if n % c == 0:
            return c
    return n


def _silu(x):
    return x * jax.nn.sigmoid(x)


def _contract(a, b, ca, cb):
    batch = tuple(range(a.ndim - 2))
    dims = (((a.ndim - 2 + ca,), (b.ndim - 2 + cb,)), (batch, batch))
    return lax.dot_general(a, b, dims, preferred_element_type=F32)


def _dot(a, b):
    return _contract(a, b, 1, 0)


def _dot_nt(a, b):
    return _contract(a, b, 1, 1)


def _dot_tn(a, b):
    return _contract(a, b, 0, 0)


def _split_bf16(a):
    hi = a.astype(BF)
    lo = (a - hi.astype(F32)).astype(BF)
    return hi, lo


def _dot3(a, b):
    ah, al = _split_bf16(a)
    bh, bl = _split_bf16(b)
    return _dot(ah, bh) + _dot(ah, bl) + _dot(al, bh)


def _modulate_rows(x_ref, g_ref, sh_ref, sc_ref, h_ref, rows):
    tm = x_ref.shape[0]
    per_row = sh_ref.shape[0] != 1

    def body(r, carry):
        sl = pl.ds(pl.multiple_of(r * rows, rows), rows)
        x = x_ref[sl, :]
        y = x * lax.rsqrt(jnp.mean(x * x, axis=-1, keepdims=True) + EPS) * g_ref[...]
        sc = sc_ref[sl, :] if per_row else sc_ref[...]
        sh = sh_ref[sl, :] if per_row else sh_ref[...]
        h_ref[sl, :] = (y * (1.0 + sc) + sh).astype(h_ref.dtype)
        return carry

    lax.fori_loop(0, tm // rows, body, 0)


def _mod_specs(shift, K, tm, rows_per_group):
    if shift.ndim == 3:
        tiles_per_group = rows_per_group // tm
        return pl.BlockSpec((None, 1, K), lambda i, *_: (i // tiles_per_group, 0, 0))
    return pl.BlockSpec((tm, K), lambda i, *_: (i, 0))


def _w_spec(w, layer, K, tn, col_off_blocks=0):
    if w.ndim == 3:
        return pl.BlockSpec((None, K, tn), lambda i, j: (layer, 0, j + col_off_blocks))
    return pl.BlockSpec((K, tn), lambda i, j: (0, j + col_off_blocks))


def _modmm_kernel(x_ref, g_ref, sh_ref, sc_ref, w_ref, o_ref, h_ref, *, rows, w_is_nk):
    @pl.when(pl.program_id(1) == 0)
    def _():
        _modulate_rows(x_ref, g_ref, sh_ref, sc_ref, h_ref, rows)

    dot = _dot_nt if w_is_nk else _dot
    o_ref[...] = dot(h_ref[...], w_ref[...].astype(BF)).astype(o_ref.dtype)


def _modmm_swiglu_kernel(x_ref, g_ref, sh_ref, sc_ref, wg_ref, wu_ref, o_ref, h_ref, *, rows):
    @pl.when(pl.program_id(1) == 0)
    def _():
        _modulate_rows(x_ref, g_ref, sh_ref, sc_ref, h_ref, rows)

    h = h_ref[...]
    gate = _dot(h, wg_ref[...].astype(BF))
    up = _dot(h, wu_ref[...].astype(BF))
    o_ref[...] = (_silu(gate) * up).astype(o_ref.dtype)


def _row_tile(M, rows_per_group, grouped):
    base = rows_per_group if grouped else M
    return _pick(base, (1024, 512, 256, 128, 64, 32, 16, 8))


def mod_matmul(x, g, shift, scale, w, layer, n_out, rows_per_group, *, swiglu=False, w_is_nk=False,
               out_dtype=F32, name):
    M, K = x.shape
    tm = _row_tile(M, rows_per_group, shift.ndim == 3)
    tn = _pick(n_out, (256, 128)) if swiglu else _pick(n_out, (512, 256, 128))
    rows = _pick(tm, (128, 64, 32, 16, 8))
    mod_spec = _mod_specs(shift, K, tm, rows_per_group)
    if not w_is_nk:
        w_spec = _w_spec(w, layer, K, tn)
    elif w.ndim == 3:
        w_spec = pl.BlockSpec((None, tn, K), lambda i, j: (layer, j, 0))
    else:
        w_spec = pl.BlockSpec((tn, K), lambda i, j: (j, 0))
    in_specs = [pl.BlockSpec((tm, K), lambda i, j: (i, 0)),
                pl.BlockSpec((1, K), lambda i, j: (0, 0)),
                mod_spec, mod_spec, w_spec]
    args = [x, g.reshape(1, K), shift, scale, w]
    if swiglu:
        in_specs.append(_w_spec(w, layer, K, tn, n_out // tn))
        args.append(w)
        body = functools.partial(_modmm_swiglu_kernel, rows=rows)
    else:
        body = functools.partial(_modmm_kernel, rows=rows, w_is_nk=w_is_nk)
    return pl.pallas_call(
        body,
        grid=(M // tm, n_out // tn),
        in_specs=in_specs,
        out_specs=pl.BlockSpec((tm, tn), lambda i, j: (i, j)),
        out_shape=jax.ShapeDtypeStruct((M, n_out), out_dtype),
        scratch_shapes=[pltpu.VMEM((tm, K), BF)],
        compiler_params=_params(2),
        name=name,
    )(*args)


def _mmres_kernel(a_ref, w_ref, res_ref, gate_ref, o_ref):
    y = _dot(a_ref[...], w_ref[...].astype(BF))
    o_ref[...] = res_ref[...] + gate_ref[...] * y


def matmul_residual(a, w, layer, res, gate, rows_per_group, *, name):
    M, K = a.shape
    N = res.shape[1]
    tm = _row_tile(M, rows_per_group, gate.ndim == 3)
    tn = _pick(N, (256, 128)) if K > 4096 else _pick(N, (512, 256, 128))
    if gate.ndim == 3:
        tiles_per_group = rows_per_group // tm
        gate_spec = pl.BlockSpec((None, 1, tn), lambda i, j: (i // tiles_per_group, 0, j))
    else:
        gate_spec = pl.BlockSpec((tm, tn), lambda i, j: (i, j))
    return pl.pallas_call(
        _mmres_kernel,
        grid=(M // tm, N // tn),
        in_specs=[pl.BlockSpec((tm, K), lambda i, j: (i, 0)),
                  _w_spec(w, layer, K, tn),
                  pl.BlockSpec((tm, tn), lambda i, j: (i, j)),
                  gate_spec],
        out_specs=pl.BlockSpec((tm, tn), lambda i, j: (i, j)),
        out_shape=jax.ShapeDtypeStruct((M, N), F32),
        compiler_params=_params(2),
        name=name,
    )(a, w, res, gate)


def _ada_kernel(c_ref, w_ref, b_ref, o_ref):
    a = _silu(c_ref[...]).astype(BF)
    o_ref[...] = _dot(a, w_ref[...].astype(BF)) + b_ref[...]


def ada_dense(c, w, b, layer, *, name):
    M, K = c.shape
    N = w.shape[-1]
    tn = _pick(N, (512, 256, 128))
    if b.ndim == 2:
        b_spec = pl.BlockSpec((None, 1, tn), lambda i, j: (layer, 0, j))
        b = b.reshape(b.shape[0], 1, N)
    else:
        b_spec = pl.BlockSpec((1, tn), lambda i, j: (0, j))
        b = b.reshape(1, N)
    return pl.pallas_call(
        _ada_kernel,
        grid=(1, N // tn),
        in_specs=[pl.BlockSpec((M, K), lambda i, j: (0, 0)), _w_spec(w, layer, K, tn), b_spec],
        out_specs=pl.BlockSpec((M, tn), lambda i, j: (0, j)),
        out_shape=jax.ShapeDtypeStruct((M, N), F32),
        compiler_params=_params(2),
        name=name,
    )(c, w, b)


def _modulate_kernel(x_ref, g_ref, sh_ref, sc_ref, o_ref, *, rows):
    _modulate_rows(x_ref, g_ref, sh_ref, sc_ref, o_ref, rows)


def modulate_rows(x, g, shift, scale, rows_per_group, *, name):
    M, K = x.shape
    tm = _row_tile(M, rows_per_group, shift.ndim == 3)
    rows = _pick(tm, (128, 64, 32, 16, 8))
    mod_spec = _mod_specs(shift, K, tm, rows_per_group)
    return pl.pallas_call(
        functools.partial(_modulate_kernel, rows=rows),
        grid=(M // tm,),
        in_specs=[pl.BlockSpec((tm, K), lambda i: (i, 0)), pl.BlockSpec((1, K), lambda i: (0, 0)),
                  mod_spec, mod_spec],
        out_specs=pl.BlockSpec((tm, K), lambda i: (i, 0)),
        out_shape=jax.ShapeDtypeStruct((M, K), F32),
        compiler_params=_params(1),
        name=name,
    )(x, g.reshape(1, K), shift, scale)


def _dot1(a, b):
    return _dot(a.astype(BF), b.astype(BF))


def _tri_inverse(n_low, ii, jj, cp, nil):
    rn = n_low.shape[-1]
    eye = (ii == jj).astype(F32)
    base = min(TRI_BLOCK, cp)
    shift = base.bit_length() - 1
    nd = jnp.where((ii >> shift) == (jj >> shift), n_low, 0.0)
    p = eye - nd
    if nil > 2:
        npow = _dot1(nd, nd)
        pw = 2
        while 2 * pw - 1 < nil - 1:
            both = _dot1(jnp.concatenate([p, npow], axis=-2), npow)
            p = p + both[:, :rn]
            npow = both[:, rn:]
            pw *= 2
        p = p + _dot1(p, npow)
    size = base
    while size < cp:
        s = size.bit_length() - 1
        off = ((ii >> (s + 1)) == (jj >> (s + 1))) & (((ii >> s) & 1) == 1) & (((jj >> s) & 1) == 0)
        x = _dot1(jnp.where(off, n_low, 0.0), p)
        p = p - _dot1(p, x)
        size *= 2
    resid = (eye - p) - _dot3(n_low, p)
    return p + _dot1(p, resid)


def _delta_chunk(q_st, k_st, v_st, beta_row, g_row, s_scr, *, G, Cp, nil):
    S, rn, dv = v_st.shape
    ii = lax.broadcasted_iota(jnp.int32, (S, rn, rn), 1)
    jj = lax.broadcasted_iota(jnp.int32, (S, rn, rn), 2)
    sh = Cp.bit_length() - 1
    same = (ii >> sh) == (jj >> sh)
    eye = ii == jj
    causal = same & (jj <= ii)
    g_mat = jnp.broadcast_to(g_row, (S, rn, rn))
    g_col = jnp.sum(jnp.where(eye, g_mat, 0.0), axis=2, keepdims=True)
    gc_col = jnp.sum(jnp.where(causal, g_mat, 0.0), axis=2, keepdims=True)
    gl_col = jnp.sum(jnp.where(same, g_mat, 0.0), axis=2, keepdims=True)
    gc_row = jnp.sum(jnp.where(same & (ii <= jj), jnp.broadcast_to(g_col, (S, rn, rn)), 0.0), axis=1, keepdims=True)
    beta_col = jnp.sum(jnp.where(eye, jnp.broadcast_to(beta_row, (S, rn, rn)), 0.0), axis=2, keepdims=True)
    decay = jnp.exp(jnp.where(causal, gc_col - gc_row, -jnp.inf))
    eg = jnp.exp(gc_col)
    kb = k_st * beta_col
    both = _dot_nt(jnp.concatenate([kb, q_st], axis=1).astype(BF), k_st.astype(BF))
    n_low = jnp.where(same & (jj < ii), both[:, :rn] * decay, 0.0)
    t_inv = _tri_inverse(n_low, ii, jj, Cp, nil).astype(BF)
    uw = _dot(t_inv, jnp.concatenate([v_st * beta_col, kb * eg], axis=2).astype(BF))
    u, w = uw[:, :, :dv], uw[:, :, dv:]
    qe = q_st * eg
    k_dec = (k_st * jnp.exp(gl_col - gc_col)).astype(BF)
    v_new, q_s = [], []
    for h in range(G):
        r = slice(h * Cp, (h + 1) * Cp)
        s_old = jnp.stack([s_scr[s * G + h] for s in range(S)], axis=0)
        ws_qs = _dot(jnp.concatenate([w[:, r], qe[:, r]], axis=1).astype(BF), s_old.astype(BF))
        v_new_h = u[:, r] - ws_qs[:, :Cp]
        q_s.append(ws_qs[:, Cp:])
        v_new.append(v_new_h)
        s_new = s_old * jnp.exp(gl_col[:, h * Cp:h * Cp + 1]) + _dot_tn(k_dec[:, r], v_new_h.astype(BF))
        for s in range(S):
            s_scr[s * G + h] = s_new[s]
    v_new = jnp.concatenate(v_new, axis=1).astype(BF)
    return jnp.concatenate(q_s, axis=1) + _dot((both[:, rn:] * decay).astype(BF), v_new)


def _gdn_kernel(*refs, C, Cp, G, NS, rep, DK, DV, has_state):
    q_ref, k_ref, v_ref, z_ref, b_ref, a_ref, alog_ref, dtb_ref, wq_ref, wk_ref, wv_ref, gn_ref = refs[:12]
    pos = 12
    if has_state:
        cq_ref, ck_ref, cv_ref, s0_ref = refs[pos:pos + 4]
        pos += 4
    o_ref, so_ref = refs[pos:pos + 2]
    s_scr, cbq, cbk, cbv = refs[pos + 2:]

    c = pl.program_id(2)
    last = pl.num_programs(2) - 1
    tail_lo = SUBLANES - (CONV_W - 1)

    @pl.when(c == 0)
    def _():
        for cb in (cbq, cbk, cbv):
            cb[...] = jnp.zeros_like(cb)
        if has_state:
            cbq[tail_lo:SUBLANES, :] = cq_ref[...]
            cbk[tail_lo:SUBLANES, :] = ck_ref[...]
            cbv[tail_lo:SUBLANES, :] = cv_ref[...]
            s_scr[...] = s0_ref[...]
        else:
            s_scr[...] = jnp.zeros_like(s_scr)

    def conv(cb, x_ref, w_ref):
        cb[SUBLANES:SUBLANES + C, :] = x_ref[...]
        acc = cb[tail_lo:tail_lo + Cp, :] * w_ref[0:1, :]
        for j in range(1, CONV_W):
            acc = acc + cb[tail_lo + j:tail_lo + j + Cp, :] * w_ref[j:j + 1, :]
        cb[tail_lo:SUBLANES, :] = cb[C + tail_lo:C + SUBLANES, :]
        return _silu(acc)

    qc = conv(cbq, q_ref, wq_ref)
    kc = conv(cbk, k_ref, wk_ref)
    vc = conv(cbv, v_ref, wv_ref)

    padded = Cp != C
    if padded:
        row_ok = lax.broadcasted_iota(jnp.int32, (Cp, 1), 0) < C
    q_parts, k_parts, v_parts = [], [], []
    for i in range(G // rep):
        qh = qc[:, i * DK:(i + 1) * DK]
        kh = kc[:, i * DK:(i + 1) * DK]
        qh = qh * lax.rsqrt(jnp.sum(qh * qh, axis=-1, keepdims=True) + EPS) * (DK ** -0.5)
        kh = kh * lax.rsqrt(jnp.sum(kh * kh, axis=-1, keepdims=True) + EPS)
        if padded:
            kh = jnp.where(row_ok, kh, 0.0)
        for e in range(rep):
            hl = i * rep + e
            vh = vc[:, hl * DV:(hl + 1) * DV]
            if padded:
                vh = jnp.where(row_ok, vh, 0.0)
            q_parts.append(qh)
            k_parts.append(kh)
            v_parts.append(vh)
    beta_row = jax.nn.sigmoid(b_ref[c])
    sp_in = a_ref[c] + dtb_ref[...]
    softplus = jnp.maximum(sp_in, 0.0) + jnp.log1p(jnp.exp(-jnp.abs(sp_in)))
    g_row = -jnp.exp(alog_ref[...]) * softplus
    if padded:
        lane_ok = (lax.broadcasted_iota(jnp.int32, (1, G * Cp), 1) & (Cp - 1)) < C
        beta_row = jnp.where(lane_ok, beta_row, 0.0)
        g_row = jnp.where(lane_ok, g_row, 0.0)

    GS = G // NS
    rn = GS * Cp

    def stacks(parts):
        return jnp.stack([jnp.concatenate(parts[s * GS:(s + 1) * GS], axis=0) for s in range(NS)], axis=0)

    def stack_lanes(row):
        return jnp.stack([row[:, s * rn:(s + 1) * rn] for s in range(NS)], axis=0)

    o = _delta_chunk(stacks(q_parts), stacks(k_parts), stacks(v_parts), stack_lanes(beta_row), stack_lanes(g_row),
                     s_scr, G=GS, Cp=Cp, nil=min(C, TRI_BLOCK))
    on = o * lax.rsqrt(jnp.mean(o * o, axis=-1, keepdims=True) + EPS) * gn_ref[...]
    for s in range(NS):
        for h in range(GS):
            hl = s * GS + h
            zg = z_ref[:, hl * DV:(hl + 1) * DV]
            o_ref[:, hl * DV:(hl + 1) * DV] = (on[s, h * Cp:h * Cp + C] * _silu(zg)).astype(o_ref.dtype)

    @pl.when(c == last)
    def _():
        so_ref[...] = s_scr[...]


def gdn_scan(proj, b_logit, a_logit, conv_prev, s0, w_conv, a_log, dt_bias, g_norm, *, B, L, HQK, HV, DK, DV, name):
    rep = HV // HQK
    KEY, VAL = HQK * DK, HV * DV
    C = _pick(L, (GDN_CHUNK,))
    n = L // C
    Cp = -(-C // BF16_ROWS) * BF16_ROWS
    assert Cp & (Cp - 1) == 0 and L >= CONV_W - 1
    GS = min(HV, max(rep, STACK_ROWS // Cp))
    NS = _pick(HV // GS, (STACKS_PER_STEP, 2, 1))
    G = NS * GS
    assert HV % G == 0 and GS % rep == 0
    has_state = s0 is not None
    qw, vw = (G // rep) * DK, G * DV
    k_off, v_off, z_off = KEY // qw, 2 * KEY // vw, (2 * KEY + VAL) // vw

    def stack_rows(t):
        t = t.reshape(B, n, C, HV // G, G)
        t = jnp.pad(t, ((0, 0), (0, 0), (0, Cp - C), (0, 0), (0, 0)))
        return jnp.transpose(t, (0, 3, 1, 4, 2)).reshape(B, HV // G, n, 1, G * Cp)

    def stack_heads(v):
        return jnp.repeat(v.reshape(HV // G, 1, G), Cp, axis=-1)

    gate_spec = pl.BlockSpec((None, None, n, 1, G * Cp), lambda b, h, c: (b, h, 0, 0, 0))
    head_spec = pl.BlockSpec((None, 1, G * Cp), lambda b, h, c: (h, 0, 0))
    in_specs = [
        pl.BlockSpec((None, C, qw), lambda b, h, c: (b, c, h)),
        pl.BlockSpec((None, C, qw), lambda b, h, c: (b, c, k_off + h)),
        pl.BlockSpec((None, C, vw), lambda b, h, c: (b, c, v_off + h)),
        pl.BlockSpec((None, C, vw), lambda b, h, c: (b, c, z_off + h)),
        gate_spec, gate_spec, head_spec, head_spec,
        pl.BlockSpec((CONV_W, qw), lambda b, h, c: (0, h)),
        pl.BlockSpec((CONV_W, qw), lambda b, h, c: (0, k_off + h)),
        pl.BlockSpec((CONV_W, vw), lambda b, h, c: (0, v_off + h)),
        pl.BlockSpec((1, DV), lambda b, h, c: (0, 0)),
    ]
    args = [proj, proj, proj, proj, stack_rows(b_logit), stack_rows(a_logit), stack_heads(a_log),
            stack_heads(dt_bias), w_conv, w_conv, w_conv, g_norm.reshape(1, DV)]
    if has_state:
        in_specs += [
            pl.BlockSpec((None, CONV_W - 1, qw), lambda b, h, c: (b, 0, h)),
            pl.BlockSpec((None, CONV_W - 1, qw), lambda b, h, c: (b, 0, k_off + h)),
            pl.BlockSpec((None, CONV_W - 1, vw), lambda b, h, c: (b, 0, v_off + h)),
            pl.BlockSpec((None, G, DK, DV), lambda b, h, c: (b, h, 0, 0)),
        ]
        args += [conv_prev, conv_prev, conv_prev, s0]
    body = functools.partial(_gdn_kernel, C=C, Cp=Cp, G=G, NS=NS, rep=rep, DK=DK, DV=DV, has_state=has_state)
    return pl.pallas_call(
        body,
        grid=(B, HV // G, n),
        in_specs=in_specs,
        out_specs=[pl.BlockSpec((None, C, vw), lambda b, h, c: (b, c, h)),
                   pl.BlockSpec((None, G, DK, DV), lambda b, h, c: (b, h, 0, 0))],
        out_shape=[jax.ShapeDtypeStruct((B, L, VAL), BF), jax.ShapeDtypeStruct((B, HV, DK, DV), F32)],
        scratch_shapes=[pltpu.VMEM((G, DK, DV), F32),
                        pltpu.VMEM((SUBLANES + Cp, qw), F32),
                        pltpu.VMEM((SUBLANES + Cp, qw), F32),
                        pltpu.VMEM((SUBLANES + Cp, vw), F32)],
        compiler_params=_params(3),
        name=name,
    )(*args)


def _kv_kernel(x_ref, g_ref, sh_ref, sc_ref, w_ref, gkv_ref, cos_ref, sin_ref,
               ckv_ref, kpe_ref, kcat_ref, h_ref, *, rows, R, P):
    _modulate_rows(x_ref, g_ref, sh_ref, sc_ref, h_ref, rows)
    y = _dot(h_ref[...], w_ref[...].astype(BF))
    c = y[:, :R]
    ckv = c * lax.rsqrt(jnp.mean(c * c, axis=-1, keepdims=True) + EPS) * gkv_ref[...]
    rot = y[:, R:R + LANES] * cos_ref[...] + y[:, R + LANES:R + 2 * LANES] * sin_ref[...]
    ckv_ref[...] = ckv
    kpe_ref[...] = rot[:, :P]
    kcat_ref[:, :R] = ckv.astype(BF)
    kcat_ref[:, R:] = rot.astype(BF)


def _rope_weight_cols(w_pe):
    K, P = w_pe.shape
    half = P // 2
    zeros = jnp.zeros((K, LANES - P), w_pe.dtype)
    swapped = jnp.concatenate([w_pe[:, half:], w_pe[:, :half]], axis=1)
    return jnp.concatenate([w_pe, zeros, swapped, zeros], axis=1)


def shared_kv(x, g, shift, scale, w_down, g_kv, cos_t, sin_t, rows_per_group, *, R, P, name):
    M, K = x.shape
    tm = _row_tile(M, rows_per_group, shift.ndim == 3)
    tm = min(tm, 512)
    rows = _pick(tm, (128, 64, 32, 16, 8))
    w_ext = jnp.concatenate([w_down[:, :R], _rope_weight_cols(w_down[:, R:])], axis=1)
    NW = R + 2 * LANES
    mod_spec = _mod_specs(shift, K, tm, rows_per_group)
    row = lambda i: (i, 0)
    fixed = lambda i: (0, 0)
    return pl.pallas_call(
        functools.partial(_kv_kernel, rows=rows, R=R, P=P),
        grid=(M // tm,),
        in_specs=[pl.BlockSpec((tm, K), row), pl.BlockSpec((1, K), fixed), mod_spec, mod_spec,
                  pl.BlockSpec((K, NW), fixed), pl.BlockSpec((1, R), fixed),
                  pl.BlockSpec((tm, LANES), row), pl.BlockSpec((tm, LANES), row)],
        out_specs=[pl.BlockSpec((tm, R), row), pl.BlockSpec((tm, P), row), pl.BlockSpec((tm, R + LANES), row)],
        out_shape=[jax.ShapeDtypeStruct((M, R), F32), jax.ShapeDtypeStruct((M, P), F32),
                   jax.ShapeDtypeStruct((M, R + LANES), BF)],
        scratch_shapes=[pltpu.VMEM((tm, K), BF)],
        compiler_params=_params(1),
        name=name,
    )(x, g.reshape(1, K), shift, scale, w_ext, g_kv.reshape(1, R), cos_t, sin_t)


def _q_kernel(x_ref, g_ref, sh_ref, sc_ref, wdq_ref, gq_ref, wuq_ref, wuk_ref, cos_ref, sin_ref,
              o_ref, cq_ref, h_ref, *, rows, NOPE, R):
    @pl.when(pl.program_id(1) == 0)
    def _():
        _modulate_rows(x_ref, g_ref, sh_ref, sc_ref, h_ref, rows)
        c = _dot(h_ref[...], wdq_ref[...].astype(BF))
        cq_ref[...] = (c * lax.rsqrt(jnp.mean(c * c, axis=-1, keepdims=True) + EPS) * gq_ref[...]).astype(BF)

    qf = _dot(cq_ref[...], wuq_ref[...].astype(BF))
    q_lat = _dot_nt(qf[:, :NOPE].astype(BF), wuk_ref[...].astype(BF))
    rot = qf[:, NOPE:NOPE + LANES] * cos_ref[...] + qf[:, NOPE + LANES:NOPE + 2 * LANES] * sin_ref[...]
    o_ref[:, :R] = q_lat.astype(o_ref.dtype)
    o_ref[:, R:] = rot.astype(o_ref.dtype)


def mla_queries(x, g, shift, scale, w_dq, g_q, w_uq, w_uk, cos_t, sin_t, rows_per_group, *,
                H, NOPE, P, R, head_major, name):
    M, K = x.shape
    QL = w_dq.shape[1]
    tm = _row_tile(M, rows_per_group, True) if head_major else _row_tile(M, rows_per_group, shift.ndim == 3)
    tm = min(tm, 512)
    rows = _pick(tm, (128, 64, 32, 16, 8))
    w_heads = w_uq.reshape(QL, H, NOPE + P)
    w_uq_r = jnp.concatenate(
        [jnp.concatenate([w_heads[:, h, :NOPE], _rope_weight_cols(w_heads[:, h, NOPE:])], axis=1)[None]
         for h in range(H)], axis=0)
    NQ = NOPE + 2 * LANES
    W = R + LANES
    mod_spec = _mod_specs(shift, K, tm, rows_per_group)
    row = lambda i, h: (i, 0)
    fixed = lambda i, h: (0, 0)
    if head_major:
        tiles = rows_per_group // tm
        out_spec = pl.BlockSpec((None, None, tm, W), lambda i, h: (i // tiles, h, i % tiles, 0))
        out_shape = jax.ShapeDtypeStruct((M // rows_per_group, H, rows_per_group, W), BF)
    else:
        out_spec = pl.BlockSpec((tm, W), lambda i, h: (i, h))
        out_shape = jax.ShapeDtypeStruct((M, H * W), BF)
    return pl.pallas_call(
        functools.partial(_q_kernel, rows=rows, NOPE=NOPE, R=R),
        grid=(M // tm, H),
        in_specs=[pl.BlockSpec((tm, K), row), pl.BlockSpec((1, K), fixed), mod_spec, mod_spec,
                  pl.BlockSpec((K, QL), fixed), pl.BlockSpec((1, QL), fixed),
                  pl.BlockSpec((None, QL, NQ), lambda i, h: (h, 0, 0)),
                  pl.BlockSpec((R, NOPE), lambda i, h: (0, h)),
                  pl.BlockSpec((tm, LANES), row), pl.BlockSpec((tm, LANES), row)],
        out_specs=out_spec,
        out_shape=out_shape,
        scratch_shapes=[pltpu.VMEM((tm, QL), BF), pltpu.VMEM((tm, K), BF)],
        compiler_params=_params(2),
        name=name,
    )(x, g.reshape(1, K), shift, scale, w_dq, g_q.reshape(1, QL), w_uq_r, w_uk, cos_t, sin_t)


def _attn_prompt_kernel(q_ref, k_ref, o_ref, m_scr, l_scr, a_scr, acc_scr, s_scr, p_scr, *, tq, tk, H, R, scale):
    qi = pl.program_id(1)
    kj = pl.program_id(2)

    @pl.when(kj == 0)
    def _():
        m_scr[...] = jnp.full_like(m_scr, -jnp.inf)
        l_scr[...] = jnp.zeros_like(l_scr)
        acc_scr[...] = jnp.zeros_like(acc_scr)

    rows = H * tq
    hg = max(1, ATTN_GROUP_ROWS // tq)
    group = hg * tq
    n_groups = rows // group
    rc = min(ATTN_SOFTMAX_ROWS, group)
    c2 = scale * LOG2_E

    def scores(g):
        q = q_ref[g * hg:(g + 1) * hg].reshape(group, q_ref.shape[-1])
        s_scr[g * group:(g + 1) * group, :] = _dot_nt(q, k_ref[...])

    def softmax(g, masked):
        for r0 in range(g * group, (g + 1) * group, rc):
            sl = slice(r0, r0 + rc)
            s = s_scr[sl, :]
            if masked:
                qpos = qi * tq + lax.rem(r0 + lax.broadcasted_iota(jnp.int32, (rc, tk), 0), tq)
                kpos = kj * tk + lax.broadcasted_iota(jnp.int32, (rc, tk), 1)
                s = jnp.where(kpos <= qpos, s, -jnp.inf)
            m_old = m_scr[sl, :]
            m_new = jnp.maximum(m_old, jnp.max(s, axis=-1, keepdims=True))
            alpha = jnp.exp2((m_old - m_new) * c2)
            p = jnp.exp2((s - m_new) * c2)
            l_scr[sl, :] = alpha * l_scr[sl, :] + jnp.sum(p, axis=-1, keepdims=True)
            m_scr[sl, :] = m_new
            a_scr[sl, :] = alpha
            p_scr[sl, :] = p.astype(BF)

    def values(g):
        gs = slice(g * group, (g + 1) * group)
        acc_scr[gs, :] = a_scr[gs, :] * acc_scr[gs, :] + _dot(p_scr[gs, :], k_ref[:, :R])

    def block(masked):
        for g in range(n_groups + 2):
            if g < n_groups:
                scores(g)
            if 1 <= g <= n_groups:
                softmax(g - 1, masked)
            if g >= 2:
                values(g - 2)

    crosses_diagonal = kj * tk + tk - 1 > qi * tq

    @pl.when((kj * tk <= qi * tq + tq - 1) & crosses_diagonal)
    def _():
        block(True)

    @pl.when(jnp.logical_not(crosses_diagonal))
    def _():
        block(False)

    @pl.when(kj == pl.num_programs(2) - 1)
    def _():
        o = acc_scr[...] / l_scr[...]
        o_ref[...] = o.reshape(H, tq, R).astype(o_ref.dtype)


def attention_prompt(q4, kcat, *, R, scale, name):
    B, H, L, W = q4.shape
    tq = _pick(L, (128, 64, 32, 16))
    tk = _pick(L, (512, 256, 128, 64, 32, 16))
    nq, nk = L // tq, L // tk

    def k_index(b, i, j):
        return (b, jnp.minimum(j, (i * tq + tq - 1) // tk), 0)

    return pl.pallas_call(
        functools.partial(_attn_prompt_kernel, tq=tq, tk=tk, H=H, R=R, scale=scale),
        grid=(B, nq, nk),
        in_specs=[pl.BlockSpec((None, H, tq, W), lambda b, i, j: (b, 0, i, 0)),
                  pl.BlockSpec((None, tk, W), k_index)],
        out_specs=pl.BlockSpec((None, H, tq, R), lambda b, i, j: (b, 0, i, 0)),
        out_shape=jax.ShapeDtypeStruct((B, H, L, R), BF),
        scratch_shapes=[pltpu.VMEM((H * tq, 1), F32), pltpu.VMEM((H * tq, 1), F32), pltpu.VMEM((H * tq, 1), F32),
                        pltpu.VMEM((H * tq, R), F32), pltpu.VMEM((H * tq, tk), F32), pltpu.VMEM((H * tq, tk), BF)],
        compiler_params=_params(3),
        name=name,
    )(q4, kcat)


def _attn_sample_kernel(pt_ref, q_ref, knew_ref, ckv_hbm, kpe_hbm, o_ref, m_scr, l_scr, acc_scr,
                        ckv_buf, kpe_buf, sems, *, NP, T, H, R, P, scale):
    b = pl.program_id(0)
    j = pl.program_id(1)
    nj = pl.num_programs(1)
    step = b * nj + j
    slot = lax.rem(step, 2)
    rows = T * H

    def page_copies(bb, jj, sl):
        copies = []
        for i in range(NP):
            pid = pt_ref[bb, jj * NP + i]
            copies.append(pltpu.make_async_copy(ckv_hbm.at[pid], ckv_buf.at[sl, i], sems.at[0, sl]))
            copies.append(pltpu.make_async_copy(kpe_hbm.at[pid], kpe_buf.at[sl, i], sems.at[1, sl]))
        return copies

    @pl.when(step == 0)
    def _():
        for cp in page_copies(b, j, slot):
            cp.start()

    @pl.when(step + 1 < pl.num_programs(0) * nj)
    def _():
        nxt = step + 1
        for cp in page_copies(lax.div(nxt, nj), lax.rem(nxt, nj), 1 - slot):
            cp.start()

    for cp in page_copies(b, j, slot):
        cp.wait()
    q = q_ref[...]
    q_lat = q[:, :R]
    q_pe = q[:, R:R + P]

    c2 = scale * LOG2_E

    def update(state, s, values):
        m_old, l_old, acc = state
        m_new = jnp.maximum(m_old, jnp.max(s, axis=-1, keepdims=True))
        alpha = jnp.exp2((m_old - m_new) * c2)
        p = jnp.exp2((s - m_new) * c2)
        return m_new, alpha * l_old + jnp.sum(p, axis=-1, keepdims=True), alpha * acc + values(p.astype(BF))

    @pl.when(j == 0)
    def _():
        tp = -(-T // BF16_ROWS) * BF16_ROWS
        knew = jnp.concatenate([knew_ref[...], jnp.zeros((tp - T, knew_ref.shape[1]), BF)], axis=0)
        s = _dot_nt(q, knew)
        qt = lax.broadcasted_iota(jnp.int32, (rows, tp), 0) // H
        kt = lax.broadcasted_iota(jnp.int32, (rows, tp), 1)
        s = jnp.where(kt <= qt, s, -jnp.inf)
        init = (jnp.full((rows, 1), -jnp.inf, F32), jnp.zeros((rows, 1), F32), jnp.zeros((rows, R), F32))
        m_scr[...], l_scr[...], acc_scr[...] = update(init, s, lambda p: _dot(p, knew[:, :R]))

    pg = min(ATTN_PAGE_GROUP, NP)
    n_groups = NP // pg

    def load(g):
        return [ckv_buf[slot, i].astype(BF) for i in range(g * pg, (g + 1) * pg)]

    def scores(g, pages):
        return jnp.concatenate([_dot_nt(q_lat, pages[i]) + _dot(q_pe, kpe_buf[slot, g * pg + i].astype(BF))
                                for i in range(pg)], axis=1)

    def values_of(pages):
        def values(p):
            page = pages[0].shape[0]
            out = _dot(p[:, :page], pages[0])
            for i in range(1, pg):
                out = out + _dot(p[:, i * page:(i + 1) * page], pages[i])
            return out
        return values

    state = (m_scr[...], l_scr[...], acc_scr[...])
    pages = load(0)
    s = scores(0, pages)
    for g in range(n_groups):
        if g + 1 < n_groups:
            pages_next = load(g + 1)
            s_next = scores(g + 1, pages_next)
        state = update(state, s, values_of(pages))
        if g + 1 < n_groups:
            pages, s = pages_next, s_next
    m_scr[...], l_scr[...], acc_scr[...] = state

    @pl.when(j == pl.num_programs(1) - 1)
    def _():
        o_ref[...] = (acc_scr[...] / l_scr[...]).astype(o_ref.dtype)


def attention_sample(q3, knew, cache_ckv, cache_kpe_t, page_table, *, T, H, R, P, scale, name):
    Bd, rows, W = q3.shape
    n_pages = page_table.shape[1]
    page = cache_ckv.shape[1]
    NP = _pick(n_pages, (16, 8, 4, 2, 1))
    any_space = pl.BlockSpec(memory_space=pl.ANY)
    grid_spec = pltpu.PrefetchScalarGridSpec(
        num_scalar_prefetch=1,
        grid=(Bd, n_pages // NP),
        in_specs=[pl.BlockSpec((None, rows, W), lambda b, j, pt: (b, 0, 0)),
                  pl.BlockSpec((None, T, W), lambda b, j, pt: (b, 0, 0)),
                  any_space, any_space],
        out_specs=pl.BlockSpec((None, rows, R), lambda b, j, pt: (b, 0, 0)),
        scratch_shapes=[pltpu.VMEM((rows, 1), F32), pltpu.VMEM((rows, 1), F32), pltpu.VMEM((rows, R), F32),
                        pltpu.VMEM((2, NP, page, R), F32), pltpu.VMEM((2, NP, P, page), F32),
                        pltpu.SemaphoreType.DMA((2, 2))],
    )
    return pl.pallas_call(
        functools.partial(_attn_sample_kernel, NP=NP, T=T, H=H, R=R, P=P, scale=scale),
        grid_spec=grid_spec,
        out_shape=jax.ShapeDtypeStruct((Bd, rows, R), BF),
        compiler_params=_params(2),
        name=name,
    )(page_table, q3, knew, cache_ckv, cache_kpe_t)


def _mm_kernel(a_ref, w_ref, o_ref):
    o_ref[...] = _dot(a_ref[...], w_ref[...].astype(BF)).astype(o_ref.dtype)


def value_up(o_lat, w_uv, *, H, R, V, head_major, name):
    if head_major:
        G, _, L, _ = o_lat.shape
        M = G * L
        tm = _pick(L, (1024, 512, 256, 128, 64, 32, 16, 8))
        tiles = L // tm
        a_spec = pl.BlockSpec((None, None, tm, R), lambda i, h: (i // tiles, h, i % tiles, 0))
    else:
        M = o_lat.shape[0]
        tm = _pick(M, (1024, 512, 256, 128, 64, 32, 16, 8))
        a_spec = pl.BlockSpec((tm, R), lambda i, h: (i, h))
    return pl.pallas_call(
        _mm_kernel,
        grid=(M // tm, H),
        in_specs=[a_spec, pl.BlockSpec((R, V), lambda i, h: (0, h))],
        out_specs=pl.BlockSpec((tm, V), lambda i, h: (i, h)),
        out_shape=jax.ShapeDtypeStruct((M, H * V), BF),
        compiler_params=_params(2),
        name=name,
    )(o_lat, w_uv)


def _rope_tables(pos, P, reps):
    half = P // 2
    inv = ROPE_THETA ** (-jnp.arange(half, dtype=F32) / half)
    ang = pos.astype(F32)[:, None] * inv[None, :]
    cos, sin = jnp.cos(ang), jnp.sin(ang)
    zeros = jnp.zeros((pos.shape[0], LANES - P), F32)
    cos_t = jnp.concatenate([cos, cos, zeros], axis=1)
    sin_t = jnp.concatenate([-sin, sin, zeros], axis=1)
    return jnp.tile(cos_t, (reps, 1)), jnp.tile(sin_t, (reps, 1))


def _trunk(x3, mods, pos, conv_in, ssm_in, past, p, tag):
    G, L, D = x3.shape
    M = G * L
    grouped = L % SUBLANES == 0
    x = x3.reshape(M, D)

    def split_mods(m, n):
        parts = jnp.split(m, n, axis=-1)
        if grouped:
            return [t[:, None, :] for t in parts]
        return [jnp.repeat(t, L, axis=0) for t in parts]

    HV, DK, DV = p['HV'], p['DK'], p['DV']
    HQK = p['HQK']
    KEY, VAL = HQK * DK, HV * DV
    H, NOPE, P, R, V = p['H'], p['NOPE'], p['P'], p['R'], p['V']
    FF = p['w_down'].shape[1]
    cos_t, sin_t = _rope_tables(pos, P, G)
    scale = (NOPE + P) ** -0.5

    sh1, sc1, gt1, sh2, sc2, gt2 = split_mods(mods['l0'], 6)
    w_in_t = p['gdn_w_in_t']
    n_proj = 2 * KEY + 2 * VAL
    proj = mod_matmul(x, p['g_mix'][0], sh1, sc1, w_in_t, 0, n_proj, L, w_is_nk=True, name=f'{tag}_gdn_in')
    w_ba_t = jnp.concatenate([w_in_t[0, n_proj:], jnp.zeros((LANES - 2 * HV, D), F32)], axis=0)
    ba = mod_matmul(x, p['g_mix'][0], sh1, sc1, w_ba_t, 0, LANES, L, w_is_nk=True, name=f'{tag}_gdn_ba')
    proj3 = proj.reshape(G, L, n_proj)
    o_g, ssm_new = gdn_scan(
        proj3, ba[:, :HV], ba[:, HV:2 * HV], conv_in, ssm_in, p['gdn_w_conv'][0], p['gdn_a_log'][0],
        p['gdn_dt_bias'][0], p['gdn_g_norm'][0], B=G, L=L, HQK=HQK, HV=HV, DK=DK, DV=DV, name=f'{tag}_gdn_scan')
    conv_new = proj3[:, L - (CONV_W - 1):, :2 * KEY + VAL]
    x = matmul_residual(o_g.reshape(M, VAL), p['gdn_w_out'], 0, x, gt1, L, name=f'{tag}_gdn_out')
    hff = mod_matmul(x, p['g_ffn'][0], sh2, sc2, p['w_gate_up'], 0, FF, L, swiglu=True, out_dtype=BF,
                     name=f'{tag}_ffn0_up')
    x = matmul_residual(hff, p['w_down'], 0, x, gt2, L, name=f'{tag}_ffn0_down')

    shk, sck = split_mods(mods['kv'], 2)
    ckv, kpe, kcat = shared_kv(x, p['kv_g_in'], shk, sck, p['kv_w_down'], p['kv_g_norm'], cos_t, sin_t, L,
                               R=R, P=P, name=f'{tag}_kv')

    sh1, sc1, gt1, sh2, sc2, gt2 = split_mods(mods['l1'], 6)
    head_major = past is None
    q = mla_queries(x, p['g_mix'][1], sh1, sc1, p['mla_w_dq'][0], p['mla_g_q'][0], p['mla_w_uq'][0], p['kv_w_uk'],
                    cos_t, sin_t, L, H=H, NOPE=NOPE, P=P, R=R, head_major=head_major, name=f'{tag}_q')
    if head_major:
        o_lat = attention_prompt(q, kcat.reshape(G, L, R + LANES), R=R, scale=scale, name=f'{tag}_attn')
    else:
        cache_ckv, cache_kpe, page_table = past
        o_lat = attention_sample(q.reshape(G, L * H, R + LANES), kcat.reshape(G, L, R + LANES), cache_ckv,
                                 cache_kpe, page_table, T=L, H=H, R=R, P=P, scale=scale, name=f'{tag}_attn')
        o_lat = o_lat.reshape(M, H * R)
    o = value_up(o_lat, p['kv_w_uv'], H=H, R=R, V=V, head_major=head_major, name=f'{tag}_uv')
    x = matmul_residual(o, p['mla_w_o'], 0, x, gt1, L, name=f'{tag}_attn_out')
    hff = mod_matmul(x, p['g_ffn'][1], sh2, sc2, p['w_gate_up'], 1, FF, L, swiglu=True, out_dtype=BF,
                     name=f'{tag}_ffn1_up')
    x = matmul_residual(hff, p['w_down'], 1, x, gt2, L, name=f'{tag}_ffn1_down')

    shf, scf = split_mods(mods['final'], 2)
    y = modulate_rows(x, p['final_g'], shf, scf, L, name=f'{tag}_final')
    return (y.reshape(G, L, D), conv_new[None], ssm_new[None], ckv.reshape(G, L, R), kpe.reshape(G, L, P))


def kernel(x_prompt, x_sample, c_prompt, c_sample, cache_ckv, cache_kpe, page_table, state_ssm, state_conv, w_ada, b_ada, g_mix, g_ffn, w_gate_up, w_down, gdn_w_in, gdn_w_conv, gdn_a_log, gdn_dt_bias, gdn_g_norm, gdn_w_out, kv_w_ada, kv_b_ada, kv_g_in, kv_w_down, kv_g_norm, kv_w_uk, kv_w_uv, mla_w_dq, mla_g_q, mla_w_uq, mla_w_o, final_w_ada, final_b_ada, final_g):
    B, S, D = x_prompt.shape
    Bd, T, _ = x_sample.shape
    HV, DK, DV = state_ssm.shape[2:]
    KEY = (state_conv.shape[-1] - HV * DV) // 2
    R = cache_ckv.shape[-1]
    P = cache_kpe.shape[-1]
    QL = mla_w_dq.shape[-1]
    nope_total = kv_w_uk.shape[1]
    H = (mla_w_uq.shape[-1] - nope_total) // P
    p = dict(w_ada=w_ada, b_ada=b_ada, g_mix=g_mix, g_ffn=g_ffn, w_gate_up=w_gate_up, w_down=w_down,
             gdn_w_in_t=jnp.swapaxes(gdn_w_in, 1, 2), gdn_w_conv=gdn_w_conv, gdn_a_log=gdn_a_log, gdn_dt_bias=gdn_dt_bias,
             gdn_g_norm=gdn_g_norm, gdn_w_out=gdn_w_out, kv_g_in=kv_g_in, kv_w_down=kv_w_down,
             kv_g_norm=kv_g_norm, kv_w_uk=kv_w_uk, kv_w_uv=kv_w_uv, mla_w_dq=mla_w_dq, mla_g_q=mla_g_q,
             mla_w_uq=mla_w_uq, mla_w_o=mla_w_o, final_g=final_g,
             HV=HV, DK=DK, DV=DV, HQK=KEY // DK, H=H, NOPE=nope_total // H, P=P, R=R,
             V=kv_w_uv.shape[1] // H)

    n_c = B + Bd
    pad = -n_c % SUBLANES
    c_all = jnp.concatenate([c_prompt, c_sample, jnp.zeros((pad, D), F32)], axis=0)
    m_l0 = ada_dense(c_all, w_ada, b_ada, 0, name='ada_l0')
    m_l1 = ada_dense(c_all, w_ada, b_ada, 1, name='ada_l1')
    m_kv = ada_dense(c_all, kv_w_ada, kv_b_ada, 0, name='ada_kv')
    m_f = ada_dense(c_all, final_w_ada, final_b_ada, 0, name='ada_final')

    def mods(lo, hi):
        return dict(l0=m_l0[lo:hi], l1=m_l1[lo:hi], kv=m_kv[lo:hi], final=m_f[lo:hi])

    y_p, conv_p, ssm_p, ckv_p, kpe_p = _trunk(x_prompt, mods(0, B), jnp.arange(S), None, None, None, p, 'p')
    past_len = page_table.shape[1] * cache_ckv.shape[1]
    y_s, conv_s, ssm_s, ckv_s, kpe_s = _trunk(x_sample, mods(B, n_c), past_len + jnp.arange(T), state_conv[0],
                                              state_ssm[0], (cache_ckv, jnp.swapaxes(cache_kpe, 1, 2), page_table),
                                              p, 's')
    return (y_p, y_s, ssm_p, conv_p, ckv_p, kpe_p, ssm_s, conv_s, ckv_s, kpe_s)
```

```python
import functools

import jax
import jax.numpy as jnp
from jax import lax
from jax.experimental import pallas as pl
from jax.experimental.pallas import tpu as pltpu

EPS = 1e-6
ROPE_THETA = 10000.0
CONV_W = 4
GDN_CHUNK = 64
F32 = jnp.float32
BF = jnp.bfloat16

V7X_VMEM_BYTES = 64 * 1024 * 1024
VMEM_LIMIT = V7X_VMEM_BYTES - 8 * 1024 * 1024
LANES = 128
SUBLANES = 8
BF16_ROWS = 16
TRI_BLOCK = 16
STACK_ROWS = 128
STACKS_PER_STEP = 8
ATTN_GROUP_ROWS = 256
ATTN_SOFTMAX_ROWS = 64
ATTN_PAGE_GROUP = 4
ATTN_PAGE_SLOTS = 3
Q_HEADS_PER_STEP = 4
LOG2_E = 1.4426950408889634


def _params(n_axes):
    return pltpu.CompilerParams(dimension_semantics=("arbitrary",) * n_axes, vmem_limit_bytes=VMEM_LIMIT)


def _pick(n, cands):
    for c in cands:
        if n % c == 0:
            return c
    return n


def _silu(x):
    return x * jax.nn.sigmoid(x)


def _contract(a, b, ca, cb):
    batch = tuple(range(a.ndim - 2))
    dims = (((a.ndim - 2 + ca,), (b.ndim - 2 + cb,)), (batch, batch))
    return lax.dot_general(a, b, dims, preferred_element_type=F32)


def _dot(a, b):
    return _contract(a, b, 1, 0)


def _dot_nt(a, b):
    return _contract(a, b, 1, 1)


def _dot_tn(a, b):
    return _contract(a, b, 0, 0)


def _split_bf16(a):
    hi = a.astype(BF)
    lo = (a - hi.astype(F32)).astype(BF)
    return hi, lo


def _dot3(a, b):
    ah, al = _split_bf16(a)
    bh, bl = _split_bf16(b)
    return _dot(ah, bh) + _dot(ah, bl) + _dot(al, bh)


def _modulate_rows(x_ref, g_ref, sh_ref, sc_ref, h_ref, rows):
    tm = x_ref.shape[0]
    per_row = sh_ref.shape[0] != 1

    def body(r, carry):
        sl = pl.ds(pl.multiple_of(r * rows, rows), rows)
        x = x_ref[sl, :]
        y = x * lax.rsqrt(jnp.mean(x * x, axis=-1, keepdims=True) + EPS) * g_ref[...]
        sc = sc_ref[sl, :] if per_row else sc_ref[...]
        sh = sh_ref[sl, :] if per_row else sh_ref[...]
        h_ref[sl, :] = (y * (1.0 + sc) + sh).astype(h_ref.dtype)
        return carry

    lax.fori_loop(0, tm // rows, body, 0)


def _mod_specs(shift, K, tm, rows_per_group):
    if shift.ndim == 3:
        tiles_per_group = rows_per_group // tm
        return pl.BlockSpec((None, 1, K), lambda i, *_: (i // tiles_per_group, 0, 0))
    return pl.BlockSpec((tm, K), lambda i, *_: (i, 0))


def _w_spec(w, layer, K, tn, col_off_blocks=0):
    if w.ndim == 3:
        return pl.BlockSpec((None, K, tn), lambda i, j: (layer, 0, j + col_off_blocks))
    return pl.BlockSpec((K, tn), lambda i, j: (0, j + col_off_blocks))


def _modmm_kernel(x_ref, g_ref, sh_ref, sc_ref, w_ref, o_ref, h_ref, *, rows, w_is_nk):
    @pl.when(pl.program_id(1) == 0)
    def _():
        _modulate_rows(x_ref, g_ref, sh_ref, sc_ref, h_ref, rows)

    dot = _dot_nt if w_is_nk else _dot
    o_ref[...] = dot(h_ref[...], w_ref[...].astype(BF)).astype(o_ref.dtype)


def _modmm_swiglu_kernel(x_ref, g_ref, sh_ref, sc_ref, wg_ref, wu_ref, o_ref, h_ref, *, rows):
    @pl.when(pl.program_id(1) == 0)
    def _():
        _modulate_rows(x_ref, g_ref, sh_ref, sc_ref, h_ref, rows)

    h = h_ref[...]
    gate = _dot(h, wg_ref[...].astype(BF))
    up = _dot(h, wu_ref[...].astype(BF))
    o_ref[...] = (_silu(gate) * up).astype(o_ref.dtype)


def _row_tile(M, rows_per_group, grouped):
    base = rows_per_group if grouped else M
    return _pick(base, (1024, 512, 256, 128, 64, 32, 16, 8))


def mod_matmul(x, g, shift, scale, w, layer, n_out, rows_per_group, *, swiglu=False, w_is_nk=False,
               out_dtype=F32, name):
    M, K = x.shape
    tm = _row_tile(M, rows_per_group, shift.ndim == 3)
    tn = _pick(n_out, (256, 128)) if swiglu else _pick(n_out, (512, 256, 128))
    rows = _pick(tm, (128, 64, 32, 16, 8))
    mod_spec = _mod_specs(shift, K, tm, rows_per_group)
    if not w_is_nk:
        w_spec = _w_spec(w, layer, K, tn)
    elif w.ndim == 3:
        w_spec = pl.BlockSpec((None, tn, K), lambda i, j: (layer, j, 0))
    else:
        w_spec = pl.BlockSpec((tn, K), lambda i, j: (j, 0))
    in_specs = [pl.BlockSpec((tm, K), lambda i, j: (i, 0)),
                pl.BlockSpec((1, K), lambda i, j: (0, 0)),
                mod_spec, mod_spec, w_spec]
    args = [x, g.reshape(1, K), shift, scale, w]
    if swiglu:
        in_specs.append(_w_spec(w, layer, K, tn, n_out // tn))
        args.append(w)
        body = functools.partial(_modmm_swiglu_kernel, rows=rows)
    else:
        body = functools.partial(_modmm_kernel, rows=rows, w_is_nk=w_is_nk)
    return pl.pallas_call(
        body,
        grid=(M // tm, n_out // tn),
        in_specs=in_specs,
        out_specs=pl.BlockSpec((tm, tn), lambda i, j: (i, j)),
        out_shape=jax.ShapeDtypeStruct((M, n_out), out_dtype),
        scratch_shapes=[pltpu.VMEM((tm, K), BF)],
        compiler_params=_params(2),
        name=name,
    )(*args)


def _mmres_kernel(a_ref, w_ref, res_ref, gate_ref, o_ref):
    y = _dot(a_ref[...], w_ref[...].astype(BF))
    o_ref[...] = res_ref[...] + gate_ref[...] * y


def matmul_residual(a, w, layer, res, gate, rows_per_group, *, name):
    M, K = a.shape
    N = res.shape[1]
    tm = _row_tile(M, rows_per_group, gate.ndim == 3)
    tn = _pick(N, (256, 128)) if K > 4096 else _pick(N, (512, 256, 128))
    if gate.ndim == 3:
        tiles_per_group = rows_per_group // tm
        gate_spec = pl.BlockSpec((None, 1, tn), lambda i, j: (i // tiles_per_group, 0, j))
    else:
        gate_spec = pl.BlockSpec((tm, tn), lambda i, j: (i, j))
    return pl.pallas_call(
        _mmres_kernel,
        grid=(M // tm, N // tn),
        in_specs=[pl.BlockSpec((tm, K), lambda i, j: (i, 0)),
                  _w_spec(w, layer, K, tn),
                  pl.BlockSpec((tm, tn), lambda i, j: (i, j)),
                  gate_spec],
        out_specs=pl.BlockSpec((tm, tn), lambda i, j: (i, j)),
        out_shape=jax.ShapeDtypeStruct((M, N), F32),
        compiler_params=_params(2),
        name=name,
    )(a, w, res, gate)


def _ada_kernel(c_ref, w_ref, b_ref, o_ref):
    a = _silu(c_ref[...]).astype(BF)
    o_ref[...] = _dot(a, w_ref[...].astype(BF)) + b_ref[...]


def ada_dense(c, w, b, layer, *, name):
    M, K = c.shape
    N = w.shape[-1]
    tn = _pick(N, (512, 256, 128))
    if b.ndim == 2:
        b_spec = pl.BlockSpec((None, 1, tn), lambda i, j: (layer, 0, j))
        b = b.reshape(b.shape[0], 1, N)
    else:
        b_spec = pl.BlockSpec((1, tn), lambda i, j: (0, j))
        b = b.reshape(1, N)
    return pl.pallas_call(
        _ada_kernel,
        grid=(1, N // tn),
        in_specs=[pl.BlockSpec((M, K), lambda i, j: (0, 0)), _w_spec(w, layer, K, tn), b_spec],
        out_specs=pl.BlockSpec((M, tn), lambda i, j: (0, j)),
        out_shape=jax.ShapeDtypeStruct((M, N), F32),
        compiler_params=_params(2),
        name=name,
    )(c, w, b)


def _modulate_kernel(x_ref, g_ref, sh_ref, sc_ref, o_ref, *, rows):
    _modulate_rows(x_ref, g_ref, sh_ref, sc_ref, o_ref, rows)


def modulate_rows(x, g, shift, scale, rows_per_group, *, name):
    M, K = x.shape
    tm = _row_tile(M, rows_per_group, shift.ndim == 3)
    rows = _pick(tm, (128, 64, 32, 16, 8))
    mod_spec = _mod_specs(shift, K, tm, rows_per_group)
    return pl.pallas_call(
        functools.partial(_modulate_kernel, rows=rows),
        grid=(M // tm,),
        in_specs=[pl.BlockSpec((tm, K), lambda i: (i, 0)), pl.BlockSpec((1, K), lambda i: (0, 0)),
                  mod_spec, mod_spec],
        out_specs=pl.BlockSpec((tm, K), lambda i: (i, 0)),
        out_shape=jax.ShapeDtypeStruct((M, K), F32),
        compiler_params=_params(1),
        name=name,
    )(x, g.reshape(1, K), shift, scale)


def _dot1(a, b):
    return _dot(a.astype(BF), b.astype(BF))


def _tri_inverse(n_low, ii, jj, cp, nil):
    rn = n_low.shape[-1]
    eye = (ii == jj).astype(F32)
    base = min(TRI_BLOCK, cp)
    shift = base.bit_length() - 1
    nd = jnp.where((ii >> shift) == (jj >> shift), n_low, 0.0)
    p = eye - nd
    if nil > 2:
        npow = _dot1(nd, nd)
        pw = 2
        while 2 * pw - 1 < nil - 1:
            both = _dot1(jnp.concatenate([p, npow], axis=-2), npow)
            p = p + both[:, :rn]
            npow = both[:, rn:]
            pw *= 2
        p = p + _dot1(p, npow)
    size = base
    while size < cp:
        s = size.bit_length() - 1
        off = ((ii >> (s + 1)) == (jj >> (s + 1))) & (((ii >> s) & 1) == 1) & (((jj >> s) & 1) == 0)
        x = _dot1(jnp.where(off, n_low, 0.0), p)
        p = p - _dot1(p, x)
        size *= 2
    resid = (eye - p) - _dot3(n_low, p)
    return p + _dot1(p, resid)


def _delta_chunk(q_st, k_st, v_st, beta_row, g_row, s_scr, *, G, Cp, nil):
    S, rn, dv = v_st.shape
    ii = lax.broadcasted_iota(jnp.int32, (1, rn, rn), 1)
    jj = lax.broadcasted_iota(jnp.int32, (1, rn, rn), 2)
    sh = Cp.bit_length() - 1
    same = (ii >> sh) == (jj >> sh)
    eye = ii == jj
    causal = same & (jj <= ii)
    g_mat = jnp.broadcast_to(g_row, (S, rn, rn))
    g_col = jnp.sum(jnp.where(eye, g_mat, 0.0), axis=2, keepdims=True)
    gc_col = jnp.sum(jnp.where(causal, g_mat, 0.0), axis=2, keepdims=True)
    gl_col = jnp.sum(jnp.where(same, g_mat, 0.0), axis=2, keepdims=True)
    gc_row = jnp.sum(jnp.where(same & (ii <= jj), jnp.broadcast_to(g_col, (S, rn, rn)), 0.0), axis=1, keepdims=True)
    beta_col = jnp.sum(jnp.where(eye, jnp.broadcast_to(beta_row, (S, rn, rn)), 0.0), axis=2, keepdims=True)
    decay = jnp.exp(jnp.where(causal, gc_col - gc_row, -jnp.inf))
    eg = jnp.exp(gc_col)
    kb = k_st * beta_col
    both = _dot_nt(jnp.concatenate([kb, q_st], axis=1).astype(BF), k_st.astype(BF))
    n_low = jnp.where(same & (jj < ii), both[:, :rn] * decay, 0.0)
    t_inv = _tri_inverse(n_low, ii, jj, Cp, nil).astype(BF)
    uw = _dot(t_inv, jnp.concatenate([v_st * beta_col, kb * eg], axis=2).astype(BF))
    u, w = uw[:, :, :dv], uw[:, :, dv:]
    qe = q_st * eg
    k_dec = (k_st * jnp.exp(gl_col - gc_col)).astype(BF)
    v_new, q_s = [], []
    for h in range(G):
        r = slice(h * Cp, (h + 1) * Cp)
        s_old = jnp.stack([s_scr[s * G + h] for s in range(S)], axis=0)
        ws_qs = _dot(jnp.concatenate([w[:, r], qe[:, r]], axis=1).astype(BF), s_old.astype(BF))
        v_new_h = u[:, r] - ws_qs[:, :Cp]
        q_s.append(ws_qs[:, Cp:])
        v_new.append(v_new_h)
        s_new = s_old * jnp.exp(gl_col[:, h * Cp:h * Cp + 1]) + _dot_tn(k_dec[:, r], v_new_h.astype(BF))
        for s in range(S):
            s_scr[s * G + h] = s_new[s]
    v_new = jnp.concatenate(v_new, axis=1).astype(BF)
    return jnp.concatenate(q_s, axis=1) + _dot((both[:, rn:] * decay).astype(BF), v_new)


def _gdn_kernel(*refs, C, Cp, G, NS, rep, DK, DV, has_state):
    q_ref, k_ref, v_ref, z_ref, b_ref, a_ref, alog_ref, dtb_ref, wq_ref, wk_ref, wv_ref, gn_ref = refs[:12]
    pos = 12
    if has_state:
        cq_ref, ck_ref, cv_ref, s0_ref = refs[pos:pos + 4]
        pos += 4
    o_ref, so_ref = refs[pos:pos + 2]
    s_scr, cbq, cbk, cbv = refs[pos + 2:]

    c = pl.program_id(2)
    last = pl.num_programs(2) - 1
    tail_lo = SUBLANES - (CONV_W - 1)
    Cc = -(-C // SUBLANES) * SUBLANES

    @pl.when(c == 0)
    def _():
        for cb in (cbq, cbk, cbv):
            cb[...] = jnp.zeros_like(cb)
        if has_state:
            cbq[tail_lo:SUBLANES, :] = cq_ref[...]
            cbk[tail_lo:SUBLANES, :] = ck_ref[...]
            cbv[tail_lo:SUBLANES, :] = cv_ref[...]
            s_scr[...] = s0_ref[...]
        else:
            s_scr[...] = jnp.zeros_like(s_scr)

    def conv(cb, x_ref, w_ref):
        cb[SUBLANES:SUBLANES + C, :] = x_ref[...]
        acc = cb[tail_lo:tail_lo + Cc, :] * w_ref[0:1, :]
        for j in range(1, CONV_W):
            acc = acc + cb[tail_lo + j:tail_lo + j + Cc, :] * w_ref[j:j + 1, :]
        cb[tail_lo:SUBLANES, :] = cb[C + tail_lo:C + SUBLANES, :]
        return _silu(acc)

    qc = conv(cbq, q_ref, wq_ref)
    kc = conv(cbk, k_ref, wk_ref)
    vc = conv(cbv, v_ref, wv_ref)

    padded = Cp != C
    if padded:
        row_ok = lax.broadcasted_iota(jnp.int32, (Cc, 1), 0) < C

    def pad_rows(x):
        return x if Cp == Cc else jnp.concatenate([x, jnp.zeros((Cp - Cc, x.shape[1]), F32)], axis=0)

    q_parts, k_parts, v_parts = [], [], []
    for i in range(G // rep):
        qh = qc[:, i * DK:(i + 1) * DK]
        kh = kc[:, i * DK:(i + 1) * DK]
        qh = qh * lax.rsqrt(jnp.sum(qh * qh, axis=-1, keepdims=True) + EPS) * (DK ** -0.5)
        kh = kh * lax.rsqrt(jnp.sum(kh * kh, axis=-1, keepdims=True) + EPS)
        if padded:
            kh = jnp.where(row_ok, kh, 0.0)
        for e in range(rep):
            hl = i * rep + e
            vh = vc[:, hl * DV:(hl + 1) * DV]
            if padded:
                vh = jnp.where(row_ok, vh, 0.0)
            q_parts.append(pad_rows(qh))
            k_parts.append(pad_rows(kh))
            v_parts.append(pad_rows(vh))
    beta_row = jax.nn.sigmoid(b_ref[c])
    sp_in = a_ref[c] + dtb_ref[...]
    softplus = jnp.maximum(sp_in, 0.0) + jnp.log1p(jnp.exp(-jnp.abs(sp_in)))
    g_row = -jnp.exp(alog_ref[...]) * softplus
    if padded:
        lane_ok = (lax.broadcasted_iota(jnp.int32, (1, G * Cp), 1) & (Cp - 1)) < C
        beta_row = jnp.where(lane_ok, beta_row, 0.0)
        g_row = jnp.where(lane_ok, g_row, 0.0)

    GS = G // NS
    rn = GS * Cp

    def stacks(parts):
        return jnp.stack([jnp.concatenate(parts[s * GS:(s + 1) * GS], axis=0) for s in range(NS)], axis=0)

    def stack_lanes(row):
        return jnp.stack([row[:, s * rn:(s + 1) * rn] for s in range(NS)], axis=0)

    o = _delta_chunk(stacks(q_parts), stacks(k_parts), stacks(v_parts), stack_lanes(beta_row), stack_lanes(g_row),
                     s_scr, G=GS, Cp=Cp, nil=min(C, TRI_BLOCK))
    on = o * lax.rsqrt(jnp.mean(o * o, axis=-1, keepdims=True) + EPS) * gn_ref[...]
    for s in range(NS):
        for h in range(GS):
            hl = s * GS + h
            zg = z_ref[:, hl * DV:(hl + 1) * DV]
            o_ref[:, hl * DV:(hl + 1) * DV] = (on[s, h * Cp:h * Cp + C] * _silu(zg)).astype(o_ref.dtype)

    @pl.when(c == last)
    def _():
        so_ref[...] = s_scr[...]


def gdn_scan(proj, b_logit, a_logit, conv_prev, s0, w_conv, a_log, dt_bias, g_norm, *, B, L, HQK, HV, DK, DV, name):
    rep = HV // HQK
    KEY, VAL = HQK * DK, HV * DV
    C = _pick(L, (GDN_CHUNK,))
    n = L // C
    Cp = -(-C // BF16_ROWS) * BF16_ROWS
    assert Cp & (Cp - 1) == 0 and L >= CONV_W - 1
    GS = min(HV, max(rep, STACK_ROWS // Cp))
    NS = _pick(HV // GS, (STACKS_PER_STEP, 4, 2, 1))
    G = NS * GS
    assert HV % G == 0 and GS % rep == 0
    has_state = s0 is not None
    qw, vw = (G // rep) * DK, G * DV
    k_off, v_off, z_off = KEY // qw, 2 * KEY // vw, (2 * KEY + VAL) // vw

    def stack_rows(t):
        t = t.reshape(B, n, C, HV // G, G)
        t = jnp.pad(t, ((0, 0), (0, 0), (0, Cp - C), (0, 0), (0, 0)))
        return jnp.transpose(t, (0, 3, 1, 4, 2)).reshape(B, HV // G, n, 1, G * Cp)

    def stack_heads(v):
        return jnp.repeat(v.reshape(HV // G, 1, G), Cp, axis=-1)

    gate_spec = pl.BlockSpec((None, None, n, 1, G * Cp), lambda b, h, c: (b, h, 0, 0, 0))
    head_spec = pl.BlockSpec((None, 1, G * Cp), lambda b, h, c: (h, 0, 0))
    in_specs = [
        pl.BlockSpec((None, C, qw), lambda b, h, c: (b, c, h)),
        pl.BlockSpec((None, C, qw), lambda b, h, c: (b, c, k_off + h)),
        pl.BlockSpec((None, C, vw), lambda b, h, c: (b, c, v_off + h)),
        pl.BlockSpec((None, C, vw), lambda b, h, c: (b, c, z_off + h)),
        gate_spec, gate_spec, head_spec, head_spec,
        pl.BlockSpec((CONV_W, qw), lambda b, h, c: (0, h)),
        pl.BlockSpec((CONV_W, qw), lambda b, h, c: (0, k_off + h)),
        pl.BlockSpec((CONV_W, vw), lambda b, h, c: (0, v_off + h)),
        pl.BlockSpec((1, DV), lambda b, h, c: (0, 0)),
    ]
    args = [proj, proj, proj, proj, stack_rows(b_logit), stack_rows(a_logit), stack_heads(a_log),
            stack_heads(dt_bias), w_conv, w_conv, w_conv, g_norm.reshape(1, DV)]
    if has_state:
        in_specs += [
            pl.BlockSpec((None, CONV_W - 1, qw), lambda b, h, c: (b, 0, h)),
            pl.BlockSpec((None, CONV_W - 1, qw), lambda b, h, c: (b, 0, k_off + h)),
            pl.BlockSpec((None, CONV_W - 1, vw), lambda b, h, c: (b, 0, v_off + h)),
            pl.BlockSpec((None, G, DK, DV), lambda b, h, c: (b, h, 0, 0)),
        ]
        args += [conv_prev, conv_prev, conv_prev, s0]
    body = functools.partial(_gdn_kernel, C=C, Cp=Cp, G=G, NS=NS, rep=rep, DK=DK, DV=DV, has_state=has_state)
    return pl.pallas_call(
        body,
        grid=(B, HV // G, n),
        in_specs=in_specs,
        out_specs=[pl.BlockSpec((None, C, vw), lambda b, h, c: (b, c, h)),
                   pl.BlockSpec((None, G, DK, DV), lambda b, h, c: (b, h, 0, 0))],
        out_shape=[jax.ShapeDtypeStruct((B, L, VAL), BF), jax.ShapeDtypeStruct((B, HV, DK, DV), F32)],
        scratch_shapes=[pltpu.VMEM((G, DK, DV), F32),
                        pltpu.VMEM((SUBLANES + Cp, qw), F32),
                        pltpu.VMEM((SUBLANES + Cp, qw), F32),
                        pltpu.VMEM((SUBLANES + Cp, vw), F32)],
        compiler_params=_params(3),
        name=name,
    )(*args)


def _kv_kernel(x_ref, g_ref, sh_ref, sc_ref, w_ref, gkv_ref, cos_ref, sin_ref,
               ckv_ref, kpe_ref, kcat_ref, h_ref, *, rows, R, P):
    _modulate_rows(x_ref, g_ref, sh_ref, sc_ref, h_ref, rows)
    y = _dot(h_ref[...], w_ref[...].astype(BF))
    c = y[:, :R]
    ckv = c * lax.rsqrt(jnp.mean(c * c, axis=-1, keepdims=True) + EPS) * gkv_ref[...]
    rot = y[:, R:R + LANES] * cos_ref[...] + y[:, R + LANES:R + 2 * LANES] * sin_ref[...]
    ckv_ref[...] = ckv
    kpe_ref[...] = rot[:, :P]
    kcat_ref[:, :R] = ckv.astype(BF)
    kcat_ref[:, R:] = rot.astype(BF)


def _rope_weight_cols(w_pe):
    K, P = w_pe.shape
    half = P // 2
    zeros = jnp.zeros((K, LANES - P), w_pe.dtype)
    swapped = jnp.concatenate([w_pe[:, half:], w_pe[:, :half]], axis=1)
    return jnp.concatenate([w_pe, zeros, swapped, zeros], axis=1)


def shared_kv(x, g, shift, scale, w_down, g_kv, cos_t, sin_t, rows_per_group, *, R, P, name):
    M, K = x.shape
    tm = _row_tile(M, rows_per_group, shift.ndim == 3)
    tm = min(tm, 512)
    rows = _pick(tm, (128, 64, 32, 16, 8))
    w_ext = jnp.concatenate([w_down[:, :R], _rope_weight_cols(w_down[:, R:])], axis=1)
    NW = R + 2 * LANES
    mod_spec = _mod_specs(shift, K, tm, rows_per_group)
    row = lambda i: (i, 0)
    fixed = lambda i: (0, 0)
    return pl.pallas_call(
        functools.partial(_kv_kernel, rows=rows, R=R, P=P),
        grid=(M // tm,),
        in_specs=[pl.BlockSpec((tm, K), row), pl.BlockSpec((1, K), fixed), mod_spec, mod_spec,
                  pl.BlockSpec((K, NW), fixed), pl.BlockSpec((1, R), fixed),
                  pl.BlockSpec((tm, LANES), row), pl.BlockSpec((tm, LANES), row)],
        out_specs=[pl.BlockSpec((tm, R), row), pl.BlockSpec((tm, P), row), pl.BlockSpec((tm, R + LANES), row)],
        out_shape=[jax.ShapeDtypeStruct((M, R), F32), jax.ShapeDtypeStruct((M, P), F32),
                   jax.ShapeDtypeStruct((M, R + LANES), BF)],
        scratch_shapes=[pltpu.VMEM((tm, K), BF)],
        compiler_params=_params(1),
        name=name,
    )(x, g.reshape(1, K), shift, scale, w_ext, g_kv.reshape(1, R), cos_t, sin_t)


def _q_kernel(x_ref, g_ref, sh_ref, sc_ref, wdq_ref, gq_ref, wuq_ref, wuk_ref, cos_ref, sin_ref,
              o_ref, cq_ref, h_ref, *, rows, NOPE, R, HB, head_major):
    @pl.when(pl.program_id(1) == 0)
    def _():
        _modulate_rows(x_ref, g_ref, sh_ref, sc_ref, h_ref, rows)
        c = _dot(h_ref[...], wdq_ref[...].astype(BF))
        cq_ref[...] = (c * lax.rsqrt(jnp.mean(c * c, axis=-1, keepdims=True) + EPS) * gq_ref[...]).astype(BF)

    W = R + LANES
    for hb in range(HB):
        qf = _dot(cq_ref[...], wuq_ref[hb].astype(BF))
        q_lat = _dot_nt(qf[:, :NOPE].astype(BF), wuk_ref[:, hb * NOPE:(hb + 1) * NOPE].astype(BF))
        rot = qf[:, NOPE:NOPE + LANES] * cos_ref[...] + qf[:, NOPE + LANES:NOPE + 2 * LANES] * sin_ref[...]
        if head_major:
            o_ref[hb, :, :R] = q_lat.astype(o_ref.dtype)
            o_ref[hb, :, R:] = rot.astype(o_ref.dtype)
        else:
            o_ref[:, hb * W:hb * W + R] = q_lat.astype(o_ref.dtype)
            o_ref[:, hb * W + R:(hb + 1) * W] = rot.astype(o_ref.dtype)


def mla_queries(x, g, shift, scale, w_dq, g_q, w_uq, w_uk, cos_t, sin_t, rows_per_group, *,
                H, NOPE, P, R, head_major, name):
    M, K = x.shape
    QL = w_dq.shape[1]
    tm = _row_tile(M, rows_per_group, True) if head_major else _row_tile(M, rows_per_group, shift.ndim == 3)
    tm = min(tm, 512)
    rows = _pick(tm, (128, 64, 32, 16, 8))
    w_heads = w_uq.reshape(QL, H, NOPE + P)
    w_uq_r = jnp.concatenate(
        [jnp.concatenate([w_heads[:, h, :NOPE], _rope_weight_cols(w_heads[:, h, NOPE:])], axis=1)[None]
         for h in range(H)], axis=0)
    NQ = NOPE + 2 * LANES
    W = R + LANES
    HB = _pick(H, (Q_HEADS_PER_STEP, 2, 1))
    mod_spec = _mod_specs(shift, K, tm, rows_per_group)
    row = lambda i, h: (i, 0)
    fixed = lambda i, h: (0, 0)
    if head_major:
        tiles = rows_per_group // tm
        out_spec = pl.BlockSpec((None, HB, tm, W), lambda i, h: (i // tiles, h, i % tiles, 0))
        out_shape = jax.ShapeDtypeStruct((M // rows_per_group, H, rows_per_group, W), BF)
    else:
        out_spec = pl.BlockSpec((tm, HB * W), lambda i, h: (i, h))
        out_shape = jax.ShapeDtypeStruct((M, H * W), BF)
    return pl.pallas_call(
        functools.partial(_q_kernel, rows=rows, NOPE=NOPE, R=R, HB=HB, head_major=head_major),
        grid=(M // tm, H // HB),
        in_specs=[pl.BlockSpec((tm, K), row), pl.BlockSpec((1, K), fixed), mod_spec, mod_spec,
                  pl.BlockSpec((K, QL), fixed), pl.BlockSpec((1, QL), fixed),
                  pl.BlockSpec((HB, QL, NQ), lambda i, h: (h, 0, 0)),
                  pl.BlockSpec((R, HB * NOPE), lambda i, h: (0, h)),
                  pl.BlockSpec((tm, LANES), row), pl.BlockSpec((tm, LANES), row)],
        out_specs=out_spec,
        out_shape=out_shape,
        scratch_shapes=[pltpu.VMEM((tm, QL), BF), pltpu.VMEM((tm, K), BF)],
        compiler_params=_params(2),
        name=name,
    )(x, g.reshape(1, K), shift, scale, w_dq, g_q.reshape(1, QL), w_uq_r, w_uk, cos_t, sin_t)


def _attn_prompt_kernel(q_ref, k_ref, o_ref, m_scr, l_scr, a_scr, acc_scr, s_scr, p_scr, *, tq, tk, H, R, scale):
    qi = pl.program_id(1)
    kj = pl.program_id(2)

    @pl.when(kj == 0)
    def _():
        m_scr[...] = jnp.full_like(m_scr, -jnp.inf)
        l_scr[...] = jnp.zeros_like(l_scr)
        acc_scr[...] = jnp.zeros_like(acc_scr)

    rows = H * tq
    hg = max(1, ATTN_GROUP_ROWS // tq)
    group = hg * tq
    n_groups = rows // group
    rc = min(ATTN_SOFTMAX_ROWS, group)
    c2 = scale * LOG2_E

    def scores(g):
        q = q_ref[g * hg:(g + 1) * hg].reshape(group, q_ref.shape[-1])
        s_scr[g * group:(g + 1) * group, :] = _dot_nt(q, k_ref[...])

    def softmax(g, masked):
        for r0 in range(g * group, (g + 1) * group, rc):
            sl = slice(r0, r0 + rc)
            s = s_scr[sl, :]
            if masked:
                qpos = qi * tq + lax.rem(r0 + lax.broadcasted_iota(jnp.int32, (rc, tk), 0), tq)
                kpos = kj * tk + lax.broadcasted_iota(jnp.int32, (rc, tk), 1)
                s = jnp.where(kpos <= qpos, s, -jnp.inf)
            m_old = m_scr[sl, :]
            m_new = jnp.maximum(m_old, jnp.max(s, axis=-1, keepdims=True))
            alpha = jnp.exp2((m_old - m_new) * c2)
            p = jnp.exp2((s - m_new) * c2)
            l_scr[sl, :] = alpha * l_scr[sl, :] + jnp.sum(p, axis=-1, keepdims=True)
            m_scr[sl, :] = m_new
            a_scr[sl, :] = alpha
            p_scr[sl, :] = p.astype(BF)

    def values(g):
        gs = slice(g * group, (g + 1) * group)
        acc_scr[gs, :] = a_scr[gs, :] * acc_scr[gs, :] + _dot(p_scr[gs, :], k_ref[:, :R])

    def block(masked):
        for g in range(n_groups + 2):
            if g < n_groups:
                scores(g)
            if 1 <= g <= n_groups:
                softmax(g - 1, masked)
            if g >= 2:
                values(g - 2)

    crosses_diagonal = kj * tk + tk - 1 > qi * tq

    @pl.when((kj * tk <= qi * tq + tq - 1) & crosses_diagonal)
    def _():
        block(True)

    @pl.when(jnp.logical_not(crosses_diagonal))
    def _():
        block(False)

    @pl.when(kj == pl.num_programs(2) - 1)
    def _():
        o = acc_scr[...] / l_scr[...]
        o_ref[...] = o.reshape(H, tq, R).astype(o_ref.dtype)


def attention_prompt(q4, kcat, *, R, scale, name):
    B, H, L, W = q4.shape
    tq = _pick(L, (128, 64, 32, 16))
    tk = _pick(L, (512, 256, 128, 64, 32, 16))
    nq, nk = L // tq, L // tk

    def k_index(b, i, j):
        return (b, jnp.minimum(j, (i * tq + tq - 1) // tk), 0)

    return pl.pallas_call(
        functools.partial(_attn_prompt_kernel, tq=tq, tk=tk, H=H, R=R, scale=scale),
        grid=(B, nq, nk),
        in_specs=[pl.BlockSpec((None, H, tq, W), lambda b, i, j: (b, 0, i, 0)),
                  pl.BlockSpec((None, tk, W), k_index)],
        out_specs=pl.BlockSpec((None, H, tq, R), lambda b, i, j: (b, 0, i, 0)),
        out_shape=jax.ShapeDtypeStruct((B, H, L, R), BF),
        scratch_shapes=[pltpu.VMEM((H * tq, 1), F32), pltpu.VMEM((H * tq, 1), F32), pltpu.VMEM((H * tq, 1), F32),
                        pltpu.VMEM((H * tq, R), F32), pltpu.VMEM((H * tq, tk), F32), pltpu.VMEM((H * tq, tk), BF)],
        compiler_params=_params(3),
        name=name,
    )(q4, kcat)


def _attn_sample_kernel(pt_ref, q_ref, knew_ref, ckv_hbm, kpe_hbm, o_ref, m_scr, l_scr, acc_scr,
                        ckv_buf, kpe_buf, sems, *, NP, T, H, R, P, scale):
    b = pl.program_id(0)
    j = pl.program_id(1)
    nj = pl.num_programs(1)
    step = b * nj + j
    n_steps = pl.num_programs(0) * nj
    n_slots = ckv_buf.shape[0]
    slot = lax.rem(step, n_slots)
    rows = T * H

    def page_copies(st):
        bb, jj, sl = lax.div(st, nj), lax.rem(st, nj), lax.rem(st, n_slots)
        copies = []
        for i in range(NP):
            pid = pt_ref[bb, jj * NP + i]
            copies.append(pltpu.make_async_copy(ckv_hbm.at[pid], ckv_buf.at[sl, i], sems.at[0, sl]))
            copies.append(pltpu.make_async_copy(kpe_hbm.at[pid], kpe_buf.at[sl, i], sems.at[1, sl]))
        return copies

    @pl.when(step == 0)
    def _():
        for ahead in range(n_slots - 1):
            @pl.when(ahead < n_steps)
            def _():
                for cp in page_copies(step + ahead):
                    cp.start()

    @pl.when(step + n_slots - 1 < n_steps)
    def _():
        for cp in page_copies(step + n_slots - 1):
            cp.start()

    for cp in page_copies(step):
        cp.wait()
    q = q_ref[...]
    q_lat = q[:, :R]
    q_pe = q[:, R:R + P]

    c2 = scale * LOG2_E

    def update(state, s, values):
        m_old, l_old, acc = state
        m_new = jnp.maximum(m_old, jnp.max(s, axis=-1, keepdims=True))
        alpha = jnp.exp2((m_old - m_new) * c2)
        p = jnp.exp2((s - m_new) * c2)
        return m_new, alpha * l_old + jnp.sum(p, axis=-1, keepdims=True), alpha * acc + values(p.astype(BF))

    @pl.when(j == 0)
    def _():
        tp = -(-T // BF16_ROWS) * BF16_ROWS
        knew = jnp.concatenate([knew_ref[...], jnp.zeros((tp - T, knew_ref.shape[1]), BF)], axis=0)
        s = _dot_nt(q, knew)
        qt = lax.broadcasted_iota(jnp.int32, (rows, tp), 0) // H
        kt = lax.broadcasted_iota(jnp.int32, (rows, tp), 1)
        s = jnp.where(kt <= qt, s, -jnp.inf)
        init = (jnp.full((rows, 1), -jnp.inf, F32), jnp.zeros((rows, 1), F32), jnp.zeros((rows, R), F32))
        m_scr[...], l_scr[...], acc_scr[...] = update(init, s, lambda p: _dot(p, knew[:, :R]))

    pg = min(ATTN_PAGE_GROUP, NP)
    n_groups = NP // pg

    def load(g):
        return [ckv_buf[slot, i].astype(BF) for i in range(g * pg, (g + 1) * pg)]

    def scores(g, pages):
        return jnp.concatenate([_dot_nt(q_lat, pages[i]) + _dot(q_pe, kpe_buf[slot, g * pg + i].astype(BF))
                                for i in range(pg)], axis=1)

    def values_of(pages):
        def values(p):
            page = pages[0].shape[0]
            out = _dot(p[:, :page], pages[0])
            for i in range(1, pg):
                out = out + _dot(p[:, i * page:(i + 1) * page], pages[i])
            return out
        return values

    state = (m_scr[...], l_scr[...], acc_scr[...])
    pages = load(0)
    s = scores(0, pages)
    for g in range(n_groups):
        if g + 1 < n_groups:
            pages_next = load(g + 1)
            s_next = scores(g + 1, pages_next)
        state = update(state, s, values_of(pages))
        if g + 1 < n_groups:
            pages, s = pages_next, s_next
    m_scr[...], l_scr[...], acc_scr[...] = state

    @pl.when(j == pl.num_programs(1) - 1)
    def _():
        o_ref[...] = (acc_scr[...] / l_scr[...]).astype(o_ref.dtype)


def attention_sample(q3, knew, cache_ckv, cache_kpe_t, page_table, *, T, H, R, P, scale, name):
    Bd, rows, W = q3.shape
    n_pages = page_table.shape[1]
    page = cache_ckv.shape[1]
    NP = _pick(n_pages, (16, 8, 4, 2, 1))
    any_space = pl.BlockSpec(memory_space=pl.ANY)
    grid_spec = pltpu.PrefetchScalarGridSpec(
        num_scalar_prefetch=1,
        grid=(Bd, n_pages // NP),
        in_specs=[pl.BlockSpec((None, rows, W), lambda b, j, pt: (b, 0, 0)),
                  pl.BlockSpec((None, T, W), lambda b, j, pt: (b, 0, 0)),
                  any_space, any_space],
        out_specs=pl.BlockSpec((None, rows, R), lambda b, j, pt: (b, 0, 0)),
        scratch_shapes=[pltpu.VMEM((rows, 1), F32), pltpu.VMEM((rows, 1), F32), pltpu.VMEM((rows, R), F32),
                        pltpu.VMEM((ATTN_PAGE_SLOTS, NP, page, R), F32),
                        pltpu.VMEM((ATTN_PAGE_SLOTS, NP, P, page), F32),
                        pltpu.SemaphoreType.DMA((2, ATTN_PAGE_SLOTS))],
    )
    return pl.pallas_call(
        functools.partial(_attn_sample_kernel, NP=NP, T=T, H=H, R=R, P=P, scale=scale),
        grid_spec=grid_spec,
        out_shape=jax.ShapeDtypeStruct((Bd, rows, R), BF),
        compiler_params=_params(2),
        name=name,
    )(page_table, q3, knew, cache_ckv, cache_kpe_t)


def _mm_kernel(a_ref, w_ref, o_ref):
    o_ref[...] = _dot(a_ref[...], w_ref[...].astype(BF)).astype(o_ref.dtype)


def value_up(o_lat, w_uv, *, H, R, V, head_major, name):
    if head_major:
        G, _, L, _ = o_lat.shape
        M = G * L
        tm = _pick(L, (1024, 512, 256, 128, 64, 32, 16, 8))
        tiles = L // tm
        a_spec = pl.BlockSpec((None, None, tm, R), lambda i, h: (i // tiles, h, i % tiles, 0))
    else:
        M = o_lat.shape[0]
        tm = _pick(M, (1024, 512, 256, 128, 64, 32, 16, 8))
        a_spec = pl.BlockSpec((tm, R), lambda i, h: (i, h))
    return pl.pallas_call(
        _mm_kernel,
        grid=(M // tm, H),
        in_specs=[a_spec, pl.BlockSpec((R, V), lambda i, h: (0, h))],
        out_specs=pl.BlockSpec((tm, V), lambda i, h: (i, h)),
        out_shape=jax.ShapeDtypeStruct((M, H * V), BF),
        compiler_params=_params(2),
        name=name,
    )(o_lat, w_uv)


def _rope_tables(pos, P, reps):
    half = P // 2
    inv = ROPE_THETA ** (-jnp.arange(half, dtype=F32) / half)
    ang = pos.astype(F32)[:, None] * inv[None, :]
    cos, sin = jnp.cos(ang), jnp.sin(ang)
    zeros = jnp.zeros((pos.shape[0], LANES - P), F32)
    cos_t = jnp.concatenate([cos, cos, zeros], axis=1)
    sin_t = jnp.concatenate([-sin, sin, zeros], axis=1)
    return jnp.tile(cos_t, (reps, 1)), jnp.tile(sin_t, (reps, 1))


def _trunk(x3, mods, pos, conv_in, ssm_in, past, p, tag):
    G, L, D = x3.shape
    M = G * L
    grouped = L % SUBLANES == 0
    x = x3.reshape(M, D)

    def split_mods(m, n):
        parts = jnp.split(m, n, axis=-1)
        if grouped:
            return [t[:, None, :] for t in parts]
        return [jnp.repeat(t, L, axis=0) for t in parts]

    HV, DK, DV = p['HV'], p['DK'], p['DV']
    HQK = p['HQK']
    KEY, VAL = HQK * DK, HV * DV
    H, NOPE, P, R, V = p['H'], p['NOPE'], p['P'], p['R'], p['V']
    FF = p['w_down'].shape[1]
    cos_t, sin_t = _rope_tables(pos, P, G)
    scale = (NOPE + P) ** -0.5

    sh1, sc1, gt1, sh2, sc2, gt2 = split_mods(mods['l0'], 6)
    w_in_t = p['gdn_w_in_t']
    n_proj = 2 * KEY + 2 * VAL
    proj = mod_matmul(x, p['g_mix'][0], sh1, sc1, w_in_t, 0, n_proj, L, w_is_nk=True, name=f'{tag}_gdn_in')
    w_ba_t = jnp.concatenate([w_in_t[0, n_proj:], jnp.zeros((LANES - 2 * HV, D), F32)], axis=0)
    ba = mod_matmul(x, p['g_mix'][0], sh1, sc1, w_ba_t, 0, LANES, L, w_is_nk=True, name=f'{tag}_gdn_ba')
    proj3 = proj.reshape(G, L, n_proj)
    o_g, ssm_new = gdn_scan(
        proj3, ba[:, :HV], ba[:, HV:2 * HV], conv_in, ssm_in, p['gdn_w_conv'][0], p['gdn_a_log'][0],
        p['gdn_dt_bias'][0], p['gdn_g_norm'][0], B=G, L=L, HQK=HQK, HV=HV, DK=DK, DV=DV, name=f'{tag}_gdn_scan')
    conv_new = proj3[:, L - (CONV_W - 1):, :2 * KEY + VAL]
    x = matmul_residual(o_g.reshape(M, VAL), p['gdn_w_out'], 0, x, gt1, L, name=f'{tag}_gdn_out')
    hff = mod_matmul(x, p['g_ffn'][0], sh2, sc2, p['w_gate_up'], 0, FF, L, swiglu=True, out_dtype=BF,
                     name=f'{tag}_ffn0_up')
    x = matmul_residual(hff, p['w_down'], 0, x, gt2, L, name=f'{tag}_ffn0_down')

    shk, sck = split_mods(mods['kv'], 2)
    ckv, kpe, kcat = shared_kv(x, p['kv_g_in'], shk, sck, p['kv_w_down'], p['kv_g_norm'], cos_t, sin_t, L,
                               R=R, P=P, name=f'{tag}_kv')

    sh1, sc1, gt1, sh2, sc2, gt2 = split_mods(mods['l1'], 6)
    head_major = past is None
    q = mla_queries(x, p['g_mix'][1], sh1, sc1, p['mla_w_dq'][0], p['mla_g_q'][0], p['mla_w_uq'][0], p['kv_w_uk'],
                    cos_t, sin_t, L, H=H, NOPE=NOPE, P=P, R=R, head_major=head_major, name=f'{tag}_q')
    if head_major:
        o_lat = attention_prompt(q, kcat.reshape(G, L, R + LANES), R=R, scale=scale, name=f'{tag}_attn')
    else:
        cache_ckv, cache_kpe, page_table = past
        o_lat = attention_sample(q.reshape(G, L * H, R + LANES), kcat.reshape(G, L, R + LANES), cache_ckv,
                                 cache_kpe, page_table, T=L, H=H, R=R, P=P, scale=scale, name=f'{tag}_attn')
        o_lat = o_lat.reshape(M, H * R)
    o = value_up(o_lat, p['kv_w_uv'], H=H, R=R, V=V, head_major=head_major, name=f'{tag}_uv')
    x = matmul_residual(o, p['mla_w_o'], 0, x, gt1, L, name=f'{tag}_attn_out')
    hff = mod_matmul(x, p['g_ffn'][1], sh2, sc2, p['w_gate_up'], 1, FF, L, swiglu=True, out_dtype=BF,
                     name=f'{tag}_ffn1_up')
    x = matmul_residual(hff, p['w_down'], 1, x, gt2, L, name=f'{tag}_ffn1_down')

    shf, scf = split_mods(mods['final'], 2)
    y = modulate_rows(x, p['final_g'], shf, scf, L, name=f'{tag}_final')
    return (y.reshape(G, L, D), conv_new[None], ssm_new[None], ckv.reshape(G, L, R), kpe.reshape(G, L, P))


def kernel(x_prompt, x_sample, c_prompt, c_sample, cache_ckv, cache_kpe, page_table, state_ssm, state_conv, w_ada, b_ada, g_mix, g_ffn, w_gate_up, w_down, gdn_w_in, gdn_w_conv, gdn_a_log, gdn_dt_bias, gdn_g_norm, gdn_w_out, kv_w_ada, kv_b_ada, kv_g_in, kv_w_down, kv_g_norm, kv_w_uk, kv_w_uv, mla_w_dq, mla_g_q, mla_w_uq, mla_w_o, final_w_ada, final_b_ada, final_g):
    B, S, D = x_prompt.shape
    Bd, T, _ = x_sample.shape
    HV, DK, DV = state_ssm.shape[2:]
    KEY = (state_conv.shape[-1] - HV * DV) // 2
    R = cache_ckv.shape[-1]
    P = cache_kpe.shape[-1]
    QL = mla_w_dq.shape[-1]
    nope_total = kv_w_uk.shape[1]
    H = (mla_w_uq.shape[-1] - nope_total) // P
    p = dict(w_ada=w_ada, b_ada=b_ada, g_mix=g_mix, g_ffn=g_ffn, w_gate_up=w_gate_up, w_down=w_down,
             gdn_w_in_t=jnp.swapaxes(gdn_w_in, 1, 2), gdn_w_conv=gdn_w_conv, gdn_a_log=gdn_a_log, gdn_dt_bias=gdn_dt_bias,
             gdn_g_norm=gdn_g_norm, gdn_w_out=gdn_w_out, kv_g_in=kv_g_in, kv_w_down=kv_w_down,
             kv_g_norm=kv_g_norm, kv_w_uk=kv_w_uk, kv_w_uv=kv_w_uv, mla_w_dq=mla_w_dq, mla_g_q=mla_g_q,
             mla_w_uq=mla_w_uq, mla_w_o=mla_w_o, final_g=final_g,
             HV=HV, DK=DK, DV=DV, HQK=KEY // DK, H=H, NOPE=nope_total // H, P=P, R=R,
             V=kv_w_uv.shape[1] // H)

    n_c = B + Bd
    pad = -n_c % SUBLANES
    c_all = jnp.concatenate([c_prompt, c_sample, jnp.zeros((pad, D), F32)], axis=0)
    m_l0 = ada_dense(c_all, w_ada, b_ada, 0, name='ada_l0')
    m_l1 = ada_dense(c_all, w_ada, b_ada, 1, name='ada_l1')
    m_kv = ada_dense(c_all, kv_w_ada, kv_b_ada, 0, name='ada_kv')
    m_f = ada_dense(c_all, final_w_ada, final_b_ada, 0, name='ada_final')

    def mods(lo, hi):
        return dict(l0=m_l0[lo:hi], l1=m_l1[lo:hi], kv=m_kv[lo:hi], final=m_f[lo:hi])

    y_p, conv_p, ssm_p, ckv_p, kpe_p = _trunk(x_prompt, mods(0, B), jnp.arange(S), None, None, None, p, 'p')
    past_len = page_table.shape[1] * cache_ckv.shape[1]
    y_s, conv_s, ssm_s, ckv_s, kpe_s = _trunk(x_sample, mods(B, n_c), past_len + jnp.arange(T), state_conv[0],
                                              state_ssm[0], (cache_ckv, jnp.swapaxes(cache_kpe, 1, 2), page_table),
                                              p, 's')
    return (y_p, y_s, ssm_p, conv_p, ckv_p, kpe_p, ssm_s, conv_s, ckv_s, kpe_s)
```

```python
import functools

import jax
import jax.numpy as jnp
from jax import lax
from jax.experimental import pallas as pl
from jax.experimental.pallas import tpu as pltpu

EPS = 1e-6
ROPE_THETA = 10000.0
CONV_W = 4
GDN_CHUNK = 64
F32 = jnp.float32
BF = jnp.bfloat16

V7X_VMEM_BYTES = 64 * 1024 * 1024
VMEM_LIMIT = V7X_VMEM_BYTES - 8 * 1024 * 1024
VMEM_TILE_BUDGET = (VMEM_LIMIT * 9) // 10
LANES = 128
SUBLANES = 8
BF16_ROWS = 16
TRI_BLOCK = 16
STACK_ROWS = 128
STACKS_PER_STEP = 8
ATTN_GROUP_ROWS = 256
ATTN_SOFTMAX_ROWS = 64
ATTN_PAGE_GROUP = 4
ATTN_PAGE_SLOTS = 3
Q_HEADS_PER_STEP = 4
LOG2_E = 1.4426950408889634


def _params(n_axes):
    return pltpu.CompilerParams(dimension_semantics=("arbitrary",) * n_axes, vmem_limit_bytes=VMEM_LIMIT)


def _pick(n, cands):
    for c in cands:
        if n % c == 0:
            return c
    return n


def _silu(x):
    return x * jax.nn.sigmoid(x)


def _contract(a, b, ca, cb):
    batch = tuple(range(a.ndim - 2))
    dims = (((a.ndim - 2 + ca,), (b.ndim - 2 + cb,)), (batch, batch))
    return lax.dot_general(a, b, dims, preferred_element_type=F32)


def _dot(a, b):
    return _contract(a, b, 1, 0)


def _dot_nt(a, b):
    return _contract(a, b, 1, 1)


def _dot_tn(a, b):
    return _contract(a, b, 0, 0)


def _split_bf16(a):
    hi = a.astype(BF)
    lo = (a - hi.astype(F32)).astype(BF)
    return hi, lo


def _dot3(a, b):
    ah, al = _split_bf16(a)
    bh, bl = _split_bf16(b)
    return _dot(ah, bh) + _dot(ah, bl) + _dot(al, bh)


def _modulate_rows(x_ref, g_ref, sh_ref, sc_ref, h_ref, rows):
    tm = x_ref.shape[0]
    per_row = sh_ref.shape[0] != 1

    def body(r, carry):
        sl = pl.ds(pl.multiple_of(r * rows, rows), rows)
        x = x_ref[sl, :]
        y = x * lax.rsqrt(jnp.mean(x * x, axis=-1, keepdims=True) + EPS) * g_ref[...]
        sc = sc_ref[sl, :] if per_row else sc_ref[...]
        sh = sh_ref[sl, :] if per_row else sh_ref[...]
        h_ref[sl, :] = (y * (1.0 + sc) + sh).astype(h_ref.dtype)
        return carry

    lax.fori_loop(0, tm // rows, body, 0)


def _grouped(mod):
    return mod[0].ndim == 3


def _mod_spec(mod, K, tm, rows_per_group):
    arr, k = mod
    if arr.ndim == 3:
        tiles_per_group = rows_per_group // tm
        return pl.BlockSpec((None, 1, K), lambda i, *_: (i // tiles_per_group, 0, k))
    return pl.BlockSpec((tm, K), lambda i, *_: (i, k))


def _w_spec(w, layer, K, tn, col_off_blocks=0):
    if w.ndim == 3:
        return pl.BlockSpec((None, K, tn), lambda i, j: (layer, 0, j + col_off_blocks))
    return pl.BlockSpec((K, tn), lambda i, j: (0, j + col_off_blocks))


def _modmm_kernel(x_ref, g_ref, sh_ref, sc_ref, w_ref, o_ref, h_ref, *, rows, w_is_nk):
    @pl.when(pl.program_id(1) == 0)
    def _():
        _modulate_rows(x_ref, g_ref, sh_ref, sc_ref, h_ref, rows)

    dot = _dot_nt if w_is_nk else _dot
    o_ref[...] = dot(h_ref[...], w_ref[...].astype(BF)).astype(o_ref.dtype)


def _modmm_swiglu_kernel(x_ref, g_ref, sh_ref, sc_ref, wg_ref, wu_ref, o_ref, h_ref, *, rows):
    @pl.when(pl.program_id(1) == 0)
    def _():
        _modulate_rows(x_ref, g_ref, sh_ref, sc_ref, h_ref, rows)

    h = h_ref[...]
    gate = _dot(h, wg_ref[...].astype(BF))
    up = _dot(h, wu_ref[...].astype(BF))
    o_ref[...] = (_silu(gate) * up).astype(o_ref.dtype)


def _row_tile(M, rows_per_group, grouped):
    base = rows_per_group if grouped else M
    return _pick(base, (1024, 512, 256, 128, 64, 32, 16, 8))


def _col_tile(N, fixed_bytes, bytes_per_col):
    for tn in (1024, 512, 256):
        if N % tn == 0 and fixed_bytes + bytes_per_col * tn <= VMEM_TILE_BUDGET:
            return tn
    return _pick(N, (128,))


def _weight_col_bytes(K, n_weights=1):
    return n_weights * K * (2 * 4 + 2)


def mod_matmul(x, g, shift, scale, w, layer, n_out, rows_per_group, *, swiglu=False, w_is_nk=False,
               out_dtype=F32, name):
    M, K = x.shape
    tm = _row_tile(M, rows_per_group, _grouped(shift))
    n_w = 2 if swiglu else 1
    out_bytes = jnp.dtype(out_dtype).itemsize
    tn = _col_tile(n_out, tm * K * (2 * 4 + 2),
                   _weight_col_bytes(K, n_w) + tm * (2 * out_bytes + 4 * n_w))
    rows = _pick(tm, (128, 64, 32, 16, 8))
    mod_specs = [_mod_spec(m, K, tm, rows_per_group) for m in (shift, scale)]
    if not w_is_nk:
        w_spec = _w_spec(w, layer, K, tn)
    elif w.ndim == 3:
        w_spec = pl.BlockSpec((None, tn, K), lambda i, j: (layer, j, 0))
    else:
        w_spec = pl.BlockSpec((tn, K), lambda i, j: (j, 0))
    in_specs = [pl.BlockSpec((tm, K), lambda i, j: (i, 0)),
                pl.BlockSpec((1, K), lambda i, j: (0, 0)),
                *mod_specs, w_spec]
    args = [x, g.reshape(1, K), shift[0], scale[0], w]
    if swiglu:
        in_specs.append(_w_spec(w, layer, K, tn, n_out // tn))
        args.append(w)
        body = functools.partial(_modmm_swiglu_kernel, rows=rows)
    else:
        body = functools.partial(_modmm_kernel, rows=rows, w_is_nk=w_is_nk)
    return pl.pallas_call(
        body,
        grid=(M // tm, n_out // tn),
        in_specs=in_specs,
        out_specs=pl.BlockSpec((tm, tn), lambda i, j: (i, j)),
        out_shape=jax.ShapeDtypeStruct((M, n_out), out_dtype),
        scratch_shapes=[pltpu.VMEM((tm, K), BF)],
        compiler_params=_params(2),
        name=name,
    )(*args)


def _mmres_kernel(a_ref, w_ref, res_ref, gate_ref, o_ref):
    y = _dot(a_ref[...], w_ref[...].astype(BF))
    o_ref[...] = res_ref[...] + gate_ref[...] * y


def matmul_residual(a, w, layer, res, gate, rows_per_group, *, name):
    M, K = a.shape
    N = res.shape[1]
    tm = _row_tile(M, rows_per_group, _grouped(gate))
    tn = _col_tile(N, tm * K * 2 * a.dtype.itemsize, _weight_col_bytes(K) + tm * (4 * 4 + 4))
    gate_arr, gate_k = gate
    gate_off = gate_k * (N // tn)
    if _grouped(gate):
        tiles_per_group = rows_per_group // tm
        gate_spec = pl.BlockSpec((None, 1, tn), lambda i, j: (i // tiles_per_group, 0, gate_off + j))
    else:
        gate_spec = pl.BlockSpec((tm, tn), lambda i, j: (i, gate_off + j))
    return pl.pallas_call(
        _mmres_kernel,
        grid=(M // tm, N // tn),
        in_specs=[pl.BlockSpec((tm, K), lambda i, j: (i, 0)),
                  _w_spec(w, layer, K, tn),
                  pl.BlockSpec((tm, tn), lambda i, j: (i, j)),
                  gate_spec],
        out_specs=pl.BlockSpec((tm, tn), lambda i, j: (i, j)),
        out_shape=jax.ShapeDtypeStruct((M, N), F32),
        compiler_params=_params(2),
        name=name,
    )(a, w, res, gate_arr)


def _ada_kernel(c_ref, w_ref, b_ref, o_ref):
    a = _silu(c_ref[...]).astype(BF)
    o_ref[...] = _dot(a, w_ref[...].astype(BF)) + b_ref[...]


def ada_dense(c, w, b, layer, *, name):
    M, K = c.shape
    N = w.shape[-1]
    tn = _pick(N, (512, 256, 128))
    if b.ndim == 2:
        b_spec = pl.BlockSpec((None, 1, tn), lambda i, j: (layer, 0, j))
        b = b.reshape(b.shape[0], 1, N)
    else:
        b_spec = pl.BlockSpec((1, tn), lambda i, j: (0, j))
        b = b.reshape(1, N)
    return pl.pallas_call(
        _ada_kernel,
        grid=(1, N // tn),
        in_specs=[pl.BlockSpec((M, K), lambda i, j: (0, 0)), _w_spec(w, layer, K, tn), b_spec],
        out_specs=pl.BlockSpec((M, tn), lambda i, j: (0, j)),
        out_shape=jax.ShapeDtypeStruct((M, N), F32),
        compiler_params=_params(2),
        name=name,
    )(c, w, b)


def _modulate_kernel(x_ref, g_ref, sh_ref, sc_ref, o_ref, *, rows):
    _modulate_rows(x_ref, g_ref, sh_ref, sc_ref, o_ref, rows)


def modulate_rows(x, g, shift, scale, rows_per_group, *, name):
    M, K = x.shape
    tm = _row_tile(M, rows_per_group, _grouped(shift))
    rows = _pick(tm, (128, 64, 32, 16, 8))
    mod_specs = [_mod_spec(m, K, tm, rows_per_group) for m in (shift, scale)]
    return pl.pallas_call(
        functools.partial(_modulate_kernel, rows=rows),
        grid=(M // tm,),
        in_specs=[pl.BlockSpec((tm, K), lambda i: (i, 0)), pl.BlockSpec((1, K), lambda i: (0, 0)),
                  *mod_specs],
        out_specs=pl.BlockSpec((tm, K), lambda i: (i, 0)),
        out_shape=jax.ShapeDtypeStruct((M, K), F32),
        compiler_params=_params(1),
        name=name,
    )(x, g.reshape(1, K), shift[0], scale[0])


def _dot1(a, b):
    return _dot(a.astype(BF), b.astype(BF))


def _tri_inverse(n_low, ii, jj, cp, nil):
    rn = n_low.shape[-1]
    eye = (ii == jj).astype(F32)
    base = min(TRI_BLOCK, cp)
    shift = base.bit_length() - 1
    nd = jnp.where((ii >> shift) == (jj >> shift), n_low, 0.0)
    p = eye - nd
    if nil > 2:
        npow = _dot1(nd, nd)
        pw = 2
        while 2 * pw - 1 < nil - 1:
            both = _dot1(jnp.concatenate([p, npow], axis=-2), npow)
            p = p + both[:, :rn]
            npow = both[:, rn:]
            pw *= 2
        p = p + _dot1(p, npow)
    size = base
    while size < cp:
        s = size.bit_length() - 1
        off = ((ii >> (s + 1)) == (jj >> (s + 1))) & (((ii >> s) & 1) == 1) & (((jj >> s) & 1) == 0)
        x = _dot1(jnp.where(off, n_low, 0.0), p)
        p = p - _dot1(p, x)
        size *= 2
    resid = (eye - p) - _dot3(n_low, p)
    return p + _dot1(p, resid)


def _delta_chunk(q_st, k_st, v_st, beta_row, g_row, s_scr, *, G, Cp, nil):
    S, rn, dv = v_st.shape
    ii = lax.broadcasted_iota(jnp.int32, (1, rn, rn), 1)
    jj = lax.broadcasted_iota(jnp.int32, (1, rn, rn), 2)
    sh = Cp.bit_length() - 1
    same = (ii >> sh) == (jj >> sh)
    eye = ii == jj
    causal = same & (jj <= ii)
    g_mat = jnp.broadcast_to(g_row, (S, rn, rn))
    g_col = jnp.sum(jnp.where(eye, g_mat, 0.0), axis=2, keepdims=True)
    gc_col = jnp.sum(jnp.where(causal, g_mat, 0.0), axis=2, keepdims=True)
    gl_col = jnp.sum(jnp.where(same, g_mat, 0.0), axis=2, keepdims=True)
    gc_row = jnp.sum(jnp.where(same & (ii <= jj), jnp.broadcast_to(g_col, (S, rn, rn)), 0.0), axis=1, keepdims=True)
    beta_col = jnp.sum(jnp.where(eye, jnp.broadcast_to(beta_row, (S, rn, rn)), 0.0), axis=2, keepdims=True)
    decay = jnp.exp(jnp.where(causal, gc_col - gc_row, -jnp.inf))
    eg = jnp.exp(gc_col)
    kb = k_st * beta_col
    both = _dot_nt(jnp.concatenate([kb, q_st], axis=1).astype(BF), k_st.astype(BF))
    n_low = jnp.where(same & (jj < ii), both[:, :rn] * decay, 0.0)
    t_inv = _tri_inverse(n_low, ii, jj, Cp, nil).astype(BF)
    uw = _dot(t_inv, jnp.concatenate([v_st * beta_col, kb * eg], axis=2).astype(BF))
    u, w = uw[:, :, :dv], uw[:, :, dv:]
    qe = q_st * eg
    k_dec = (k_st * jnp.exp(gl_col - gc_col)).astype(BF)
    v_new, q_s = [], []
    for h in range(G):
        r = slice(h * Cp, (h + 1) * Cp)
        s_old = jnp.stack([s_scr[s * G + h] for s in range(S)], axis=0)
        ws_qs = _dot(jnp.concatenate([w[:, r], qe[:, r]], axis=1).astype(BF), s_old.astype(BF))
        v_new_h = u[:, r] - ws_qs[:, :Cp]
        q_s.append(ws_qs[:, Cp:])
        v_new.append(v_new_h)
        s_new = s_old * jnp.exp(gl_col[:, h * Cp:h * Cp + 1]) + _dot_tn(k_dec[:, r], v_new_h.astype(BF))
        for s in range(S):
            s_scr[s * G + h] = s_new[s]
    v_new = jnp.concatenate(v_new, axis=1).astype(BF)
    return jnp.concatenate(q_s, axis=1) + _dot((both[:, rn:] * decay).astype(BF), v_new)


def _gdn_kernel(*refs, C, Cp, G, NS, rep, DK, DV, has_state):
    q_ref, k_ref, v_ref, z_ref, b_ref, a_ref, alog_ref, dtb_ref, wq_ref, wk_ref, wv_ref, gn_ref = refs[:12]
    pos = 12
    if has_state:
        cq_ref, ck_ref, cv_ref, s0_ref = refs[pos:pos + 4]
        pos += 4
    o_ref, so_ref = refs[pos:pos + 2]
    s_scr, cbq, cbk, cbv = refs[pos + 2:]

    c = pl.program_id(2)
    last = pl.num_programs(2) - 1
    tail_lo = SUBLANES - (CONV_W - 1)
    Cc = -(-C // SUBLANES) * SUBLANES

    @pl.when(c == 0)
    def _():
        for cb in (cbq, cbk, cbv):
            cb[...] = jnp.zeros_like(cb)
        if has_state:
            cbq[tail_lo:SUBLANES, :] = cq_ref[...]
            cbk[tail_lo:SUBLANES, :] = ck_ref[...]
            cbv[tail_lo:SUBLANES, :] = cv_ref[...]
            s_scr[...] = s0_ref[...]
        else:
            s_scr[...] = jnp.zeros_like(s_scr)

    def conv(cb, x_ref, w_ref):
        cb[SUBLANES:SUBLANES + C, :] = x_ref[...]
        acc = cb[tail_lo:tail_lo + Cc, :] * w_ref[0:1, :]
        for j in range(1, CONV_W):
            acc = acc + cb[tail_lo + j:tail_lo + j + Cc, :] * w_ref[j:j + 1, :]
        cb[tail_lo:SUBLANES, :] = cb[C + tail_lo:C + SUBLANES, :]
        return _silu(acc)

    qc = conv(cbq, q_ref, wq_ref)
    kc = conv(cbk, k_ref, wk_ref)
    vc = conv(cbv, v_ref, wv_ref)

    padded = Cp != C
    if padded:
        row_ok = lax.broadcasted_iota(jnp.int32, (Cc, 1), 0) < C

    def pad_rows(x):
        return x if Cp == Cc else jnp.concatenate([x, jnp.zeros((Cp - Cc, x.shape[1]), F32)], axis=0)

    q_parts, k_parts, v_parts = [], [], []
    for i in range(G // rep):
        qh = qc[:, i * DK:(i + 1) * DK]
        kh = kc[:, i * DK:(i + 1) * DK]
        qh = qh * lax.rsqrt(jnp.sum(qh * qh, axis=-1, keepdims=True) + EPS) * (DK ** -0.5)
        kh = kh * lax.rsqrt(jnp.sum(kh * kh, axis=-1, keepdims=True) + EPS)
        if padded:
            kh = jnp.where(row_ok, kh, 0.0)
        for e in range(rep):
            hl = i * rep + e
            vh = vc[:, hl * DV:(hl + 1) * DV]
            if padded:
                vh = jnp.where(row_ok, vh, 0.0)
            q_parts.append(pad_rows(qh))
            k_parts.append(pad_rows(kh))
            v_parts.append(pad_rows(vh))
    beta_row = jax.nn.sigmoid(b_ref[c])
    sp_in = a_ref[c] + dtb_ref[...]
    softplus = jnp.maximum(sp_in, 0.0) + jnp.log1p(jnp.exp(-jnp.abs(sp_in)))
    g_row = -jnp.exp(alog_ref[...]) * softplus
    if padded:
        lane_ok = (lax.broadcasted_iota(jnp.int32, (1, G * Cp), 1) & (Cp - 1)) < C
        beta_row = jnp.where(lane_ok, beta_row, 0.0)
        g_row = jnp.where(lane_ok, g_row, 0.0)

    GS = G // NS
    rn = GS * Cp

    def stacks(parts):
        return jnp.stack([jnp.concatenate(parts[s * GS:(s + 1) * GS], axis=0) for s in range(NS)], axis=0)

    def stack_lanes(row):
        return jnp.stack([row[:, s * rn:(s + 1) * rn] for s in range(NS)], axis=0)

    o = _delta_chunk(stacks(q_parts), stacks(k_parts), stacks(v_parts), stack_lanes(beta_row), stack_lanes(g_row),
                     s_scr, G=GS, Cp=Cp, nil=min(C, TRI_BLOCK))
    on = o * lax.rsqrt(jnp.mean(o * o, axis=-1, keepdims=True) + EPS) * gn_ref[...]
    for s in range(NS):
        for h in range(GS):
            hl = s * GS + h
            zg = z_ref[:, hl * DV:(hl + 1) * DV]
            o_ref[:, hl * DV:(hl + 1) * DV] = (on[s, h * Cp:h * Cp + C] * _silu(zg)).astype(o_ref.dtype)

    @pl.when(c == last)
    def _():
        so_ref[...] = s_scr[...]


def gdn_scan(proj, b_logit, a_logit, conv_prev, s0, w_conv, a_log, dt_bias, g_norm, *, B, L, HQK, HV, DK, DV, name):
    rep = HV // HQK
    KEY, VAL = HQK * DK, HV * DV
    C = _pick(L, (GDN_CHUNK,))
    n = L // C
    Cp = -(-C // BF16_ROWS) * BF16_ROWS
    assert Cp & (Cp - 1) == 0 and L >= CONV_W - 1
    GS = min(HV, max(rep, STACK_ROWS // Cp))
    NS = _pick(HV // GS, (STACKS_PER_STEP, 4, 2, 1))
    G = NS * GS
    assert HV % G == 0 and GS % rep == 0
    has_state = s0 is not None
    qw, vw = (G // rep) * DK, G * DV
    k_off, v_off, z_off = KEY // qw, 2 * KEY // vw, (2 * KEY + VAL) // vw

    def stack_rows(t):
        t = t.reshape(B, n, C, HV // G, G)
        t = jnp.pad(t, ((0, 0), (0, 0), (0, Cp - C), (0, 0), (0, 0)))
        return jnp.transpose(t, (0, 3, 1, 4, 2)).reshape(B, HV // G, n, 1, G * Cp)

    def stack_heads(v):
        return jnp.repeat(v.reshape(HV // G, 1, G), Cp, axis=-1)

    gate_spec = pl.BlockSpec((None, None, n, 1, G * Cp), lambda b, h, c: (b, h, 0, 0, 0))
    head_spec = pl.BlockSpec((None, 1, G * Cp), lambda b, h, c: (h, 0, 0))
    in_specs = [
        pl.BlockSpec((None, C, qw), lambda b, h, c: (b, c, h)),
        pl.BlockSpec((None, C, qw), lambda b, h, c: (b, c, k_off + h)),
        pl.BlockSpec((None, C, vw), lambda b, h, c: (b, c, v_off + h)),
        pl.BlockSpec((None, C, vw), lambda b, h, c: (b, c, z_off + h)),
        gate_spec, gate_spec, head_spec, head_spec,
        pl.BlockSpec((CONV_W, qw), lambda b, h, c: (0, h)),
        pl.BlockSpec((CONV_W, qw), lambda b, h, c: (0, k_off + h)),
        pl.BlockSpec((CONV_W, vw), lambda b, h, c: (0, v_off + h)),
        pl.BlockSpec((1, DV), lambda b, h, c: (0, 0)),
    ]
    args = [proj, proj, proj, proj, stack_rows(b_logit), stack_rows(a_logit), stack_heads(a_log),
            stack_heads(dt_bias), w_conv, w_conv, w_conv, g_norm.reshape(1, DV)]
    if has_state:
        in_specs += [
            pl.BlockSpec((None, CONV_W - 1, qw), lambda b, h, c: (b, 0, h)),
            pl.BlockSpec((None, CONV_W - 1, qw), lambda b, h, c: (b, 0, k_off + h)),
            pl.BlockSpec((None, CONV_W - 1, vw), lambda b, h, c: (b, 0, v_off + h)),
            pl.BlockSpec((None, G, DK, DV), lambda b, h, c: (b, h, 0, 0)),
        ]
        args += [conv_prev, conv_prev, conv_prev, s0]
    body = functools.partial(_gdn_kernel, C=C, Cp=Cp, G=G, NS=NS, rep=rep, DK=DK, DV=DV, has_state=has_state)
    return pl.pallas_call(
        body,
        grid=(B, HV // G, n),
        in_specs=in_specs,
        out_specs=[pl.BlockSpec((None, C, vw), lambda b, h, c: (b, c, h)),
                   pl.BlockSpec((None, G, DK, DV), lambda b, h, c: (b, h, 0, 0))],
        out_shape=[jax.ShapeDtypeStruct((B, L, VAL), BF), jax.ShapeDtypeStruct((B, HV, DK, DV), F32)],
        scratch_shapes=[pltpu.VMEM((G, DK, DV), F32),
                        pltpu.VMEM((SUBLANES + Cp, qw), F32),
                        pltpu.VMEM((SUBLANES + Cp, qw), F32),
                        pltpu.VMEM((SUBLANES + Cp, vw), F32)],
        compiler_params=_params(3),
        name=name,
    )(*args)


def _kv_kernel(x_ref, g_ref, sh_ref, sc_ref, w_ref, gkv_ref, cos_ref, sin_ref,
               ckv_ref, kpe_ref, kcat_ref, h_ref, *, rows, R, P):
    _modulate_rows(x_ref, g_ref, sh_ref, sc_ref, h_ref, rows)
    y = _dot(h_ref[...], w_ref[...].astype(BF))
    c = y[:, :R]
    ckv = c * lax.rsqrt(jnp.mean(c * c, axis=-1, keepdims=True) + EPS) * gkv_ref[...]
    rot = y[:, R:R + LANES] * cos_ref[...] + y[:, R + LANES:R + 2 * LANES] * sin_ref[...]
    ckv_ref[...] = ckv
    kpe_ref[...] = rot[:, :P]
    kcat_ref[:, :R] = ckv.astype(BF)
    kcat_ref[:, R:] = rot.astype(BF)


def _rope_weight_cols(w_pe):
    P = w_pe.shape[-1]
    half = P // 2
    zeros = jnp.zeros(w_pe.shape[:-1] + (LANES - P,), w_pe.dtype)
    swapped = jnp.concatenate([w_pe[..., half:], w_pe[..., :half]], axis=-1)
    return jnp.concatenate([w_pe, zeros, swapped, zeros], axis=-1)


def kv_down_weight(w_down, R):
    return jnp.concatenate([w_down[:, :R], _rope_weight_cols(w_down[:, R:])], axis=1)


def query_up_weight(w_uq, H, NOPE):
    QL = w_uq.shape[0]
    w_heads = w_uq.reshape(QL, H, -1)
    per_head = jnp.concatenate([w_heads[..., :NOPE], _rope_weight_cols(w_heads[..., NOPE:])], axis=-1)
    return jnp.transpose(per_head, (1, 0, 2))


def shared_kv(x, g, shift, scale, w_ext, g_kv, cos_t, sin_t, rows_per_group, *, R, P, name):
    M, K = x.shape
    tm = _row_tile(M, rows_per_group, _grouped(shift))
    tm = min(tm, 512)
    rows = _pick(tm, (128, 64, 32, 16, 8))
    NW = R + 2 * LANES
    mod_specs = [_mod_spec(m, K, tm, rows_per_group) for m in (shift, scale)]
    row = lambda i: (i, 0)
    fixed = lambda i: (0, 0)
    return pl.pallas_call(
        functools.partial(_kv_kernel, rows=rows, R=R, P=P),
        grid=(M // tm,),
        in_specs=[pl.BlockSpec((tm, K), row), pl.BlockSpec((1, K), fixed), *mod_specs,
                  pl.BlockSpec((K, NW), fixed), pl.BlockSpec((1, R), fixed),
                  pl.BlockSpec((tm, LANES), row), pl.BlockSpec((tm, LANES), row)],
        out_specs=[pl.BlockSpec((tm, R), row), pl.BlockSpec((tm, P), row), pl.BlockSpec((tm, R + LANES), row)],
        out_shape=[jax.ShapeDtypeStruct((M, R), F32), jax.ShapeDtypeStruct((M, P), F32),
                   jax.ShapeDtypeStruct((M, R + LANES), BF)],
        scratch_shapes=[pltpu.VMEM((tm, K), BF)],
        compiler_params=_params(1),
        name=name,
    )(x, g.reshape(1, K), shift[0], scale[0], w_ext, g_kv.reshape(1, R), cos_t, sin_t)


def _q_kernel(x_ref, g_ref, sh_ref, sc_ref, wdq_ref, gq_ref, wuq_ref, wuk_ref, cos_ref, sin_ref,
              o_ref, cq_ref, h_ref, *, rows, NOPE, R, HB, head_major):
    @pl.when(pl.program_id(1) == 0)
    def _():
        _modulate_rows(x_ref, g_ref, sh_ref, sc_ref, h_ref, rows)
        c = _dot(h_ref[...], wdq_ref[...].astype(BF))
        cq_ref[...] = (c * lax.rsqrt(jnp.mean(c * c, axis=-1, keepdims=True) + EPS) * gq_ref[...]).astype(BF)

    W = R + LANES
    for hb in range(HB):
        qf = _dot(cq_ref[...], wuq_ref[hb].astype(BF))
        q_lat = _dot_nt(qf[:, :NOPE].astype(BF), wuk_ref[:, hb * NOPE:(hb + 1) * NOPE].astype(BF))
        rot = qf[:, NOPE:NOPE + LANES] * cos_ref[...] + qf[:, NOPE + LANES:NOPE + 2 * LANES] * sin_ref[...]
        if head_major:
            o_ref[hb, :, :R] = q_lat.astype(o_ref.dtype)
            o_ref[hb, :, R:] = rot.astype(o_ref.dtype)
        else:
            o_ref[:, hb * W:hb * W + R] = q_lat.astype(o_ref.dtype)
            o_ref[:, hb * W + R:(hb + 1) * W] = rot.astype(o_ref.dtype)


def mla_queries(x, g, shift, scale, w_dq, g_q, w_uq_r, w_uk, cos_t, sin_t, rows_per_group, *,
                H, NOPE, P, R, head_major, name):
    M, K = x.shape
    QL = w_dq.shape[1]
    tm = _row_tile(M, rows_per_group, True) if head_major else _row_tile(M, rows_per_group, _grouped(shift))
    tm = min(tm, 512)
    rows = _pick(tm, (128, 64, 32, 16, 8))
    NQ = NOPE + 2 * LANES
    W = R + LANES
    HB = _pick(H, (Q_HEADS_PER_STEP, 2, 1))
    mod_specs = [_mod_spec(m, K, tm, rows_per_group) for m in (shift, scale)]
    row = lambda i, h: (i, 0)
    fixed = lambda i, h: (0, 0)
    if head_major:
        tiles = rows_per_group // tm
        out_spec = pl.BlockSpec((None, HB, tm, W), lambda i, h: (i // tiles, h, i % tiles, 0))
        out_shape = jax.ShapeDtypeStruct((M // rows_per_group, H, rows_per_group, W), BF)
    else:
        out_spec = pl.BlockSpec((tm, HB * W), lambda i, h: (i, h))
        out_shape = jax.ShapeDtypeStruct((M, H * W), BF)
    return pl.pallas_call(
        functools.partial(_q_kernel, rows=rows, NOPE=NOPE, R=R, HB=HB, head_major=head_major),
        grid=(M // tm, H // HB),
        in_specs=[pl.BlockSpec((tm, K), row), pl.BlockSpec((1, K), fixed), *mod_specs,
                  pl.BlockSpec((K, QL), fixed), pl.BlockSpec((1, QL), fixed),
                  pl.BlockSpec((HB, QL, NQ), lambda i, h: (h, 0, 0)),
                  pl.BlockSpec((R, HB * NOPE), lambda i, h: (0, h)),
                  pl.BlockSpec((tm, LANES), row), pl.BlockSpec((tm, LANES), row)],
        out_specs=out_spec,
        out_shape=out_shape,
        scratch_shapes=[pltpu.VMEM((tm, QL), BF), pltpu.VMEM((tm, K), BF)],
        compiler_params=_params(2),
        name=name,
    )(x, g.reshape(1, K), shift[0], scale[0], w_dq, g_q.reshape(1, QL), w_uq_r, w_uk, cos_t, sin_t)


def _attn_prompt_kernel(q_ref, k_ref, o_ref, m_scr, l_scr, a_scr, acc_scr, s_scr, p_scr, *, tq, tk, H, R, scale):
    qi = pl.program_id(1)
    kj = pl.program_id(2)

    @pl.when(kj == 0)
    def _():
        m_scr[...] = jnp.full_like(m_scr, -jnp.inf)
        l_scr[...] = jnp.zeros_like(l_scr)
        acc_scr[...] = jnp.zeros_like(acc_scr)

    rows = H * tq
    hg = max(1, ATTN_GROUP_ROWS // tq)
    group = hg * tq
    n_groups = rows // group
    rc = min(ATTN_SOFTMAX_ROWS, group)
    c2 = scale * LOG2_E

    def scores(g):
        q = q_ref[g * hg:(g + 1) * hg].reshape(group, q_ref.shape[-1])
        s_scr[g * group:(g + 1) * group, :] = _dot_nt(q, k_ref[...])

    def softmax(g, masked):
        for r0 in range(g * group, (g + 1) * group, rc):
            sl = slice(r0, r0 + rc)
            s = s_scr[sl, :]
            if masked:
                qpos = qi * tq + lax.rem(r0 + lax.broadcasted_iota(jnp.int32, (rc, tk), 0), tq)
                kpos = kj * tk + lax.broadcasted_iota(jnp.int32, (rc, tk), 1)
                s = jnp.where(kpos <= qpos, s, -jnp.inf)
            m_old = m_scr[sl, :]
            m_new = jnp.maximum(m_old, jnp.max(s, axis=-1, keepdims=True))
            alpha = jnp.exp2((m_old - m_new) * c2)
            p = jnp.exp2((s - m_new) * c2)
            l_scr[sl, :] = alpha * l_scr[sl, :] + jnp.sum(p, axis=-1, keepdims=True)
            m_scr[sl, :] = m_new
            a_scr[sl, :] = alpha
            p_scr[sl, :] = p.astype(BF)

    def values(g):
        gs = slice(g * group, (g + 1) * group)
        acc_scr[gs, :] = a_scr[gs, :] * acc_scr[gs, :] + _dot(p_scr[gs, :], k_ref[:, :R])

    def block(masked):
        for g in range(n_groups + 2):
            if g < n_groups:
                scores(g)
            if 1 <= g <= n_groups:
                softmax(g - 1, masked)
            if g >= 2:
                values(g - 2)

    crosses_diagonal = kj * tk + tk - 1 > qi * tq

    @pl.when((kj * tk <= qi * tq + tq - 1) & crosses_diagonal)
    def _():
        block(True)

    @pl.when(jnp.logical_not(crosses_diagonal))
    def _():
        block(False)

    @pl.when(kj == pl.num_programs(2) - 1)
    def _():
        o = acc_scr[...] / l_scr[...]
        o_ref[...] = o.reshape(H, tq, R).astype(o_ref.dtype)


def attention_prompt(q4, kcat, *, R, scale, name):
    B, H, L, W = q4.shape
    tq = _pick(L, (128, 64, 32, 16))
    tk = _pick(L, (512, 256, 128, 64, 32, 16))
    nq, nk = L // tq, L // tk

    def k_index(b, i, j):
        return (b, jnp.minimum(j, (i * tq + tq - 1) // tk), 0)

    return pl.pallas_call(
        functools.partial(_attn_prompt_kernel, tq=tq, tk=tk, H=H, R=R, scale=scale),
        grid=(B, nq, nk),
        in_specs=[pl.BlockSpec((None, H, tq, W), lambda b, i, j: (b, 0, i, 0)),
                  pl.BlockSpec((None, tk, W), k_index)],
        out_specs=pl.BlockSpec((None, H, tq, R), lambda b, i, j: (b, 0, i, 0)),
        out_shape=jax.ShapeDtypeStruct((B, H, L, R), BF),
        scratch_shapes=[pltpu.VMEM((H * tq, 1), F32), pltpu.VMEM((H * tq, 1), F32), pltpu.VMEM((H * tq, 1), F32),
                        pltpu.VMEM((H * tq, R), F32), pltpu.VMEM((H * tq, tk), F32), pltpu.VMEM((H * tq, tk), BF)],
        compiler_params=_params(3),
        name=name,
    )(q4, kcat)


def _attn_sample_kernel(pt_ref, q_ref, knew_ref, ckv_hbm, kpe_hbm, o_ref, m_scr, l_scr, acc_scr,
                        ckv_buf, kpe_buf, sems, *, NP, T, H, R, P, scale):
    b = pl.program_id(0)
    j = pl.program_id(1)
    nj = pl.num_programs(1)
    step = b * nj + j
    n_steps = pl.num_programs(0) * nj
    n_slots = ckv_buf.shape[0]
    slot = lax.rem(step, n_slots)
    rows = T * H

    def page_copies(st):
        bb, jj, sl = lax.div(st, nj), lax.rem(st, nj), lax.rem(st, n_slots)
        copies = []
        for i in range(NP):
            pid = pt_ref[bb, jj * NP + i]
            copies.append(pltpu.make_async_copy(ckv_hbm.at[pid], ckv_buf.at[sl, i], sems.at[0, sl]))
            copies.append(pltpu.make_async_copy(kpe_hbm.at[pid], kpe_buf.at[sl, i], sems.at[1, sl]))
        return copies

    @pl.when(step == 0)
    def _():
        for ahead in range(n_slots - 1):
            @pl.when(ahead < n_steps)
            def _():
                for cp in page_copies(step + ahead):
                    cp.start()

    @pl.when(step + n_slots - 1 < n_steps)
    def _():
        for cp in page_copies(step + n_slots - 1):
            cp.start()

    for cp in page_copies(step):
        cp.wait()
    q = q_ref[...]
    q_lat = q[:, :R]
    q_pe = q[:, R:R + P]

    c2 = scale * LOG2_E

    def update(state, s, values):
        m_old, l_old, acc = state
        m_new = jnp.maximum(m_old, jnp.max(s, axis=-1, keepdims=True))
        alpha = jnp.exp2((m_old - m_new) * c2)
        p = jnp.exp2((s - m_new) * c2)
        return m_new, alpha * l_old + jnp.sum(p, axis=-1, keepdims=True), alpha * acc + values(p.astype(BF))

    @pl.when(j == 0)
    def _():
        tp = -(-T // BF16_ROWS) * BF16_ROWS
        knew = jnp.concatenate([knew_ref[...], jnp.zeros((tp - T, knew_ref.shape[1]), BF)], axis=0)
        s = _dot_nt(q, knew)
        qt = lax.broadcasted_iota(jnp.int32, (rows, tp), 0) // H
        kt = lax.broadcasted_iota(jnp.int32, (rows, tp), 1)
        s = jnp.where(kt <= qt, s, -jnp.inf)
        init = (jnp.full((rows, 1), -jnp.inf, F32), jnp.zeros((rows, 1), F32), jnp.zeros((rows, R), F32))
        m_scr[...], l_scr[...], acc_scr[...] = update(init, s, lambda p: _dot(p, knew[:, :R]))

    pg = min(ATTN_PAGE_GROUP, NP)
    n_groups = NP // pg

    def load(g):
        return [ckv_buf[slot, i].astype(BF) for i in range(g * pg, (g + 1) * pg)]

    def scores(g, pages):
        return jnp.concatenate([_dot_nt(q_lat, pages[i]) + _dot(q_pe, kpe_buf[slot, g * pg + i].astype(BF))
                                for i in range(pg)], axis=1)

    def values_of(pages):
        def values(p):
            page = pages[0].shape[0]
            out = _dot(p[:, :page], pages[0])
            for i in range(1, pg):
                out = out + _dot(p[:, i * page:(i + 1) * page], pages[i])
            return out
        return values

    state = (m_scr[...], l_scr[...], acc_scr[...])
    pages = load(0)
    s = scores(0, pages)
    for g in range(n_groups):
        if g + 1 < n_groups:
            pages_next = load(g + 1)
            s_next = scores(g + 1, pages_next)
        state = update(state, s, values_of(pages))
        if g + 1 < n_groups:
            pages, s = pages_next, s_next
    m_scr[...], l_scr[...], acc_scr[...] = state

    @pl.when(j == pl.num_programs(1) - 1)
    def _():
        o_ref[...] = (acc_scr[...] / l_scr[...]).astype(o_ref.dtype)


def attention_sample(q3, knew, cache_ckv, cache_kpe_t, page_table, *, T, H, R, P, scale, name):
    Bd, rows, W = q3.shape
    n_pages = page_table.shape[1]
    page = cache_ckv.shape[1]
    NP = _pick(n_pages, (16, 8, 4, 2, 1))
    any_space = pl.BlockSpec(memory_space=pl.ANY)
    grid_spec = pltpu.PrefetchScalarGridSpec(
        num_scalar_prefetch=1,
        grid=(Bd, n_pages // NP),
        in_specs=[pl.BlockSpec((None, rows, W), lambda b, j, pt: (b, 0, 0)),
                  pl.BlockSpec((None, T, W), lambda b, j, pt: (b, 0, 0)),
                  any_space, any_space],
        out_specs=pl.BlockSpec((None, rows, R), lambda b, j, pt: (b, 0, 0)),
        scratch_shapes=[pltpu.VMEM((rows, 1), F32), pltpu.VMEM((rows, 1), F32), pltpu.VMEM((rows, R), F32),
                        pltpu.VMEM((ATTN_PAGE_SLOTS, NP, page, R), F32),
                        pltpu.VMEM((ATTN_PAGE_SLOTS, NP, P, page), F32),
                        pltpu.SemaphoreType.DMA((2, ATTN_PAGE_SLOTS))],
    )
    return pl.pallas_call(
        functools.partial(_attn_sample_kernel, NP=NP, T=T, H=H, R=R, P=P, scale=scale),
        grid_spec=grid_spec,
        out_shape=jax.ShapeDtypeStruct((Bd, rows, R), BF),
        compiler_params=_params(2),
        name=name,
    )(page_table, q3, knew, cache_ckv, cache_kpe_t)


def _value_up_kernel(a_ref, w_ref, o_ref, *, H, R, V, head_major):
    for h in range(H):
        a = a_ref[h] if head_major else a_ref[:, h * R:(h + 1) * R]
        o_ref[:, h * V:(h + 1) * V] = _dot(a, w_ref[:, h * V:(h + 1) * V].astype(BF)).astype(o_ref.dtype)


def value_up(o_lat, w_uv, *, H, R, V, head_major, name):
    if head_major:
        G, _, L, _ = o_lat.shape
        M = G * L
        tm = _pick(L, (512, 256, 128, 64, 32, 16, 8))
        tiles = L // tm
        a_spec = pl.BlockSpec((None, H, tm, R), lambda i: (i // tiles, 0, i % tiles, 0))
    else:
        M = o_lat.shape[0]
        tm = _pick(M, (512, 256, 128, 64, 32, 16, 8))
        a_spec = pl.BlockSpec((tm, H * R), lambda i: (i, 0))
    return pl.pallas_call(
        functools.partial(_value_up_kernel, H=H, R=R, V=V, head_major=head_major),
        grid=(M // tm,),
        in_specs=[a_spec, pl.BlockSpec((R, H * V), lambda i: (0, 0))],
        out_specs=pl.BlockSpec((tm, H * V), lambda i: (i, 0)),
        out_shape=jax.ShapeDtypeStruct((M, H * V), BF),
        compiler_params=_params(1),
        name=name,
    )(o_lat, w_uv)


def _rope_tables(pos, P, reps):
    half = P // 2
    inv = ROPE_THETA ** (-jnp.arange(half, dtype=F32) / half)
    ang = pos.astype(F32)[:, None] * inv[None, :]
    cos, sin = jnp.cos(ang), jnp.sin(ang)
    zeros = jnp.zeros((pos.shape[0], LANES - P), F32)
    cos_t = jnp.concatenate([cos, cos, zeros], axis=1)
    sin_t = jnp.concatenate([-sin, sin, zeros], axis=1)
    return jnp.tile(cos_t, (reps, 1)), jnp.tile(sin_t, (reps, 1))


def _trunk(x3, mods, pos, conv_in, ssm_in, past, p, tag):
    G, L, D = x3.shape
    M = G * L
    grouped = L % SUBLANES == 0
    x = x3.reshape(M, D)

    def split_mods(m, n):
        arr = m[:, None, :] if grouped else jnp.repeat(m, L, axis=0)
        return [(arr, k) for k in range(n)]

    HV, DK, DV = p['HV'], p['DK'], p['DV']
    HQK = p['HQK']
    KEY, VAL = HQK * DK, HV * DV
    H, NOPE, P, R, V = p['H'], p['NOPE'], p['P'], p['R'], p['V']
    FF = p['w_down'].shape[1]
    cos_t, sin_t = _rope_tables(pos, P, G)
    scale = (NOPE + P) ** -0.5

    sh1, sc1, gt1, sh2, sc2, gt2 = split_mods(mods['l0'], 6)
    w_in_t = p['gdn_w_in_t']
    n_proj = 2 * KEY + 2 * VAL
    proj = mod_matmul(x, p['g_mix'][0], sh1, sc1, w_in_t, 0, n_proj, L, w_is_nk=True, name=f'{tag}_gdn_in')
    ba = mod_matmul(x, p['g_mix'][0], sh1, sc1, p['gdn_w_ba_t'], 0, LANES, L, w_is_nk=True, name=f'{tag}_gdn_ba')
    proj3 = proj.reshape(G, L, n_proj)
    o_g, ssm_new = gdn_scan(
        proj3, ba[:, :HV], ba[:, HV:2 * HV], conv_in, ssm_in, p['gdn_w_conv'][0], p['gdn_a_log'][0],
        p['gdn_dt_bias'][0], p['gdn_g_norm'][0], B=G, L=L, HQK=HQK, HV=HV, DK=DK, DV=DV, name=f'{tag}_gdn_scan')
    conv_new = proj3[:, L - (CONV_W - 1):, :2 * KEY + VAL]
    x = matmul_residual(o_g.reshape(M, VAL), p['gdn_w_out'], 0, x, gt1, L, name=f'{tag}_gdn_out')
    hff = mod_matmul(x, p['g_ffn'][0], sh2, sc2, p['w_gate_up'], 0, FF, L, swiglu=True, out_dtype=BF,
                     name=f'{tag}_ffn0_up')
    x = matmul_residual(hff, p['w_down'], 0, x, gt2, L, name=f'{tag}_ffn0_down')

    shk, sck = split_mods(mods['kv'], 2)
    ckv, kpe, kcat = shared_kv(x, p['kv_g_in'], shk, sck, p['kv_w_ext'], p['kv_g_norm'], cos_t, sin_t, L,
                               R=R, P=P, name=f'{tag}_kv')

    sh1, sc1, gt1, sh2, sc2, gt2 = split_mods(mods['l1'], 6)
    head_major = past is None
    q = mla_queries(x, p['g_mix'][1], sh1, sc1, p['mla_w_dq'][0], p['mla_g_q'][0], p['mla_w_uq_r'], p['kv_w_uk'],
                    cos_t, sin_t, L, H=H, NOPE=NOPE, P=P, R=R, head_major=head_major, name=f'{tag}_q')
    if head_major:
        o_lat = attention_prompt(q, kcat.reshape(G, L, R + LANES), R=R, scale=scale, name=f'{tag}_attn')
    else:
        cache_ckv, cache_kpe, page_table = past
        o_lat = attention_sample(q.reshape(G, L * H, R + LANES), kcat.reshape(G, L, R + LANES), cache_ckv,
                                 cache_kpe, page_table, T=L, H=H, R=R, P=P, scale=scale, name=f'{tag}_attn')
        o_lat = o_lat.reshape(M, H * R)
    o = value_up(o_lat, p['kv_w_uv'], H=H, R=R, V=V, head_major=head_major, name=f'{tag}_uv')
    x = matmul_residual(o, p['mla_w_o'], 0, x, gt1, L, name=f'{tag}_attn_out')
    hff = mod_matmul(x, p['g_ffn'][1], sh2, sc2, p['w_gate_up'], 1, FF, L, swiglu=True, out_dtype=BF,
                     name=f'{tag}_ffn1_up')
    x = matmul_residual(hff, p['w_down'], 1, x, gt2, L, name=f'{tag}_ffn1_down')

    shf, scf = split_mods(mods['final'], 2)
    y = modulate_rows(x, p['final_g'], shf, scf, L, name=f'{tag}_final')
    return (y.reshape(G, L, D), conv_new[None], ssm_new[None], ckv.reshape(G, L, R), kpe.reshape(G, L, P))


def kernel(x_prompt, x_sample, c_prompt, c_sample, cache_ckv, cache_kpe, page_table, state_ssm, state_conv, w_ada, b_ada, g_mix, g_ffn, w_gate_up, w_down, gdn_w_in, gdn_w_conv, gdn_a_log, gdn_dt_bias, gdn_g_norm, gdn_w_out, kv_w_ada, kv_b_ada, kv_g_in, kv_w_down, kv_g_norm, kv_w_uk, kv_w_uv, mla_w_dq, mla_g_q, mla_w_uq, mla_w_o, final_w_ada, final_b_ada, final_g):
    B, S, D = x_prompt.shape
    Bd, T, _ = x_sample.shape
    HV, DK, DV = state_ssm.shape[2:]
    KEY = (state_conv.shape[-1] - HV * DV) // 2
    R = cache_ckv.shape[-1]
    P = cache_kpe.shape[-1]
    QL = mla_w_dq.shape[-1]
    nope_total = kv_w_uk.shape[1]
    H = (mla_w_uq.shape[-1] - nope_total) // P
    NOPE = nope_total // H
    w_in_t = jnp.swapaxes(gdn_w_in, 1, 2)
    n_proj = 2 * KEY + 2 * HV * DV
    w_ba_t = jnp.concatenate([w_in_t[0, n_proj:], jnp.zeros((LANES - 2 * HV, D), F32)], axis=0)
    p = dict(w_ada=w_ada, b_ada=b_ada, g_mix=g_mix, g_ffn=g_ffn, w_gate_up=w_gate_up, w_down=w_down,
             gdn_w_in_t=w_in_t, gdn_w_ba_t=w_ba_t, gdn_w_conv=gdn_w_conv, gdn_a_log=gdn_a_log,
             gdn_dt_bias=gdn_dt_bias, gdn_g_norm=gdn_g_norm, gdn_w_out=gdn_w_out, kv_g_in=kv_g_in,
             kv_w_ext=kv_down_weight(kv_w_down, R), kv_g_norm=kv_g_norm, kv_w_uk=kv_w_uk, kv_w_uv=kv_w_uv,
             mla_w_dq=mla_w_dq, mla_g_q=mla_g_q, mla_w_uq_r=query_up_weight(mla_w_uq[0], H, NOPE),
             mla_w_o=mla_w_o, final_g=final_g,
             HV=HV, DK=DK, DV=DV, HQK=KEY // DK, H=H, NOPE=NOPE, P=P, R=R, V=kv_w_uv.shape[1] // H)

    n_c = B + Bd
    pad = -n_c % SUBLANES
    c_all = jnp.concatenate([c_prompt, c_sample, jnp.zeros((pad, D), F32)], axis=0)
    m_l0 = ada_dense(c_all, w_ada, b_ada, 0, name='ada_l0')
    m_l1 = ada_dense(c_all, w_ada, b_ada, 1, name='ada_l1')
    m_kv = ada_dense(c_all, kv_w_ada, kv_b_ada, 0, name='ada_kv')
    m_f = ada_dense(c_all, final_w_ada, final_b_ada, 0, name='ada_final')

    def mods(lo, hi):
        return dict(l0=m_l0[lo:hi], l1=m_l1[lo:hi], kv=m_kv[lo:hi], final=m_f[lo:hi])

    y_p, conv_p, ssm_p, ckv_p, kpe_p = _trunk(x_prompt, mods(0, B), jnp.arange(S), None, None, None, p, 'p')
    past_len = page_table.shape[1] * cache_ckv.shape[1]
    y_s, conv_s, ssm_s, ckv_s, kpe_s = _trunk(x_sample, mods(B, n_c), past_len + jnp.arange(T), state_conv[0],
                                              state_ssm[0], (cache_ckv, jnp.swapaxes(cache_kpe, 1, 2), page_table),
                                              p, 's')
    return (y_p, y_s, ssm_p, conv_p, ckv_p, kpe_p, ssm_s, conv_s, ckv_s, kpe_s)
```

```python
import functools

import jax
import jax.numpy as jnp
from jax import lax
from jax.experimental import pallas as pl
from jax.experimental.pallas import tpu as pltpu

EPS = 1e-6
ROPE_THETA = 10000.0
CONV_W = 4
GDN_CHUNK = 64
F32 = jnp.float32
BF = jnp.bfloat16

V7X_VMEM_BYTES = 64 * 1024 * 1024
VMEM_LIMIT = V7X_VMEM_BYTES - 8 * 1024 * 1024
VMEM_TILE_BUDGET = (VMEM_LIMIT * 9) // 10
LANES = 128
SUBLANES = 8
BF16_ROWS = 16
TRI_BLOCK = 16
STACK_ROWS = 128
MAX_STACK_HEADS = 8
SEQS_PER_STEP = 4
STACKS_PER_STEP = 16
ATTN_GROUP_ROWS = 256
ATTN_SOFTMAX_ROWS = 64
ATTN_PAGE_GROUP = 4
ATTN_PAGE_SLOTS = 3
Q_HEADS_PER_STEP = 4
LOG2_E = 1.4426950408889634


def _params(n_axes):
    return pltpu.CompilerParams(dimension_semantics=("arbitrary",) * n_axes, vmem_limit_bytes=VMEM_LIMIT)


def _pick(n, cands):
    for c in cands:
        if n % c == 0:
            return c
    return n


def _silu(x):
    return x * jax.nn.sigmoid(x)


def _contract(a, b, ca, cb):
    batch = tuple(range(a.ndim - 2))
    dims = (((a.ndim - 2 + ca,), (b.ndim - 2 + cb,)), (batch, batch))
    return lax.dot_general(a, b, dims, preferred_element_type=F32)


def _dot(a, b):
    return _contract(a, b, 1, 0)


def _dot_nt(a, b):
    return _contract(a, b, 1, 1)


def _dot_tn(a, b):
    return _contract(a, b, 0, 0)


def _split_bf16(a):
    hi = a.astype(BF)
    lo = (a - hi.astype(F32)).astype(BF)
    return hi, lo


def _dot3(a, b):
    ah, al = _split_bf16(a)
    bh, bl = _split_bf16(b)
    return _dot(ah, bh) + _dot(ah, bl) + _dot(al, bh)


def _modulate_rows(x_ref, g_ref, sh_ref, sc_ref, h_ref, rows):
    tm = x_ref.shape[0]
    per_row = sh_ref.shape[0] != 1

    def body(r, carry):
        sl = pl.ds(pl.multiple_of(r * rows, rows), rows)
        x = x_ref[sl, :]
        y = x * lax.rsqrt(jnp.mean(x * x, axis=-1, keepdims=True) + EPS) * g_ref[...]
        sc = sc_ref[sl, :] if per_row else sc_ref[...]
        sh = sh_ref[sl, :] if per_row else sh_ref[...]
        h_ref[sl, :] = (y * (1.0 + sc) + sh).astype(h_ref.dtype)
        return carry

    lax.fori_loop(0, tm // rows, body, 0)


def _grouped(mod):
    return mod[0].ndim == 3


def _mod_spec(mod, K, tm, rows_per_group):
    arr, k = mod
    if arr.ndim == 3:
        tiles_per_group = rows_per_group // tm
        return pl.BlockSpec((None, 1, K), lambda i, *_: (i // tiles_per_group, 0, k))
    return pl.BlockSpec((tm, K), lambda i, *_: (i, k))


def _w_spec(w, layer, K, tn, col_off_blocks=0):
    if w.ndim == 3:
        return pl.BlockSpec((None, K, tn), lambda i, j: (layer, 0, j + col_off_blocks))
    return pl.BlockSpec((K, tn), lambda i, j: (0, j + col_off_blocks))


def _modmm_kernel(x_ref, g_ref, sh_ref, sc_ref, w_ref, o_ref, h_ref, *, rows, w_is_nk):
    @pl.when(pl.program_id(1) == 0)
    def _():
        _modulate_rows(x_ref, g_ref, sh_ref, sc_ref, h_ref, rows)

    dot = _dot_nt if w_is_nk else _dot
    o_ref[...] = dot(h_ref[...], w_ref[...].astype(BF)).astype(o_ref.dtype)


def _modmm_swiglu_kernel(x_ref, g_ref, sh_ref, sc_ref, wg_ref, wu_ref, o_ref, h_ref, *, rows):
    @pl.when(pl.program_id(1) == 0)
    def _():
        _modulate_rows(x_ref, g_ref, sh_ref, sc_ref, h_ref, rows)

    h = h_ref[...]
    gate = _dot(h, wg_ref[...].astype(BF))
    up = _dot(h, wu_ref[...].astype(BF))
    o_ref[...] = (_silu(gate) * up).astype(o_ref.dtype)


def _row_tile(M, rows_per_group, grouped):
    base = rows_per_group if grouped else M
    return _pick(base, (1024, 512, 256, 128, 64, 32, 16, 8))


def _col_tile(N, fixed_bytes, bytes_per_col):
    for tn in (1024, 512, 256):
        if N % tn == 0 and fixed_bytes + bytes_per_col * tn <= VMEM_TILE_BUDGET:
            return tn
    return _pick(N, (128,))


def _weight_col_bytes(K, n_weights=1):
    return n_weights * K * (2 * 4 + 2)


def mod_matmul(x, g, shift, scale, w, layer, n_out, rows_per_group, *, swiglu=False, w_is_nk=False,
               out_dtype=F32, name):
    M, K = x.shape
    tm = _row_tile(M, rows_per_group, _grouped(shift))
    n_w = 2 if swiglu else 1
    out_bytes = jnp.dtype(out_dtype).itemsize
    tn = _col_tile(n_out, tm * K * (2 * 4 + 2),
                   _weight_col_bytes(K, n_w) + tm * (2 * out_bytes + 4 * n_w))
    rows = _pick(tm, (128, 64, 32, 16, 8))
    mod_specs = [_mod_spec(m, K, tm, rows_per_group) for m in (shift, scale)]
    if not w_is_nk:
        w_spec = _w_spec(w, layer, K, tn)
    elif w.ndim == 3:
        w_spec = pl.BlockSpec((None, tn, K), lambda i, j: (layer, j, 0))
    else:
        w_spec = pl.BlockSpec((tn, K), lambda i, j: (j, 0))
    in_specs = [pl.BlockSpec((tm, K), lambda i, j: (i, 0)),
                pl.BlockSpec((1, K), lambda i, j: (0, 0)),
                *mod_specs, w_spec]
    args = [x, g.reshape(1, K), shift[0], scale[0], w]
    if swiglu:
        in_specs.append(_w_spec(w, layer, K, tn, n_out // tn))
        args.append(w)
        body = functools.partial(_modmm_swiglu_kernel, rows=rows)
    else:
        body = functools.partial(_modmm_kernel, rows=rows, w_is_nk=w_is_nk)
    return pl.pallas_call(
        body,
        grid=(M // tm, n_out // tn),
        in_specs=in_specs,
        out_specs=pl.BlockSpec((tm, tn), lambda i, j: (i, j)),
        out_shape=jax.ShapeDtypeStruct((M, n_out), out_dtype),
        scratch_shapes=[pltpu.VMEM((tm, K), BF)],
        compiler_params=_params(2),
        name=name,
    )(*args)


def _mmres_kernel(a_ref, w_ref, res_ref, gate_ref, o_ref):
    y = _dot(a_ref[...], w_ref[...].astype(BF))
    o_ref[...] = res_ref[...] + gate_ref[...] * y


def matmul_residual(a, w, layer, res, gate, rows_per_group, *, name):
    M, K = a.shape
    N = res.shape[1]
    tm = _row_tile(M, rows_per_group, _grouped(gate))
    tn = _col_tile(N, tm * K * 2 * a.dtype.itemsize, _weight_col_bytes(K) + tm * (4 * 4 + 4))
    gate_arr, gate_k = gate
    gate_off = gate_k * (N // tn)
    if _grouped(gate):
        tiles_per_group = rows_per_group // tm
        gate_spec = pl.BlockSpec((None, 1, tn), lambda i, j: (i // tiles_per_group, 0, gate_off + j))
    else:
        gate_spec = pl.BlockSpec((tm, tn), lambda i, j: (i, gate_off + j))
    return pl.pallas_call(
        _mmres_kernel,
        grid=(M // tm, N // tn),
        in_specs=[pl.BlockSpec((tm, K), lambda i, j: (i, 0)),
                  _w_spec(w, layer, K, tn),
                  pl.BlockSpec((tm, tn), lambda i, j: (i, j)),
                  gate_spec],
        out_specs=pl.BlockSpec((tm, tn), lambda i, j: (i, j)),
        out_shape=jax.ShapeDtypeStruct((M, N), F32),
        compiler_params=_params(2),
        name=name,
    )(a, w, res, gate_arr)


def _ada_kernel(c_ref, w_ref, b_ref, o_ref):
    a = _silu(c_ref[...]).astype(BF)
    o_ref[...] = _dot(a, w_ref[...].astype(BF)) + b_ref[...]


def ada_dense(c, w, b, layer, *, name):
    M, K = c.shape
    N = w.shape[-1]
    tn = _col_tile(N, 2 * M * K * 4, _weight_col_bytes(K) + M * 3 * 4)
    if b.ndim == 2:
        b_spec = pl.BlockSpec((None, 1, tn), lambda i, j: (layer, 0, j))
        b = b.reshape(b.shape[0], 1, N)
    else:
        b_spec = pl.BlockSpec((1, tn), lambda i, j: (0, j))
        b = b.reshape(1, N)
    return pl.pallas_call(
        _ada_kernel,
        grid=(1, N // tn),
        in_specs=[pl.BlockSpec((M, K), lambda i, j: (0, 0)), _w_spec(w, layer, K, tn), b_spec],
        out_specs=pl.BlockSpec((M, tn), lambda i, j: (0, j)),
        out_shape=jax.ShapeDtypeStruct((M, N), F32),
        compiler_params=_params(2),
        name=name,
    )(c, w, b)


def _modulate_kernel(x_ref, g_ref, sh_ref, sc_ref, o_ref, *, rows):
    _modulate_rows(x_ref, g_ref, sh_ref, sc_ref, o_ref, rows)


def modulate_rows(x, g, shift, scale, rows_per_group, *, name):
    M, K = x.shape
    tm = _row_tile(M, rows_per_group, _grouped(shift))
    rows = _pick(tm, (128, 64, 32, 16, 8))
    mod_specs = [_mod_spec(m, K, tm, rows_per_group) for m in (shift, scale)]
    return pl.pallas_call(
        functools.partial(_modulate_kernel, rows=rows),
        grid=(M // tm,),
        in_specs=[pl.BlockSpec((tm, K), lambda i: (i, 0)), pl.BlockSpec((1, K), lambda i: (0, 0)),
                  *mod_specs],
        out_specs=pl.BlockSpec((tm, K), lambda i: (i, 0)),
        out_shape=jax.ShapeDtypeStruct((M, K), F32),
        compiler_params=_params(1),
        name=name,
    )(x, g.reshape(1, K), shift[0], scale[0])


def _dot1(a, b):
    return _dot(a.astype(BF), b.astype(BF))


def _tri_inverse(n_low, ii, jj, cp, nil):
    rn = n_low.shape[-1]
    eye = (ii == jj).astype(F32)
    base = min(TRI_BLOCK, cp)
    shift = base.bit_length() - 1
    nd = jnp.where((ii >> shift) == (jj >> shift), n_low, 0.0)
    p = eye - nd
    if nil > 2:
        npow = _dot1(nd, nd)
        pw = 2
        while 2 * pw - 1 < nil - 1:
            both = _dot1(jnp.concatenate([p, npow], axis=-2), npow)
            p = p + both[:, :rn]
            npow = both[:, rn:]
            pw *= 2
        p = p + _dot1(p, npow)
    size = base
    while size < cp:
        s = size.bit_length() - 1
        off = ((ii >> (s + 1)) == (jj >> (s + 1))) & (((ii >> s) & 1) == 1) & (((jj >> s) & 1) == 0)
        x = _dot1(jnp.where(off, n_low, 0.0), p)
        p = p - _dot1(p, x)
        size *= 2
    resid = (eye - p) - _dot3(n_low, p)
    return p + _dot1(p, resid)


def _delta_chunk(q_st, k_st, v_st, beta_row, g_row, load_state, store_state, *, G, Cp, nil):
    S, rn, dv = v_st.shape
    ii = lax.broadcasted_iota(jnp.int32, (1, rn, rn), 1)
    jj = lax.broadcasted_iota(jnp.int32, (1, rn, rn), 2)
    sh = Cp.bit_length() - 1
    same = (ii >> sh) == (jj >> sh)
    eye = ii == jj
    causal = same & (jj <= ii)
    g_mat = jnp.broadcast_to(g_row, (S, rn, rn))
    g_col = jnp.sum(jnp.where(eye, g_mat, 0.0), axis=2, keepdims=True)
    gc_col = jnp.sum(jnp.where(causal, g_mat, 0.0), axis=2, keepdims=True)
    gl_col = jnp.sum(jnp.where(same, g_mat, 0.0), axis=2, keepdims=True)
    gc_row = jnp.sum(jnp.where(same & (ii <= jj), jnp.broadcast_to(g_col, (S, rn, rn)), 0.0), axis=1, keepdims=True)
    beta_col = jnp.sum(jnp.where(eye, jnp.broadcast_to(beta_row, (S, rn, rn)), 0.0), axis=2, keepdims=True)
    decay = jnp.exp(jnp.where(causal, gc_col - gc_row, -jnp.inf))
    eg = jnp.exp(gc_col)
    kb = k_st * beta_col
    both = _dot_nt(jnp.concatenate([kb, q_st], axis=1).astype(BF), k_st.astype(BF))
    n_low = jnp.where(same & (jj < ii), both[:, :rn] * decay, 0.0)
    t_inv = _tri_inverse(n_low, ii, jj, Cp, nil).astype(BF)
    uw = _dot(t_inv, jnp.concatenate([v_st * beta_col, kb * eg], axis=2).astype(BF))
    u, w = uw[:, :, :dv], uw[:, :, dv:]
    qe = q_st * eg
    k_dec = k_st * jnp.exp(gl_col - gc_col)
    pad = jnp.zeros((S, -Cp % BF16_ROWS, k_dec.shape[2]), F32)

    def bf16_rows(x):
        return (jnp.concatenate([x, pad], axis=1) if pad.shape[1] else x).astype(BF)

    v_new, q_s = [], []
    for h in range(G):
        r = slice(h * Cp, (h + 1) * Cp)
        s_old = jnp.stack([load_state(s * G + h) for s in range(S)], axis=0)
        ws_qs = _dot(jnp.concatenate([w[:, r], qe[:, r]], axis=1).astype(BF), s_old.astype(BF))
        v_new_h = u[:, r] - ws_qs[:, :Cp]
        q_s.append(ws_qs[:, Cp:])
        v_new.append(v_new_h)
        s_new = (s_old * jnp.exp(gl_col[:, h * Cp:h * Cp + 1])
                 + _dot_tn(bf16_rows(k_dec[:, r]), bf16_rows(v_new_h)))
        for s in range(S):
            store_state(s * G + h, s_new[s])
    v_new = jnp.concatenate(v_new, axis=1).astype(BF)
    return jnp.concatenate(q_s, axis=1) + _dot((both[:, rn:] * decay).astype(BF), v_new)


def _gdn_kernel(*refs, C, Cp, G, NS, BB, rep, DK, DV, has_state, single_chunk):
    q_ref, k_ref, v_ref, z_ref, b_ref, a_ref, alog_ref, dtb_ref, wq_ref, wk_ref, wv_ref, gn_ref = refs[:12]
    pos = 12
    if has_state:
        cq_ref, ck_ref, cv_ref, s0_ref = refs[pos:pos + 4]
        pos += 4
    o_ref, so_ref = refs[pos:pos + 2]
    s_scr, cbq, cbk, cbv = refs[pos + 2:]

    c = pl.program_id(2)
    last = pl.num_programs(2) - 1
    tail_lo = SUBLANES - (CONV_W - 1)
    direct_state = has_state and single_chunk

    @pl.when(c == 0)
    def _():
        for cb in (cbq, cbk, cbv):
            cb[...] = jnp.zeros_like(cb)
        if has_state:
            cbq[:, tail_lo:SUBLANES, :] = cq_ref[...]
            cbk[:, tail_lo:SUBLANES, :] = ck_ref[...]
            cbv[:, tail_lo:SUBLANES, :] = cv_ref[...]
            if not direct_state:
                for bb in range(BB):
                    s_scr[bb * G:(bb + 1) * G] = s0_ref[bb]
        else:
            s_scr[...] = jnp.zeros_like(s_scr)

    def conv(cb, x_ref, w_ref, bb):
        cb[bb, SUBLANES:SUBLANES + C, :] = x_ref[bb]
        acc = cb[bb, tail_lo:tail_lo + Cp, :] * w_ref[0:1, :]
        for j in range(1, CONV_W):
            acc = acc + cb[bb, tail_lo + j:tail_lo + j + Cp, :] * w_ref[j:j + 1, :]
        cb[bb, tail_lo:SUBLANES, :] = cb[bb, C + tail_lo:C + SUBLANES, :]
        return _silu(acc)

    padded = Cp != C
    if padded:
        row_ok = lax.broadcasted_iota(jnp.int32, (Cp, 1), 0) < C
        lane_ok = (lax.broadcasted_iota(jnp.int32, (1, G * Cp), 1) & (Cp - 1)) < C
    GS = G // NS
    rn = GS * Cp
    q_parts, k_parts, v_parts, beta_rows, g_rows = [], [], [], [], []
    for bb in range(BB):
        qc = conv(cbq, q_ref, wq_ref, bb)
        kc = conv(cbk, k_ref, wk_ref, bb)
        vc = conv(cbv, v_ref, wv_ref, bb)
        for i in range(G // rep):
            qh = qc[:, i * DK:(i + 1) * DK]
            kh = kc[:, i * DK:(i + 1) * DK]
            qh = qh * lax.rsqrt(jnp.sum(qh * qh, axis=-1, keepdims=True) + EPS) * (DK ** -0.5)
            kh = kh * lax.rsqrt(jnp.sum(kh * kh, axis=-1, keepdims=True) + EPS)
            if padded:
                kh = jnp.where(row_ok, kh, 0.0)
            for e in range(rep):
                hl = i * rep + e
                vh = vc[:, hl * DV:(hl + 1) * DV]
                if padded:
                    vh = jnp.where(row_ok, vh, 0.0)
                q_parts.append(qh)
                k_parts.append(kh)
                v_parts.append(vh)
        beta_row = jax.nn.sigmoid(b_ref[bb, c])
        sp_in = a_ref[bb, c] + dtb_ref[...]
        softplus = jnp.maximum(sp_in, 0.0) + jnp.log1p(jnp.exp(-jnp.abs(sp_in)))
        g_row = -jnp.exp(alog_ref[...]) * softplus
        if padded:
            beta_row = jnp.where(lane_ok, beta_row, 0.0)
            g_row = jnp.where(lane_ok, g_row, 0.0)
        beta_rows += [beta_row[:, s * rn:(s + 1) * rn] for s in range(NS)]
        g_rows += [g_row[:, s * rn:(s + 1) * rn] for s in range(NS)]

    def load_state(i):
        return s0_ref[i // G, i % G] if direct_state else s_scr[i]

    def store_state(i, value):
        if direct_state:
            so_ref[i // G, i % G] = value
        else:
            s_scr[i] = value

    def stacks(parts):
        return jnp.stack([jnp.concatenate(parts[s * GS:(s + 1) * GS], axis=0) for s in range(BB * NS)], axis=0)

    o = _delta_chunk(stacks(q_parts), stacks(k_parts), stacks(v_parts), jnp.stack(beta_rows, axis=0),
                     jnp.stack(g_rows, axis=0), load_state, store_state, G=GS, Cp=Cp, nil=min(C, TRI_BLOCK))
    on = o * lax.rsqrt(jnp.mean(o * o, axis=-1, keepdims=True) + EPS) * gn_ref[...]
    for bb in range(BB):
        for s in range(NS):
            for h in range(GS):
                hl = s * GS + h
                zg = z_ref[bb, :, hl * DV:(hl + 1) * DV]
                o_ref[bb, :, hl * DV:(hl + 1) * DV] = (
                    on[bb * NS + s, h * Cp:h * Cp + C] * _silu(zg)).astype(o_ref.dtype)

    if not direct_state:
        @pl.when(c == last)
        def _():
            for bb in range(BB):
                so_ref[bb] = s_scr[bb * G:(bb + 1) * G]


def gdn_scan(proj, b_logit, a_logit, conv_prev, s0, w_conv, a_log, dt_bias, g_norm, *, B, L, HQK, HV, DK, DV, name):
    rep = HV // HQK
    KEY, VAL = HQK * DK, HV * DV
    C = _pick(L, (GDN_CHUNK,))
    n = L // C
    Cp = -(-C // SUBLANES) * SUBLANES
    assert Cp & (Cp - 1) == 0 and L >= CONV_W - 1
    GS = min(HV, max(rep, min(STACK_ROWS // Cp, MAX_STACK_HEADS)))
    NS = _pick(HV // GS, (STACKS_PER_STEP, 4, 2, 1))
    G = NS * GS
    assert HV % G == 0 and GS % rep == 0
    has_state = s0 is not None
    BB = _pick(B, (SEQS_PER_STEP, 1)) if n == 1 else 1
    qw, vw = (G // rep) * DK, G * DV
    k_off, v_off, z_off = KEY // qw, 2 * KEY // vw, (2 * KEY + VAL) // vw

    def stack_rows(t):
        t = t.reshape(B, n, C, HV // G, G)
        t = jnp.pad(t, ((0, 0), (0, 0), (0, Cp - C), (0, 0), (0, 0)))
        return jnp.transpose(t, (0, 3, 1, 4, 2)).reshape(B, HV // G, n, 1, G * Cp)

    def stack_heads(v):
        return jnp.repeat(v.reshape(HV // G, 1, G), Cp, axis=-1)

    gate_spec = pl.BlockSpec((BB, None, n, 1, G * Cp), lambda b, h, c: (b, h, 0, 0, 0))
    head_spec = pl.BlockSpec((None, 1, G * Cp), lambda b, h, c: (h, 0, 0))
    in_specs = [
        pl.BlockSpec((BB, C, qw), lambda b, h, c: (b, c, h)),
        pl.BlockSpec((BB, C, qw), lambda b, h, c: (b, c, k_off + h)),
        pl.BlockSpec((BB, C, vw), lambda b, h, c: (b, c, v_off + h)),
        pl.BlockSpec((BB, C, vw), lambda b, h, c: (b, c, z_off + h)),
        gate_spec, gate_spec, head_spec, head_spec,
        pl.BlockSpec((CONV_W, qw), lambda b, h, c: (0, h)),
        pl.BlockSpec((CONV_W, qw), lambda b, h, c: (0, k_off + h)),
        pl.BlockSpec((CONV_W, vw), lambda b, h, c: (0, v_off + h)),
        pl.BlockSpec((1, DV), lambda b, h, c: (0, 0)),
    ]
    args = [proj, proj, proj, proj, stack_rows(b_logit), stack_rows(a_logit), stack_heads(a_log),
            stack_heads(dt_bias), w_conv, w_conv, w_conv, g_norm.reshape(1, DV)]
    if has_state:
        in_specs += [
            pl.BlockSpec((BB, CONV_W - 1, qw), lambda b, h, c: (b, 0, h)),
            pl.BlockSpec((BB, CONV_W - 1, qw), lambda b, h, c: (b, 0, k_off + h)),
            pl.BlockSpec((BB, CONV_W - 1, vw), lambda b, h, c: (b, 0, v_off + h)),
            pl.BlockSpec((BB, G, DK, DV), lambda b, h, c: (b, h, 0, 0)),
        ]
        args += [conv_prev, conv_prev, conv_prev, s0]
    body = functools.partial(_gdn_kernel, C=C, Cp=Cp, G=G, NS=NS, BB=BB, rep=rep, DK=DK, DV=DV,
                             has_state=has_state, single_chunk=n == 1)
    return pl.pallas_call(
        body,
        grid=(B // BB, HV // G, n),
        in_specs=in_specs,
        out_specs=[pl.BlockSpec((BB, C, vw), lambda b, h, c: (b, c, h)),
                   pl.BlockSpec((BB, G, DK, DV), lambda b, h, c: (b, h, 0, 0))],
        out_shape=[jax.ShapeDtypeStruct((B, L, VAL), BF), jax.ShapeDtypeStruct((B, HV, DK, DV), F32)],
        scratch_shapes=[pltpu.VMEM((BB * G, DK, DV), F32),
                        pltpu.VMEM((BB, SUBLANES + Cp, qw), F32),
                        pltpu.VMEM((BB, SUBLANES + Cp, qw), F32),
                        pltpu.VMEM((BB, SUBLANES + Cp, vw), F32)],
        compiler_params=_params(3),
        name=name,
    )(*args)


def _kv_kernel(x_ref, g_ref, sh_ref, sc_ref, w_ref, gkv_ref, cos_ref, sin_ref,
               ckv_ref, kpe_ref, kcat_ref, h_ref, *, rows, R, P):
    _modulate_rows(x_ref, g_ref, sh_ref, sc_ref, h_ref, rows)
    y = _dot(h_ref[...], w_ref[...].astype(BF))
    c = y[:, :R]
    ckv = c * lax.rsqrt(jnp.mean(c * c, axis=-1, keepdims=True) + EPS) * gkv_ref[...]
    rot = y[:, R:R + LANES] * cos_ref[...] + y[:, R + LANES:R + 2 * LANES] * sin_ref[...]
    ckv_ref[...] = ckv
    kpe_ref[...] = rot[:, :P]
    kcat_ref[:, :R] = ckv.astype(BF)
    kcat_ref[:, R:] = rot.astype(BF)


def _rope_weight_cols(w_pe):
    P = w_pe.shape[-1]
    half = P // 2
    zeros = jnp.zeros(w_pe.shape[:-1] + (LANES - P,), w_pe.dtype)
    swapped = jnp.concatenate([w_pe[..., half:], w_pe[..., :half]], axis=-1)
    return jnp.concatenate([w_pe, zeros, swapped, zeros], axis=-1)


def kv_down_weight(w_down, R):
    return jnp.concatenate([w_down[:, :R], _rope_weight_cols(w_down[:, R:])], axis=1)


def query_up_weight(w_uq, H, NOPE):
    QL = w_uq.shape[0]
    w_heads = w_uq.reshape(QL, H, -1)
    per_head = jnp.concatenate([w_heads[..., :NOPE], _rope_weight_cols(w_heads[..., NOPE:])], axis=-1)
    return jnp.transpose(per_head, (1, 0, 2))


def shared_kv(x, g, shift, scale, w_ext, g_kv, cos_t, sin_t, rows_per_group, *, R, P, name):
    M, K = x.shape
    tm = _row_tile(M, rows_per_group, _grouped(shift))
    tm = min(tm, 512)
    rows = _pick(tm, (128, 64, 32, 16, 8))
    NW = R + 2 * LANES
    mod_specs = [_mod_spec(m, K, tm, rows_per_group) for m in (shift, scale)]
    row = lambda i: (i, 0)
    fixed = lambda i: (0, 0)
    return pl.pallas_call(
        functools.partial(_kv_kernel, rows=rows, R=R, P=P),
        grid=(M // tm,),
        in_specs=[pl.BlockSpec((tm, K), row), pl.BlockSpec((1, K), fixed), *mod_specs,
                  pl.BlockSpec((K, NW), fixed), pl.BlockSpec((1, R), fixed),
                  pl.BlockSpec((tm, LANES), row), pl.BlockSpec((tm, LANES), row)],
        out_specs=[pl.BlockSpec((tm, R), row), pl.BlockSpec((tm, P), row), pl.BlockSpec((tm, R + LANES), row)],
        out_shape=[jax.ShapeDtypeStruct((M, R), F32), jax.ShapeDtypeStruct((M, P), F32),
                   jax.ShapeDtypeStruct((M, R + LANES), BF)],
        scratch_shapes=[pltpu.VMEM((tm, K), BF)],
        compiler_params=_params(1),
        name=name,
    )(x, g.reshape(1, K), shift[0], scale[0], w_ext, g_kv.reshape(1, R), cos_t, sin_t)


def _q_kernel(x_ref, g_ref, sh_ref, sc_ref, wdq_ref, gq_ref, wuq_ref, wuk_ref, cos_ref, sin_ref,
              o_ref, cq_ref, h_ref, *, rows, NOPE, R, HB, head_major):
    @pl.when(pl.program_id(1) == 0)
    def _():
        _modulate_rows(x_ref, g_ref, sh_ref, sc_ref, h_ref, rows)
        c = _dot(h_ref[...], wdq_ref[...].astype(BF))
        cq_ref[...] = (c * lax.rsqrt(jnp.mean(c * c, axis=-1, keepdims=True) + EPS) * gq_ref[...]).astype(BF)

    W = R + LANES
    for hb in range(HB):
        qf = _dot(cq_ref[...], wuq_ref[hb].astype(BF))
        q_lat = _dot_nt(qf[:, :NOPE].astype(BF), wuk_ref[:, hb * NOPE:(hb + 1) * NOPE].astype(BF))
        rot = qf[:, NOPE:NOPE + LANES] * cos_ref[...] + qf[:, NOPE + LANES:NOPE + 2 * LANES] * sin_ref[...]
        if head_major:
            o_ref[hb, :, :R] = q_lat.astype(o_ref.dtype)
            o_ref[hb, :, R:] = rot.astype(o_ref.dtype)
        else:
            o_ref[:, hb * W:hb * W + R] = q_lat.astype(o_ref.dtype)
            o_ref[:, hb * W + R:(hb + 1) * W] = rot.astype(o_ref.dtype)


def mla_queries(x, g, shift, scale, w_dq, g_q, w_uq_r, w_uk, cos_t, sin_t, rows_per_group, *,
                H, NOPE, P, R, head_major, name):
    M, K = x.shape
    QL = w_dq.shape[1]
    tm = _row_tile(M, rows_per_group, True) if head_major else _row_tile(M, rows_per_group, _grouped(shift))
    tm = min(tm, 512)
    rows = _pick(tm, (128, 64, 32, 16, 8))
    NQ = NOPE + 2 * LANES
    W = R + LANES
    HB = _pick(H, (Q_HEADS_PER_STEP, 2, 1))
    mod_specs = [_mod_spec(m, K, tm, rows_per_group) for m in (shift, scale)]
    row = lambda i, h: (i, 0)
    fixed = lambda i, h: (0, 0)
    if head_major:
        tiles = rows_per_group // tm
        out_spec = pl.BlockSpec((None, HB, tm, W), lambda i, h: (i // tiles, h, i % tiles, 0))
        out_shape = jax.ShapeDtypeStruct((M // rows_per_group, H, rows_per_group, W), BF)
    else:
        out_spec = pl.BlockSpec((tm, HB * W), lambda i, h: (i, h))
        out_shape = jax.ShapeDtypeStruct((M, H * W), BF)
    return pl.pallas_call(
        functools.partial(_q_kernel, rows=rows, NOPE=NOPE, R=R, HB=HB, head_major=head_major),
        grid=(M // tm, H // HB),
        in_specs=[pl.BlockSpec((tm, K), row), pl.BlockSpec((1, K), fixed), *mod_specs,
                  pl.BlockSpec((K, QL), fixed), pl.BlockSpec((1, QL), fixed),
                  pl.BlockSpec((HB, QL, NQ), lambda i, h: (h, 0, 0)),
                  pl.BlockSpec((R, HB * NOPE), lambda i, h: (0, h)),
                  pl.BlockSpec((tm, LANES), row), pl.BlockSpec((tm, LANES), row)],
        out_specs=out_spec,
        out_shape=out_shape,
        scratch_shapes=[pltpu.VMEM((tm, QL), BF), pltpu.VMEM((tm, K), BF)],
        compiler_params=_params(2),
        name=name,
    )(x, g.reshape(1, K), shift[0], scale[0], w_dq, g_q.reshape(1, QL), w_uq_r, w_uk, cos_t, sin_t)


def _attn_prompt_kernel(q_ref, k_ref, o_ref, m_scr, l_scr, a_scr, acc_scr, s_scr, p_scr, *, tq, tk, H, R, scale):
    qi = pl.program_id(1)
    kj = pl.program_id(2)

    @pl.when(kj == 0)
    def _():
        m_scr[...] = jnp.full_like(m_scr, -jnp.inf)
        l_scr[...] = jnp.zeros_like(l_scr)
        acc_scr[...] = jnp.zeros_like(acc_scr)

    rows = H * tq
    hg = max(1, ATTN_GROUP_ROWS // tq)
    group = hg * tq
    n_groups = rows // group
    rc = min(ATTN_SOFTMAX_ROWS, group)
    c2 = scale * LOG2_E

    def scores(g):
        q = q_ref[g * hg:(g + 1) * hg].reshape(group, q_ref.shape[-1])
        s_scr[g * group:(g + 1) * group, :] = _dot_nt(q, k_ref[...])

    def softmax(g, masked):
        for r0 in range(g * group, (g + 1) * group, rc):
            sl = slice(r0, r0 + rc)
            s = s_scr[sl, :]
            if masked:
                qpos = qi * tq + lax.rem(r0 + lax.broadcasted_iota(jnp.int32, (rc, tk), 0), tq)
                kpos = kj * tk + lax.broadcasted_iota(jnp.int32, (rc, tk), 1)
                s = jnp.where(kpos <= qpos, s, -jnp.inf)
            m_old = m_scr[sl, :]
            m_new = jnp.maximum(m_old, jnp.max(s, axis=-1, keepdims=True))
            alpha = jnp.exp2((m_old - m_new) * c2)
            p = jnp.exp2((s - m_new) * c2)
            l_scr[sl, :] = alpha * l_scr[sl, :] + jnp.sum(p, axis=-1, keepdims=True)
            m_scr[sl, :] = m_new
            a_scr[sl, :] = alpha
            p_scr[sl, :] = p.astype(BF)

    def values(g):
        gs = slice(g * group, (g + 1) * group)
        acc_scr[gs, :] = a_scr[gs, :] * acc_scr[gs, :] + _dot(p_scr[gs, :], k_ref[:, :R])

    def block(masked):
        for g in range(n_groups + 2):
            if g < n_groups:
                scores(g)
            if 1 <= g <= n_groups:
                softmax(g - 1, masked)
            if g >= 2:
                values(g - 2)

    crosses_diagonal = kj * tk + tk - 1 > qi * tq

    @pl.when((kj * tk <= qi * tq + tq - 1) & crosses_diagonal)
    def _():
        block(True)

    @pl.when(jnp.logical_not(crosses_diagonal))
    def _():
        block(False)

    @pl.when(kj == pl.num_programs(2) - 1)
    def _():
        o = acc_scr[...] / l_scr[...]
        o_ref[...] = o.reshape(H, tq, R).astype(o_ref.dtype)


def attention_prompt(q4, kcat, *, R, scale, name):
    B, H, L, W = q4.shape
    tq = _pick(L, (128, 64, 32, 16))
    tk = _pick(L, (512, 256, 128, 64, 32, 16))
    nq, nk = L // tq, L // tk

    def k_index(b, i, j):
        return (b, jnp.minimum(j, (i * tq + tq - 1) // tk), 0)

    return pl.pallas_call(
        functools.partial(_attn_prompt_kernel, tq=tq, tk=tk, H=H, R=R, scale=scale),
        grid=(B, nq, nk),
        in_specs=[pl.BlockSpec((None, H, tq, W), lambda b, i, j: (b, 0, i, 0)),
                  pl.BlockSpec((None, tk, W), k_index)],
        out_specs=pl.BlockSpec((None, H, tq, R), lambda b, i, j: (b, 0, i, 0)),
        out_shape=jax.ShapeDtypeStruct((B, H, L, R), BF),
        scratch_shapes=[pltpu.VMEM((H * tq, 1), F32), pltpu.VMEM((H * tq, 1), F32), pltpu.VMEM((H * tq, 1), F32),
                        pltpu.VMEM((H * tq, R), F32), pltpu.VMEM((H * tq, tk), F32), pltpu.VMEM((H * tq, tk), BF)],
        compiler_params=_params(3),
        name=name,
    )(q4, kcat)


def _attn_sample_kernel(pt_ref, q_ref, knew_ref, ckv_hbm, kpe_hbm, o_ref, m_scr, l_scr, acc_scr,
                        ckv_buf, kpe_buf, sems, *, NP, T, H, R, P, scale):
    b = pl.program_id(0)
    j = pl.program_id(1)
    nj = pl.num_programs(1)
    step = b * nj + j
    n_steps = pl.num_programs(0) * nj
    n_slots = ckv_buf.shape[0]
    slot = lax.rem(step, n_slots)
    rows = T * H

    def page_copies(st):
        bb, jj, sl = lax.div(st, nj), lax.rem(st, nj), lax.rem(st, n_slots)
        copies = []
        for i in range(NP):
            pid = pt_ref[bb, jj * NP + i]
            copies.append(pltpu.make_async_copy(ckv_hbm.at[pid], ckv_buf.at[sl, i], sems.at[0, sl]))
            copies.append(pltpu.make_async_copy(kpe_hbm.at[pid], kpe_buf.at[sl, i], sems.at[1, sl]))
        return copies

    @pl.when(step == 0)
    def _():
        for ahead in range(n_slots - 1):
            @pl.when(ahead < n_steps)
            def _():
                for cp in page_copies(step + ahead):
                    cp.start()

    @pl.when(step + n_slots - 1 < n_steps)
    def _():
        for cp in page_copies(step + n_slots - 1):
            cp.start()

    for cp in page_copies(step):
        cp.wait()
    q = q_ref[...]
    q_lat = q[:, :R]
    q_pe = q[:, R:R + P]

    c2 = scale * LOG2_E

    def update(state, s, values):
        m_old, l_old, acc = state
        m_new = jnp.maximum(m_old, jnp.max(s, axis=-1, keepdims=True))
        alpha = jnp.exp2((m_old - m_new) * c2)
        p = jnp.exp2((s - m_new) * c2)
        return m_new, alpha * l_old + jnp.sum(p, axis=-1, keepdims=True), alpha * acc + values(p.astype(BF))

    @pl.when(j == 0)
    def _():
        tp = -(-T // BF16_ROWS) * BF16_ROWS
        knew = jnp.concatenate([knew_ref[...], jnp.zeros((tp - T, knew_ref.shape[1]), BF)], axis=0)
        s = _dot_nt(q, knew)
        qt = lax.broadcasted_iota(jnp.int32, (rows, tp), 0) // H
        kt = lax.broadcasted_iota(jnp.int32, (rows, tp), 1)
        s = jnp.where(kt <= qt, s, -jnp.inf)
        init = (jnp.full((rows, 1), -jnp.inf, F32), jnp.zeros((rows, 1), F32), jnp.zeros((rows, R), F32))
        m_scr[...], l_scr[...], acc_scr[...] = update(init, s, lambda p: _dot(p, knew[:, :R]))

    pg = min(ATTN_PAGE_GROUP, NP)
    n_groups = NP // pg

    def load(g):
        return [ckv_buf[slot, i].astype(BF) for i in range(g * pg, (g + 1) * pg)]

    def scores(g, pages):
        return jnp.concatenate([_dot_nt(q_lat, pages[i]) + _dot(q_pe, kpe_buf[slot, g * pg + i].astype(BF))
                                for i in range(pg)], axis=1)

    def values_of(pages):
        def values(p):
            page = pages[0].shape[0]
            out = _dot(p[:, :page], pages[0])
            for i in range(1, pg):
                out = out + _dot(p[:, i * page:(i + 1) * page], pages[i])
            return out
        return values

    state = (m_scr[...], l_scr[...], acc_scr[...])
    pages = load(0)
    s = scores(0, pages)
    for g in range(n_groups):
        if g + 1 < n_groups:
            pages_next = load(g + 1)
            s_next = scores(g + 1, pages_next)
        state = update(state, s, values_of(pages))
        if g + 1 < n_groups:
            pages, s = pages_next, s_next
    m_scr[...], l_scr[...], acc_scr[...] = state

    @pl.when(j == pl.num_programs(1) - 1)
    def _():
        o_ref[...] = (acc_scr[...] / l_scr[...]).astype(o_ref.dtype)


def attention_sample(q3, knew, cache_ckv, cache_kpe_t, page_table, *, T, H, R, P, scale, name):
    Bd, rows, W = q3.shape
    n_pages = page_table.shape[1]
    page = cache_ckv.shape[1]
    NP = _pick(n_pages, (16, 8, 4, 2, 1))
    any_space = pl.BlockSpec(memory_space=pl.ANY)
    grid_spec = pltpu.PrefetchScalarGridSpec(
        num_scalar_prefetch=1,
        grid=(Bd, n_pages // NP),
        in_specs=[pl.BlockSpec((None, rows, W), lambda b, j, pt: (b, 0, 0)),
                  pl.BlockSpec((None, T, W), lambda b, j, pt: (b, 0, 0)),
                  any_space, any_space],
        out_specs=pl.BlockSpec((None, rows, R), lambda b, j, pt: (b, 0, 0)),
        scratch_shapes=[pltpu.VMEM((rows, 1), F32), pltpu.VMEM((rows, 1), F32), pltpu.VMEM((rows, R), F32),
                        pltpu.VMEM((ATTN_PAGE_SLOTS, NP, page, R), F32),
                        pltpu.VMEM((ATTN_PAGE_SLOTS, NP, P, page), F32),
                        pltpu.SemaphoreType.DMA((2, ATTN_PAGE_SLOTS))],
    )
    return pl.pallas_call(
        functools.partial(_attn_sample_kernel, NP=NP, T=T, H=H, R=R, P=P, scale=scale),
        grid_spec=grid_spec,
        out_shape=jax.ShapeDtypeStruct((Bd, rows, R), BF),
        compiler_params=_params(2),
        name=name,
    )(page_table, q3, knew, cache_ckv, cache_kpe_t)


def _value_up_kernel(a_ref, w_ref, o_ref, *, H, R, V, head_major):
    for h in range(H):
        a = a_ref[h] if head_major else a_ref[:, h * R:(h + 1) * R]
        o_ref[:, h * V:(h + 1) * V] = _dot(a, w_ref[:, h * V:(h + 1) * V].astype(BF)).astype(o_ref.dtype)


def value_up(o_lat, w_uv, *, H, R, V, head_major, name):
    if head_major:
        G, _, L, _ = o_lat.shape
        M = G * L
        tm = _pick(L, (512, 256, 128, 64, 32, 16, 8))
        tiles = L // tm
        a_spec = pl.BlockSpec((None, H, tm, R), lambda i: (i // tiles, 0, i % tiles, 0))
    else:
        M = o_lat.shape[0]
        tm = _pick(M, (512, 256, 128, 64, 32, 16, 8))
        a_spec = pl.BlockSpec((tm, H * R), lambda i: (i, 0))
    return pl.pallas_call(
        functools.partial(_value_up_kernel, H=H, R=R, V=V, head_major=head_major),
        grid=(M // tm,),
        in_specs=[a_spec, pl.BlockSpec((R, H * V), lambda i: (0, 0))],
        out_specs=pl.BlockSpec((tm, H * V), lambda i: (i, 0)),
        out_shape=jax.ShapeDtypeStruct((M, H * V), BF),
        compiler_params=_params(1),
        name=name,
    )(o_lat, w_uv)


def _rope_tables(pos, P, reps):
    half = P // 2
    inv = ROPE_THETA ** (-jnp.arange(half, dtype=F32) / half)
    ang = pos.astype(F32)[:, None] * inv[None, :]
    cos, sin = jnp.cos(ang), jnp.sin(ang)
    zeros = jnp.zeros((pos.shape[0], LANES - P), F32)
    cos_t = jnp.concatenate([cos, cos, zeros], axis=1)
    sin_t = jnp.concatenate([-sin, sin, zeros], axis=1)
    return jnp.tile(cos_t, (reps, 1)), jnp.tile(sin_t, (reps, 1))


def _trunk(x3, mods, pos, conv_in, ssm_in, past, p, tag):
    G, L, D = x3.shape
    M = G * L
    grouped = L % SUBLANES == 0
    x = x3.reshape(M, D)

    def split_mods(m, n):
        arr = m[:, None, :] if grouped else jnp.repeat(m, L, axis=0)
        return [(arr, k) for k in range(n)]

    HV, DK, DV = p['HV'], p['DK'], p['DV']
    HQK = p['HQK']
    KEY, VAL = HQK * DK, HV * DV
    H, NOPE, P, R, V = p['H'], p['NOPE'], p['P'], p['R'], p['V']
    FF = p['w_down'].shape[1]
    cos_t, sin_t = _rope_tables(pos, P, G)
    scale = (NOPE + P) ** -0.5

    sh1, sc1, gt1, sh2, sc2, gt2 = split_mods(mods['l0'], 6)
    w_in_t = p['gdn_w_in_t']
    n_proj = 2 * KEY + 2 * VAL
    proj = mod_matmul(x, p['g_mix'][0], sh1, sc1, w_in_t, 0, n_proj, L, w_is_nk=True, name=f'{tag}_gdn_in')
    ba = mod_matmul(x, p['g_mix'][0], sh1, sc1, p['gdn_w_ba_t'], 0, LANES, L, w_is_nk=True, name=f'{tag}_gdn_ba')
    proj3 = proj.reshape(G, L, n_proj)
    o_g, ssm_new = gdn_scan(
        proj3, ba[:, :HV], ba[:, HV:2 * HV], conv_in, ssm_in, p['gdn_w_conv'][0], p['gdn_a_log'][0],
        p['gdn_dt_bias'][0], p['gdn_g_norm'][0], B=G, L=L, HQK=HQK, HV=HV, DK=DK, DV=DV, name=f'{tag}_gdn_scan')
    conv_new = proj3[:, L - (CONV_W - 1):, :2 * KEY + VAL]
    x = matmul_residual(o_g.reshape(M, VAL), p['gdn_w_out'], 0, x, gt1, L, name=f'{tag}_gdn_out')
    hff = mod_matmul(x, p['g_ffn'][0], sh2, sc2, p['w_gate_up'], 0, FF, L, swiglu=True, out_dtype=BF,
                     name=f'{tag}_ffn0_up')
    x = matmul_residual(hff, p['w_down'], 0, x, gt2, L, name=f'{tag}_ffn0_down')

    shk, sck = split_mods(mods['kv'], 2)
    ckv, kpe, kcat = shared_kv(x, p['kv_g_in'], shk, sck, p['kv_w_ext'], p['kv_g_norm'], cos_t, sin_t, L,
                               R=R, P=P, name=f'{tag}_kv')

    sh1, sc1, gt1, sh2, sc2, gt2 = split_mods(mods['l1'], 6)
    head_major = past is None
    q = mla_queries(x, p['g_mix'][1], sh1, sc1, p['mla_w_dq'][0], p['mla_g_q'][0], p['mla_w_uq_r'], p['kv_w_uk'],
                    cos_t, sin_t, L, H=H, NOPE=NOPE, P=P, R=R, head_major=head_major, name=f'{tag}_q')
    if head_major:
        o_lat = attention_prompt(q, kcat.reshape(G, L, R + LANES), R=R, scale=scale, name=f'{tag}_attn')
    else:
        cache_ckv, cache_kpe, page_table = past
        o_lat = attention_sample(q.reshape(G, L * H, R + LANES), kcat.reshape(G, L, R + LANES), cache_ckv,
                                 cache_kpe, page_table, T=L, H=H, R=R, P=P, scale=scale, name=f'{tag}_attn')
        o_lat = o_lat.reshape(M, H * R)
    o = value_up(o_lat, p['kv_w_uv'], H=H, R=R, V=V, head_major=head_major, name=f'{tag}_uv')
    x = matmul_residual(o, p['mla_w_o'], 0, x, gt1, L, name=f'{tag}_attn_out')
    hff = mod_matmul(x, p['g_ffn'][1], sh2, sc2, p['w_gate_up'], 1, FF, L, swiglu=True, out_dtype=BF,
                     name=f'{tag}_ffn1_up')
    x = matmul_residual(hff, p['w_down'], 1, x, gt2, L, name=f'{tag}_ffn1_down')

    shf, scf = split_mods(mods['final'], 2)
    y = modulate_rows(x, p['final_g'], shf, scf, L, name=f'{tag}_final')
    return (y.reshape(G, L, D), conv_new[None], ssm_new[None], ckv.reshape(G, L, R), kpe.reshape(G, L, P))


def kernel(x_prompt, x_sample, c_prompt, c_sample, cache_ckv, cache_kpe, page_table, state_ssm, state_conv, w_ada, b_ada, g_mix, g_ffn, w_gate_up, w_down, gdn_w_in, gdn_w_conv, gdn_a_log, gdn_dt_bias, gdn_g_norm, gdn_w_out, kv_w_ada, kv_b_ada, kv_g_in, kv_w_down, kv_g_norm, kv_w_uk, kv_w_uv, mla_w_dq, mla_g_q, mla_w_uq, mla_w_o, final_w_ada, final_b_ada, final_g):
    B, S, D = x_prompt.shape
    Bd, T, _ = x_sample.shape
    HV, DK, DV = state_ssm.shape[2:]
    KEY = (state_conv.shape[-1] - HV * DV) // 2
    R = cache_ckv.shape[-1]
    P = cache_kpe.shape[-1]
    QL = mla_w_dq.shape[-1]
    nope_total = kv_w_uk.shape[1]
    H = (mla_w_uq.shape[-1] - nope_total) // P
    NOPE = nope_total // H
    w_in_t = jnp.swapaxes(gdn_w_in, 1, 2)
    n_proj = 2 * KEY + 2 * HV * DV
    w_ba_t = jnp.concatenate([w_in_t[0, n_proj:], jnp.zeros((LANES - 2 * HV, D), F32)], axis=0)
    p = dict(w_ada=w_ada, b_ada=b_ada, g_mix=g_mix, g_ffn=g_ffn, w_gate_up=w_gate_up, w_down=w_down,
             gdn_w_in_t=w_in_t, gdn_w_ba_t=w_ba_t, gdn_w_conv=gdn_w_conv, gdn_a_log=gdn_a_log,
             gdn_dt_bias=gdn_dt_bias, gdn_g_norm=gdn_g_norm, gdn_w_out=gdn_w_out, kv_g_in=kv_g_in,
             kv_w_ext=kv_down_weight(kv_w_down, R), kv_g_norm=kv_g_norm, kv_w_uk=kv_w_uk, kv_w_uv=kv_w_uv,
             mla_w_dq=mla_w_dq, mla_g_q=mla_g_q, mla_w_uq_r=query_up_weight(mla_w_uq[0], H, NOPE),
             mla_w_o=mla_w_o, final_g=final_g,
             HV=HV, DK=DK, DV=DV, HQK=KEY // DK, H=H, NOPE=NOPE, P=P, R=R, V=kv_w_uv.shape[1] // H)

    n_c = B + Bd
    pad = -n_c % SUBLANES
    c_all = jnp.concatenate([c_prompt, c_sample, jnp.zeros((pad, D), F32)], axis=0)
    m_l0 = ada_dense(c_all, w_ada, b_ada, 0, name='ada_l0')
    m_l1 = ada_dense(c_all, w_ada, b_ada, 1, name='ada_l1')
    m_kv = ada_dense(c_all, kv_w_ada, kv_b_ada, 0, name='ada_kv')
    m_f = ada_dense(c_all, final_w_ada, final_b_ada, 0, name='ada_final')

    def mods(lo, hi):
        return dict(l0=m_l0[lo:hi], l1=m_l1[lo:hi], kv=m_kv[lo:hi], final=m_f[lo:hi])

    y_p, conv_p, ssm_p, ckv_p, kpe_p = _trunk(x_prompt, mods(0, B), jnp.arange(S), None, None, None, p, 'p')
    past_len = page_table.shape[1] * cache_ckv.shape[1]
    y_s, conv_s, ssm_s, ckv_s, kpe_s = _trunk(x_sample, mods(B, n_c), past_len + jnp.arange(T), state_conv[0],
                                              state_ssm[0], (cache_ckv, jnp.swapaxes(cache_kpe, 1, 2), page_table),
                                              p, 's')
    return (y_p, y_s, ssm_p, conv_p, ckv_p, kpe_p, ssm_s, conv_s, ckv_s, kpe_s)
```

```python
import functools

import jax
import jax.numpy as jnp
from jax import lax
from jax.experimental import pallas as pl
from jax.experimental.pallas import tpu as pltpu

EPS = 1e-6
ROPE_THETA = 10000.0
CONV_W = 4
GDN_CHUNK = 64
F32 = jnp.float32
BF = jnp.bfloat16

V7X_VMEM_BYTES = 64 * 1024 * 1024
VMEM_LIMIT = V7X_VMEM_BYTES - 8 * 1024 * 1024
VMEM_TILE_BUDGET = (VMEM_LIMIT * 9) // 10
LANES = 128
SUBLANES = 8
BF16_ROWS = 16
TRI_BLOCK = 16
STACK_ROWS = 128
MAX_STACK_HEADS = 8
SEQS_PER_STEP = 4
STACKS_PER_STEP = 16
FRONT_GROUP_ROWS = 256
ATTN_GROUP_ROWS = 256
ATTN_SOFTMAX_ROWS = 64
ATTN_PAGES_PER_STEP = 64
ATTN_PAGE_GROUP = 4
ATTN_PAGE_SLOTS = 2
Q_HEADS_PER_STEP = 4
LOG2_E = 1.4426950408889634


def _params(n_axes):
    return pltpu.CompilerParams(dimension_semantics=("arbitrary",) * n_axes, vmem_limit_bytes=VMEM_LIMIT)


def _pick(n, cands):
    for c in cands:
        if n % c == 0:
            return c
    return n


def _silu(x):
    return x * jax.nn.sigmoid(x)


def _contract(a, b, ca, cb):
    batch = tuple(range(a.ndim - 2))
    dims = (((a.ndim - 2 + ca,), (b.ndim - 2 + cb,)), (batch, batch))
    return lax.dot_general(a, b, dims, preferred_element_type=F32)


def _dot(a, b):
    return _contract(a, b, 1, 0)


def _dot_nt(a, b):
    return _contract(a, b, 1, 1)


def _dot_tn(a, b):
    return _contract(a, b, 0, 0)


def _split_bf16(a):
    hi = a.astype(BF)
    lo = (a - hi.astype(F32)).astype(BF)
    return hi, lo


def _dot3(a, b):
    ah, al = _split_bf16(a)
    bh, bl = _split_bf16(b)
    return _dot(ah, bh) + _dot(ah, bl) + _dot(al, bh)


def _modulate_rows(x_ref, g_ref, sh_ref, sc_ref, h_ref, rows):
    tm = x_ref.shape[0]
    per_row = sh_ref.shape[0] != 1

    def body(r, carry):
        sl = pl.ds(pl.multiple_of(r * rows, rows), rows)
        x = x_ref[sl, :]
        y = x * lax.rsqrt(jnp.mean(x * x, axis=-1, keepdims=True) + EPS) * g_ref[...]
        sc = sc_ref[sl, :] if per_row else sc_ref[...]
        sh = sh_ref[sl, :] if per_row else sh_ref[...]
        h_ref[sl, :] = (y * (1.0 + sc) + sh).astype(h_ref.dtype)
        return carry

    lax.fori_loop(0, tm // rows, body, 0)


def _grouped(mod):
    return mod[0].ndim == 3


def _mod_spec(mod, K, tm, rows_per_group):
    arr, k = mod
    if arr.ndim == 3:
        tiles_per_group = rows_per_group // tm
        return pl.BlockSpec((None, 1, K), lambda i, *_: (i // tiles_per_group, 0, k))
    return pl.BlockSpec((tm, K), lambda i, *_: (i, k))


def _w_spec(w, layer, K, tn, col_off_blocks=0):
    if w.ndim == 3:
        return pl.BlockSpec((None, K, tn), lambda i, j: (layer, 0, j + col_off_blocks))
    return pl.BlockSpec((K, tn), lambda i, j: (0, j + col_off_blocks))


def _modmm_kernel(x_ref, g_ref, sh_ref, sc_ref, w_ref, o_ref, h_ref, *, rows, w_is_nk):
    @pl.when(pl.program_id(1) == 0)
    def _():
        _modulate_rows(x_ref, g_ref, sh_ref, sc_ref, h_ref, rows)

    dot = _dot_nt if w_is_nk else _dot
    o_ref[...] = dot(h_ref[...], w_ref[...].astype(BF)).astype(o_ref.dtype)


def _modmm_swiglu_kernel(x_ref, g_ref, sh_ref, sc_ref, wg_ref, wu_ref, o_ref, h_ref, *, rows):
    @pl.when(pl.program_id(1) == 0)
    def _():
        _modulate_rows(x_ref, g_ref, sh_ref, sc_ref, h_ref, rows)

    h = h_ref[...]
    gate = _dot(h, wg_ref[...].astype(BF))
    up = _dot(h, wu_ref[...].astype(BF))
    o_ref[...] = (_silu(gate) * up).astype(o_ref.dtype)


def _gdn_front_kernel(x_ref, g_ref, sh_ref, sc_ref, w_ref, cw_ref, o_ref, tail_ref, h_ref, cb, halo,
                      *, rows, tiles_per_seq, n_qk, n_conv, DK):
    i = pl.program_id(0)
    j = pl.program_id(1)
    tm, tn = o_ref.shape
    tail_lo = SUBLANES - (CONV_W - 1)
    group = min(FRONT_GROUP_ROWS, tm)
    n_groups = tm // group
    rc = min(ATTN_SOFTMAX_ROWS, group)

    @pl.when(j == 0)
    def _():
        _modulate_rows(x_ref, g_ref, sh_ref, sc_ref, h_ref, rows)

    def project(g, w_bf):
        r0 = g * group
        cb[SUBLANES + r0:SUBLANES + r0 + group, :] = _dot_nt(h_ref[r0:r0 + group, :], w_bf)

    def activate(g, kind):
        for r0 in range(g * group, (g + 1) * group, rc):
            if kind == 'z':
                act = _silu(cb[SUBLANES + r0:SUBLANES + r0 + rc, :])
            else:
                acc = cb[tail_lo + r0:tail_lo + r0 + rc, :] * cw_ref[0:1, :]
                for t in range(1, CONV_W):
                    acc = acc + cb[tail_lo + r0 + t:tail_lo + r0 + t + rc, :] * cw_ref[t:t + 1, :]
                act = _silu(acc)
                if kind in ('q', 'k'):
                    heads = []
                    for hh in range(tn // DK):
                        a = act[:, hh * DK:(hh + 1) * DK]
                        f = lax.rsqrt(jnp.sum(a * a, axis=-1, keepdims=True) + EPS)
                        heads.append(a * (f * (DK ** -0.5) if kind == 'q' else f))
                    act = jnp.concatenate(heads, axis=1)
            o_ref[r0:r0 + rc, :] = act

    def tile(kind):
        if kind != 'z':
            first = lax.rem(i, tiles_per_seq) == 0

            @pl.when(first)
            def _():
                cb[0:SUBLANES, :] = jnp.zeros((SUBLANES, tn), F32)

            @pl.when(jnp.logical_not(first))
            def _():
                cb[0:SUBLANES, :] = halo[j]

        w_bf = w_ref[...].astype(BF)
        for g in range(n_groups + 1):
            if g < n_groups:
                project(g, w_bf)
            if g >= 1:
                activate(g - 1, kind)
        last_rows = cb[tm:tm + SUBLANES, :]
        tail_ref[...] = last_rows
        if kind != 'z':
            halo[j] = last_rows

    n_q = n_qk // 2
    pl.when(j < n_q)(lambda: tile('q'))
    pl.when((j >= n_q) & (j < n_qk))(lambda: tile('k'))
    pl.when((j >= n_qk) & (j < n_conv))(lambda: tile('v'))
    pl.when(j >= n_conv)(lambda: tile('z'))


def gdn_front(x, g, shift, scale, w_t, w_conv, rows_per_group, *, KEY, VAL, DK, name):
    M, K = x.shape
    n_out = 2 * KEY + 2 * VAL
    tm = _row_tile(M, rows_per_group, True)
    tn = _pick(KEY, (512, 256, 128))
    assert _grouped(shift) and rows_per_group % tm == 0 and VAL % tn == 0 and tn % DK == 0
    tiles_per_seq = rows_per_group // tm
    n_qk, n_conv = 2 * KEY // tn, (2 * KEY + VAL) // tn
    rows = _pick(tm, (128, 64, 32, 16, 8))
    mod_specs = [_mod_spec(m, K, tm, rows_per_group) for m in (shift, scale)]
    body = functools.partial(_gdn_front_kernel, rows=rows, tiles_per_seq=tiles_per_seq, n_qk=n_qk,
                             n_conv=n_conv, DK=DK)
    return pl.pallas_call(
        body,
        grid=(M // tm, n_out // tn),
        in_specs=[pl.BlockSpec((tm, K), lambda i, j: (i, 0)),
                  pl.BlockSpec((1, K), lambda i, j: (0, 0)),
                  *mod_specs,
                  pl.BlockSpec((None, tn, K), lambda i, j: (0, j, 0)),
                  pl.BlockSpec((CONV_W, tn), lambda i, j: (0, jnp.minimum(j, n_conv - 1)))],
        out_specs=[pl.BlockSpec((tm, tn), lambda i, j: (i, j)),
                   pl.BlockSpec((None, SUBLANES, tn), lambda i, j: (i, 0, j))],
        out_shape=[jax.ShapeDtypeStruct((M, n_out), F32),
                   jax.ShapeDtypeStruct((M // tm, SUBLANES, n_out), F32)],
        scratch_shapes=[pltpu.VMEM((tm, K), BF), pltpu.VMEM((SUBLANES + tm, tn), F32),
                        pltpu.VMEM((n_conv, SUBLANES, tn), F32)],
        compiler_params=_params(2),
        name=name,
    )(x, g.reshape(1, K), shift[0], scale[0], w_t, w_conv)


def _row_tile(M, rows_per_group, grouped):
    base = rows_per_group if grouped else M
    return _pick(base, (1024, 512, 256, 128, 64, 32, 16, 8))


def _col_tile(N, fixed_bytes, bytes_per_col):
    for tn in (1024, 512, 256):
        if N % tn == 0 and fixed_bytes + bytes_per_col * tn <= VMEM_TILE_BUDGET:
            return tn
    return _pick(N, (128,))


def _weight_col_bytes(K, n_weights=1):
    return n_weights * K * (2 * 4 + 2)


def mod_matmul(x, g, shift, scale, w, layer, n_out, rows_per_group, *, swiglu=False, w_is_nk=False,
               out_dtype=F32, name):
    M, K = x.shape
    tm = _row_tile(M, rows_per_group, _grouped(shift))
    n_w = 2 if swiglu else 1
    out_bytes = jnp.dtype(out_dtype).itemsize
    tn = _col_tile(n_out, tm * K * (2 * 4 + 2),
                   _weight_col_bytes(K, n_w) + tm * (2 * out_bytes + 4 * n_w))
    rows = _pick(tm, (128, 64, 32, 16, 8))
    mod_specs = [_mod_spec(m, K, tm, rows_per_group) for m in (shift, scale)]
    if not w_is_nk:
        w_spec = _w_spec(w, layer, K, tn)
    elif w.ndim == 3:
        w_spec = pl.BlockSpec((None, tn, K), lambda i, j: (layer, j, 0))
    else:
        w_spec = pl.BlockSpec((tn, K), lambda i, j: (j, 0))
    in_specs = [pl.BlockSpec((tm, K), lambda i, j: (i, 0)),
                pl.BlockSpec((1, K), lambda i, j: (0, 0)),
                *mod_specs, w_spec]
    args = [x, g.reshape(1, K), shift[0], scale[0], w]
    if swiglu:
        in_specs.append(_w_spec(w, layer, K, tn, n_out // tn))
        args.append(w)
        body = functools.partial(_modmm_swiglu_kernel, rows=rows)
    else:
        body = functools.partial(_modmm_kernel, rows=rows, w_is_nk=w_is_nk)
    return pl.pallas_call(
        body,
        grid=(M // tm, n_out // tn),
        in_specs=in_specs,
        out_specs=pl.BlockSpec((tm, tn), lambda i, j: (i, j)),
        out_shape=jax.ShapeDtypeStruct((M, n_out), out_dtype),
        scratch_shapes=[pltpu.VMEM((tm, K), BF)],
        compiler_params=_params(2),
        name=name,
    )(*args)


def _mmres_kernel(a_ref, w_ref, res_ref, gate_ref, o_ref):
    y = _dot(a_ref[...], w_ref[...].astype(BF))
    o_ref[...] = res_ref[...] + gate_ref[...] * y


def matmul_residual(a, w, layer, res, gate, rows_per_group, *, name):
    M, K = a.shape
    N = res.shape[1]
    tm = _row_tile(M, rows_per_group, _grouped(gate))
    tn = _col_tile(N, tm * K * 2 * a.dtype.itemsize, _weight_col_bytes(K) + tm * (4 * 4 + 4))
    gate_arr, gate_k = gate
    gate_off = gate_k * (N // tn)
    if _grouped(gate):
        tiles_per_group = rows_per_group // tm
        gate_spec = pl.BlockSpec((None, 1, tn), lambda i, j: (i // tiles_per_group, 0, gate_off + j))
    else:
        gate_spec = pl.BlockSpec((tm, tn), lambda i, j: (i, gate_off + j))
    return pl.pallas_call(
        _mmres_kernel,
        grid=(M // tm, N // tn),
        in_specs=[pl.BlockSpec((tm, K), lambda i, j: (i, 0)),
                  _w_spec(w, layer, K, tn),
                  pl.BlockSpec((tm, tn), lambda i, j: (i, j)),
                  gate_spec],
        out_specs=pl.BlockSpec((tm, tn), lambda i, j: (i, j)),
        out_shape=jax.ShapeDtypeStruct((M, N), F32),
        compiler_params=_params(2),
        name=name,
    )(a, w, res, gate_arr)


def _ada_kernel(c_ref, w_ref, b_ref, o_ref):
    a = _silu(c_ref[...]).astype(BF)
    o_ref[...] = _dot(a, w_ref[...].astype(BF)) + b_ref[...]


def ada_dense(c, w, b, layer, *, name):
    M, K = c.shape
    N = w.shape[-1]
    tn = _col_tile(N, 2 * M * K * 4, _weight_col_bytes(K) + M * 3 * 4)
    if b.ndim == 2:
        b_spec = pl.BlockSpec((None, 1, tn), lambda i, j: (layer, 0, j))
        b = b.reshape(b.shape[0], 1, N)
    else:
        b_spec = pl.BlockSpec((1, tn), lambda i, j: (0, j))
        b = b.reshape(1, N)
    return pl.pallas_call(
        _ada_kernel,
        grid=(1, N // tn),
        in_specs=[pl.BlockSpec((M, K), lambda i, j: (0, 0)), _w_spec(w, layer, K, tn), b_spec],
        out_specs=pl.BlockSpec((M, tn), lambda i, j: (0, j)),
        out_shape=jax.ShapeDtypeStruct((M, N), F32),
        compiler_params=_params(2),
        name=name,
    )(c, w, b)


def _modulate_kernel(x_ref, g_ref, sh_ref, sc_ref, o_ref, *, rows):
    _modulate_rows(x_ref, g_ref, sh_ref, sc_ref, o_ref, rows)


def modulate_rows(x, g, shift, scale, rows_per_group, *, name):
    M, K = x.shape
    tm = _row_tile(M, rows_per_group, _grouped(shift))
    rows = _pick(tm, (128, 64, 32, 16, 8))
    mod_specs = [_mod_spec(m, K, tm, rows_per_group) for m in (shift, scale)]
    return pl.pallas_call(
        functools.partial(_modulate_kernel, rows=rows),
        grid=(M // tm,),
        in_specs=[pl.BlockSpec((tm, K), lambda i: (i, 0)), pl.BlockSpec((1, K), lambda i: (0, 0)),
                  *mod_specs],
        out_specs=pl.BlockSpec((tm, K), lambda i: (i, 0)),
        out_shape=jax.ShapeDtypeStruct((M, K), F32),
        compiler_params=_params(1),
        name=name,
    )(x, g.reshape(1, K), shift[0], scale[0])


def _dot1(a, b):
    return _dot(a.astype(BF), b.astype(BF))


def _tri_inverse(n_low, ii, jj, cp, nil):
    rn = n_low.shape[-1]
    eye = (ii == jj).astype(F32)
    base = min(TRI_BLOCK, cp)
    shift = base.bit_length() - 1
    nd = jnp.where((ii >> shift) == (jj >> shift), n_low, 0.0)
    p = eye - nd
    if nil > 2:
        npow = _dot1(nd, nd)
        pw = 2
        while 2 * pw - 1 < nil - 1:
            both = _dot1(jnp.concatenate([p, npow], axis=-2), npow)
            p = p + both[:, :rn]
            npow = both[:, rn:]
            pw *= 2
        p = p + _dot1(p, npow)
    size = base
    while size < cp:
        s = size.bit_length() - 1
        off = ((ii >> (s + 1)) == (jj >> (s + 1))) & (((ii >> s) & 1) == 1) & (((jj >> s) & 1) == 0)
        x = _dot1(jnp.where(off, n_low, 0.0), p)
        p = p - _dot1(p, x)
        size *= 2
    resid = (eye - p) - _dot3(n_low, p)
    return p + _dot1(p, resid)


def _delta_chunk(q_st, k_st, v_st, beta_row, g_row, load_state, store_state, *, G, Cp, nil):
    S, rn, dv = v_st.shape
    ii = lax.broadcasted_iota(jnp.int32, (1, rn, rn), 1)
    jj = lax.broadcasted_iota(jnp.int32, (1, rn, rn), 2)
    sh = Cp.bit_length() - 1
    same = (ii >> sh) == (jj >> sh)
    eye = ii == jj
    causal = same & (jj <= ii)
    g_mat = jnp.broadcast_to(g_row, (S, rn, rn))
    g_col = jnp.sum(jnp.where(eye, g_mat, 0.0), axis=2, keepdims=True)
    gc_col = jnp.sum(jnp.where(causal, g_mat, 0.0), axis=2, keepdims=True)
    gl_col = jnp.sum(jnp.where(same, g_mat, 0.0), axis=2, keepdims=True)
    gc_row = jnp.sum(jnp.where(same & (ii <= jj), jnp.broadcast_to(g_col, (S, rn, rn)), 0.0), axis=1, keepdims=True)
    beta_col = jnp.sum(jnp.where(eye, jnp.broadcast_to(beta_row, (S, rn, rn)), 0.0), axis=2, keepdims=True)
    decay = jnp.exp(jnp.where(causal, gc_col - gc_row, -jnp.inf))
    eg = jnp.exp(gc_col)
    kb = k_st * beta_col
    both = _dot_nt(jnp.concatenate([kb, q_st], axis=1).astype(BF), k_st.astype(BF))
    n_low = jnp.where(same & (jj < ii), both[:, :rn] * decay, 0.0)
    t_inv = _tri_inverse(n_low, ii, jj, Cp, nil).astype(BF)
    uw = _dot(t_inv, jnp.concatenate([v_st * beta_col, kb * eg], axis=2).astype(BF))
    u, w = uw[:, :, :dv], uw[:, :, dv:]
    qe = q_st * eg
    k_dec = k_st * jnp.exp(gl_col - gc_col)
    pad = jnp.zeros((S, -Cp % BF16_ROWS, k_dec.shape[2]), F32)

    def bf16_rows(x):
        return (jnp.concatenate([x, pad], axis=1) if pad.shape[1] else x).astype(BF)

    v_new, q_s = [], []
    for h in range(G):
        r = slice(h * Cp, (h + 1) * Cp)
        s_old = jnp.stack([load_state(s * G + h) for s in range(S)], axis=0)
        ws_qs = _dot(jnp.concatenate([w[:, r], qe[:, r]], axis=1).astype(BF), s_old.astype(BF))
        v_new_h = u[:, r] - ws_qs[:, :Cp]
        q_s.append(ws_qs[:, Cp:])
        v_new.append(v_new_h)
        s_new = (s_old * jnp.exp(gl_col[:, h * Cp:h * Cp + 1])
                 + _dot_tn(bf16_rows(k_dec[:, r]), bf16_rows(v_new_h)))
        for s in range(S):
            store_state(s * G + h, s_new[s])
    v_new = jnp.concatenate(v_new, axis=1).astype(BF)
    return jnp.concatenate(q_s, axis=1) + _dot((both[:, rn:] * decay).astype(BF), v_new)


def _gdn_kernel(*refs, C, Cp, G, NS, BB, rep, DK, DV, has_state, single_chunk, activated):
    q_ref, k_ref, v_ref, z_ref, b_ref, a_ref, alog_ref, dtb_ref, wq_ref, wk_ref, wv_ref, gn_ref = refs[:12]
    pos = 12
    if has_state:
        cq_ref, ck_ref, cv_ref, s0_ref = refs[pos:pos + 4]
        pos += 4
    o_ref, so_ref = refs[pos:pos + 2]
    s_scr, cbq, cbk, cbv = refs[pos + 2:]

    c = pl.program_id(2)
    last = pl.num_programs(2) - 1
    tail_lo = SUBLANES - (CONV_W - 1)
    direct_state = has_state and single_chunk

    @pl.when(c == 0)
    def _():
        for cb in (cbq, cbk, cbv):
            cb[...] = jnp.zeros_like(cb)
        if has_state:
            cbq[:, tail_lo:SUBLANES, :] = cq_ref[...]
            cbk[:, tail_lo:SUBLANES, :] = ck_ref[...]
            cbv[:, tail_lo:SUBLANES, :] = cv_ref[...]
            if not direct_state:
                for bb in range(BB):
                    s_scr[bb * G:(bb + 1) * G] = s0_ref[bb]
        else:
            s_scr[...] = jnp.zeros_like(s_scr)

    def conv(cb, x_ref, w_ref, bb):
        cb[bb, SUBLANES:SUBLANES + C, :] = x_ref[bb]
        acc = cb[bb, tail_lo:tail_lo + Cp, :] * w_ref[0:1, :]
        for j in range(1, CONV_W):
            acc = acc + cb[bb, tail_lo + j:tail_lo + j + Cp, :] * w_ref[j:j + 1, :]
        cb[bb, tail_lo:SUBLANES, :] = cb[bb, C + tail_lo:C + SUBLANES, :]
        return _silu(acc)

    padded = Cp != C
    if padded:
        row_ok = lax.broadcasted_iota(jnp.int32, (Cp, 1), 0) < C
        lane_ok = (lax.broadcasted_iota(jnp.int32, (1, G * Cp), 1) & (Cp - 1)) < C
    GS = G // NS
    rn = GS * Cp
    q_parts, k_parts, v_parts, beta_rows, g_rows = [], [], [], [], []
    for bb in range(BB):
        if activated:
            qc, kc, vc = q_ref[bb], k_ref[bb], v_ref[bb]
        else:
            qc = conv(cbq, q_ref, wq_ref, bb)
            kc = conv(cbk, k_ref, wk_ref, bb)
            vc = conv(cbv, v_ref, wv_ref, bb)
        for i in range(G // rep):
            qh = qc[:, i * DK:(i + 1) * DK]
            kh = kc[:, i * DK:(i + 1) * DK]
            if not activated:
                qh = qh * lax.rsqrt(jnp.sum(qh * qh, axis=-1, keepdims=True) + EPS) * (DK ** -0.5)
                kh = kh * lax.rsqrt(jnp.sum(kh * kh, axis=-1, keepdims=True) + EPS)
            if padded:
                kh = jnp.where(row_ok, kh, 0.0)
            for e in range(rep):
                hl = i * rep + e
                vh = vc[:, hl * DV:(hl + 1) * DV]
                if padded:
                    vh = jnp.where(row_ok, vh, 0.0)
                q_parts.append(qh)
                k_parts.append(kh)
                v_parts.append(vh)
        beta_row = jax.nn.sigmoid(b_ref[bb, c])
        sp_in = a_ref[bb, c] + dtb_ref[...]
        softplus = jnp.maximum(sp_in, 0.0) + jnp.log1p(jnp.exp(-jnp.abs(sp_in)))
        g_row = -jnp.exp(alog_ref[...]) * softplus
        if padded:
            beta_row = jnp.where(lane_ok, beta_row, 0.0)
            g_row = jnp.where(lane_ok, g_row, 0.0)
        beta_rows += [beta_row[:, s * rn:(s + 1) * rn] for s in range(NS)]
        g_rows += [g_row[:, s * rn:(s + 1) * rn] for s in range(NS)]

    def load_state(i):
        return s0_ref[i // G, i % G] if direct_state else s_scr[i]

    def store_state(i, value):
        if direct_state:
            so_ref[i // G, i % G] = value
        else:
            s_scr[i] = value

    def stacks(parts):
        return jnp.stack([jnp.concatenate(parts[s * GS:(s + 1) * GS], axis=0) for s in range(BB * NS)], axis=0)

    o = _delta_chunk(stacks(q_parts), stacks(k_parts), stacks(v_parts), jnp.stack(beta_rows, axis=0),
                     jnp.stack(g_rows, axis=0), load_state, store_state, G=GS, Cp=Cp, nil=min(C, TRI_BLOCK))
    on = o * lax.rsqrt(jnp.mean(o * o, axis=-1, keepdims=True) + EPS) * gn_ref[...]
    for bb in range(BB):
        for s in range(NS):
            for h in range(GS):
                hl = s * GS + h
                zg = z_ref[bb, :, hl * DV:(hl + 1) * DV]
                o_ref[bb, :, hl * DV:(hl + 1) * DV] = (
                    on[bb * NS + s, h * Cp:h * Cp + C] * (zg if activated else _silu(zg))).astype(o_ref.dtype)

    if not direct_state:
        @pl.when(c == last)
        def _():
            for bb in range(BB):
                so_ref[bb] = s_scr[bb * G:(bb + 1) * G]


def gdn_scan(proj, b_logit, a_logit, conv_prev, s0, w_conv, a_log, dt_bias, g_norm, *, B, L, HQK, HV, DK, DV,
             activated, name):
    rep = HV // HQK
    KEY, VAL = HQK * DK, HV * DV
    C = _pick(L, (GDN_CHUNK,))
    n = L // C
    Cp = -(-C // SUBLANES) * SUBLANES
    assert Cp & (Cp - 1) == 0 and L >= CONV_W - 1 and not (activated and Cp != C)
    GS = min(HV, max(rep, min(STACK_ROWS // Cp, MAX_STACK_HEADS)))
    NS = _pick(HV // GS, (STACKS_PER_STEP, 4, 2, 1))
    G = NS * GS
    assert HV % G == 0 and GS % rep == 0
    has_state = s0 is not None
    BB = _pick(B, (SEQS_PER_STEP, 1)) if n == 1 else 1
    qw, vw = (G // rep) * DK, G * DV
    k_off, v_off, z_off = KEY // qw, 2 * KEY // vw, (2 * KEY + VAL) // vw

    def stack_rows(t):
        t = t.reshape(B, n, C, HV // G, G)
        t = jnp.pad(t, ((0, 0), (0, 0), (0, Cp - C), (0, 0), (0, 0)))
        return jnp.transpose(t, (0, 3, 1, 4, 2)).reshape(B, HV // G, n, 1, G * Cp)

    def stack_heads(v):
        return jnp.repeat(v.reshape(HV // G, 1, G), Cp, axis=-1)

    gate_spec = pl.BlockSpec((BB, None, n, 1, G * Cp), lambda b, h, c: (b, h, 0, 0, 0))
    head_spec = pl.BlockSpec((None, 1, G * Cp), lambda b, h, c: (h, 0, 0))
    in_specs = [
        pl.BlockSpec((BB, C, qw), lambda b, h, c: (b, c, h)),
        pl.BlockSpec((BB, C, qw), lambda b, h, c: (b, c, k_off + h)),
        pl.BlockSpec((BB, C, vw), lambda b, h, c: (b, c, v_off + h)),
        pl.BlockSpec((BB, C, vw), lambda b, h, c: (b, c, z_off + h)),
        gate_spec, gate_spec, head_spec, head_spec,
        pl.BlockSpec((CONV_W, qw), lambda b, h, c: (0, h)),
        pl.BlockSpec((CONV_W, qw), lambda b, h, c: (0, k_off + h)),
        pl.BlockSpec((CONV_W, vw), lambda b, h, c: (0, v_off + h)),
        pl.BlockSpec((1, DV), lambda b, h, c: (0, 0)),
    ]
    args = [proj, proj, proj, proj, stack_rows(b_logit), stack_rows(a_logit), stack_heads(a_log),
            stack_heads(dt_bias), w_conv, w_conv, w_conv, g_norm.reshape(1, DV)]
    if has_state:
        in_specs += [
            pl.BlockSpec((BB, CONV_W - 1, qw), lambda b, h, c: (b, 0, h)),
            pl.BlockSpec((BB, CONV_W - 1, qw), lambda b, h, c: (b, 0, k_off + h)),
            pl.BlockSpec((BB, CONV_W - 1, vw), lambda b, h, c: (b, 0, v_off + h)),
            pl.BlockSpec((BB, G, DK, DV), lambda b, h, c: (b, h, 0, 0)),
        ]
        args += [conv_prev, conv_prev, conv_prev, s0]
    body = functools.partial(_gdn_kernel, C=C, Cp=Cp, G=G, NS=NS, BB=BB, rep=rep, DK=DK, DV=DV,
                             has_state=has_state, single_chunk=n == 1, activated=activated)
    return pl.pallas_call(
        body,
        grid=(B // BB, HV // G, n),
        in_specs=in_specs,
        out_specs=[pl.BlockSpec((BB, C, vw), lambda b, h, c: (b, c, h)),
                   pl.BlockSpec((BB, G, DK, DV), lambda b, h, c: (b, h, 0, 0))],
        out_shape=[jax.ShapeDtypeStruct((B, L, VAL), BF), jax.ShapeDtypeStruct((B, HV, DK, DV), F32)],
        scratch_shapes=[pltpu.VMEM((BB * G, DK, DV), F32),
                        pltpu.VMEM((BB, SUBLANES + Cp, qw), F32),
                        pltpu.VMEM((BB, SUBLANES + Cp, qw), F32),
                        pltpu.VMEM((BB, SUBLANES + Cp, vw), F32)],
        compiler_params=_params(3),
        name=name,
    )(*args)


def _kv_kernel(x_ref, g_ref, sh_ref, sc_ref, w_ref, gkv_ref, cos_ref, sin_ref,
               ckv_ref, kpe_ref, kcat_ref, h_ref, *, rows, R, P):
    _modulate_rows(x_ref, g_ref, sh_ref, sc_ref, h_ref, rows)
    y = _dot(h_ref[...], w_ref[...].astype(BF))
    c = y[:, :R]
    ckv = c * lax.rsqrt(jnp.mean(c * c, axis=-1, keepdims=True) + EPS) * gkv_ref[...]
    rot = y[:, R:R + LANES] * cos_ref[...] + y[:, R + LANES:R + 2 * LANES] * sin_ref[...]
    ckv_ref[...] = ckv
    kpe_ref[...] = rot[:, :P]
    kcat_ref[:, :R] = ckv.astype(BF)
    kcat_ref[:, R:] = rot.astype(BF)


def _rope_weight_cols(w_pe):
    P = w_pe.shape[-1]
    half = P // 2
    zeros = jnp.zeros(w_pe.shape[:-1] + (LANES - P,), w_pe.dtype)
    swapped = jnp.concatenate([w_pe[..., half:], w_pe[..., :half]], axis=-1)
    return jnp.concatenate([w_pe, zeros, swapped, zeros], axis=-1)


def kv_down_weight(w_down, R):
    return jnp.concatenate([w_down[:, :R], _rope_weight_cols(w_down[:, R:])], axis=1)


def query_up_weight(w_uq, H, NOPE):
    QL = w_uq.shape[0]
    w_heads = w_uq.reshape(QL, H, -1)
    per_head = jnp.concatenate([w_heads[..., :NOPE], _rope_weight_cols(w_heads[..., NOPE:])], axis=-1)
    return jnp.transpose(per_head, (1, 0, 2))


def shared_kv(x, g, shift, scale, w_ext, g_kv, cos_t, sin_t, rows_per_group, *, R, P, name):
    M, K = x.shape
    tm = _row_tile(M, rows_per_group, _grouped(shift))
    tm = min(tm, 512)
    rows = _pick(tm, (128, 64, 32, 16, 8))
    NW = R + 2 * LANES
    mod_specs = [_mod_spec(m, K, tm, rows_per_group) for m in (shift, scale)]
    row = lambda i: (i, 0)
    fixed = lambda i: (0, 0)
    return pl.pallas_call(
        functools.partial(_kv_kernel, rows=rows, R=R, P=P),
        grid=(M // tm,),
        in_specs=[pl.BlockSpec((tm, K), row), pl.BlockSpec((1, K), fixed), *mod_specs,
                  pl.BlockSpec((K, NW), fixed), pl.BlockSpec((1, R), fixed),
                  pl.BlockSpec((tm, LANES), row), pl.BlockSpec((tm, LANES), row)],
        out_specs=[pl.BlockSpec((tm, R), row), pl.BlockSpec((tm, P), row), pl.BlockSpec((tm, R + LANES), row)],
        out_shape=[jax.ShapeDtypeStruct((M, R), F32), jax.ShapeDtypeStruct((M, P), F32),
                   jax.ShapeDtypeStruct((M, R + LANES), BF)],
        scratch_shapes=[pltpu.VMEM((tm, K), BF)],
        compiler_params=_params(1),
        name=name,
    )(x, g.reshape(1, K), shift[0], scale[0], w_ext, g_kv.reshape(1, R), cos_t, sin_t)


def _q_kernel(x_ref, g_ref, sh_ref, sc_ref, wdq_ref, gq_ref, wuq_ref, wuk_ref, cos_ref, sin_ref,
              o_ref, cq_ref, h_ref, *, rows, NOPE, R, HB, head_major):
    @pl.when(pl.program_id(1) == 0)
    def _():
        _modulate_rows(x_ref, g_ref, sh_ref, sc_ref, h_ref, rows)
        c = _dot(h_ref[...], wdq_ref[...].astype(BF))
        cq_ref[...] = (c * lax.rsqrt(jnp.mean(c * c, axis=-1, keepdims=True) + EPS) * gq_ref[...]).astype(BF)

    W = R + LANES
    for hb in range(HB):
        qf = _dot(cq_ref[...], wuq_ref[hb].astype(BF))
        q_lat = _dot_nt(qf[:, :NOPE].astype(BF), wuk_ref[:, hb * NOPE:(hb + 1) * NOPE].astype(BF))
        rot = qf[:, NOPE:NOPE + LANES] * cos_ref[...] + qf[:, NOPE + LANES:NOPE + 2 * LANES] * sin_ref[...]
        if head_major:
            o_ref[hb, :, :R] = q_lat.astype(o_ref.dtype)
            o_ref[hb, :, R:] = rot.astype(o_ref.dtype)
        else:
            o_ref[:, hb * W:hb * W + R] = q_lat.astype(o_ref.dtype)
            o_ref[:, hb * W + R:(hb + 1) * W] = rot.astype(o_ref.dtype)


def mla_queries(x, g, shift, scale, w_dq, g_q, w_uq_r, w_uk, cos_t, sin_t, rows_per_group, *,
                H, NOPE, P, R, head_major, name):
    M, K = x.shape
    QL = w_dq.shape[1]
    tm = _row_tile(M, rows_per_group, True) if head_major else _row_tile(M, rows_per_group, _grouped(shift))
    tm = min(tm, 512)
    rows = _pick(tm, (128, 64, 32, 16, 8))
    NQ = NOPE + 2 * LANES
    W = R + LANES
    HB = _pick(H, (Q_HEADS_PER_STEP, 2, 1))
    mod_specs = [_mod_spec(m, K, tm, rows_per_group) for m in (shift, scale)]
    row = lambda i, h: (i, 0)
    fixed = lambda i, h: (0, 0)
    if head_major:
        tiles = rows_per_group // tm
        out_spec = pl.BlockSpec((None, HB, tm, W), lambda i, h: (i // tiles, h, i % tiles, 0))
        out_shape = jax.ShapeDtypeStruct((M // rows_per_group, H, rows_per_group, W), BF)
    else:
        out_spec = pl.BlockSpec((tm, HB * W), lambda i, h: (i, h))
        out_shape = jax.ShapeDtypeStruct((M, H * W), BF)
    return pl.pallas_call(
        functools.partial(_q_kernel, rows=rows, NOPE=NOPE, R=R, HB=HB, head_major=head_major),
        grid=(M // tm, H // HB),
        in_specs=[pl.BlockSpec((tm, K), row), pl.BlockSpec((1, K), fixed), *mod_specs,
                  pl.BlockSpec((K, QL), fixed), pl.BlockSpec((1, QL), fixed),
                  pl.BlockSpec((HB, QL, NQ), lambda i, h: (h, 0, 0)),
                  pl.BlockSpec((R, HB * NOPE), lambda i, h: (0, h)),
                  pl.BlockSpec((tm, LANES), row), pl.BlockSpec((tm, LANES), row)],
        out_specs=out_spec,
        out_shape=out_shape,
        scratch_shapes=[pltpu.VMEM((tm, QL), BF), pltpu.VMEM((tm, K), BF)],
        compiler_params=_params(2),
        name=name,
    )(x, g.reshape(1, K), shift[0], scale[0], w_dq, g_q.reshape(1, QL), w_uq_r, w_uk, cos_t, sin_t)


def _attn_prompt_kernel(q_ref, k_ref, o_ref, m_scr, l_scr, a_scr, acc_scr, s_scr, p_scr, *, tq, tk, H, R, scale):
    qi = pl.program_id(1)
    kj = pl.program_id(2)

    @pl.when(kj == 0)
    def _():
        m_scr[...] = jnp.full_like(m_scr, -jnp.inf)
        l_scr[...] = jnp.zeros_like(l_scr)
        acc_scr[...] = jnp.zeros_like(acc_scr)

    rows = H * tq
    hg = max(1, ATTN_GROUP_ROWS // tq)
    group = hg * tq
    n_groups = rows // group
    rc = min(ATTN_SOFTMAX_ROWS, group)
    c2 = scale * LOG2_E

    def scores(g):
        q = q_ref[g * hg:(g + 1) * hg].reshape(group, q_ref.shape[-1])
        s_scr[g * group:(g + 1) * group, :] = _dot_nt(q, k_ref[...])

    def softmax(g, masked):
        for r0 in range(g * group, (g + 1) * group, rc):
            sl = slice(r0, r0 + rc)
            s = s_scr[sl, :]
            if masked:
                qpos = qi * tq + lax.rem(r0 + lax.broadcasted_iota(jnp.int32, (rc, tk), 0), tq)
                kpos = kj * tk + lax.broadcasted_iota(jnp.int32, (rc, tk), 1)
                s = jnp.where(kpos <= qpos, s, -jnp.inf)
            m_old = m_scr[sl, :]
            m_new = jnp.maximum(m_old, jnp.max(s, axis=-1, keepdims=True))
            alpha = jnp.exp2((m_old - m_new) * c2)
            p = jnp.exp2((s - m_new) * c2)
            l_scr[sl, :] = alpha * l_scr[sl, :] + jnp.sum(p, axis=-1, keepdims=True)
            m_scr[sl, :] = m_new
            a_scr[sl, :] = alpha
            p_scr[sl, :] = p.astype(BF)

    def values(g):
        gs = slice(g * group, (g + 1) * group)
        acc_scr[gs, :] = a_scr[gs, :] * acc_scr[gs, :] + _dot(p_scr[gs, :], k_ref[:, :R])

    def block(masked):
        for g in range(n_groups + 2):
            if g < n_groups:
                scores(g)
            if 1 <= g <= n_groups:
                softmax(g - 1, masked)
            if g >= 2:
                values(g - 2)

    crosses_diagonal = kj * tk + tk - 1 > qi * tq

    @pl.when((kj * tk <= qi * tq + tq - 1) & crosses_diagonal)
    def _():
        block(True)

    @pl.when(jnp.logical_not(crosses_diagonal))
    def _():
        block(False)

    @pl.when(kj == pl.num_programs(2) - 1)
    def _():
        o = acc_scr[...] / l_scr[...]
        o_ref[...] = o.reshape(H, tq, R).astype(o_ref.dtype)


def attention_prompt(q4, kcat, *, R, scale, name):
    B, H, L, W = q4.shape
    tq = _pick(L, (128, 64, 32, 16))
    tk = _pick(L, (512, 256, 128, 64, 32, 16))
    nq, nk = L // tq, L // tk

    def k_index(b, i, j):
        return (b, jnp.minimum(j, (i * tq + tq - 1) // tk), 0)

    return pl.pallas_call(
        functools.partial(_attn_prompt_kernel, tq=tq, tk=tk, H=H, R=R, scale=scale),
        grid=(B, nq, nk),
        in_specs=[pl.BlockSpec((None, H, tq, W), lambda b, i, j: (b, 0, i, 0)),
                  pl.BlockSpec((None, tk, W), k_index)],
        out_specs=pl.BlockSpec((None, H, tq, R), lambda b, i, j: (b, 0, i, 0)),
        out_shape=jax.ShapeDtypeStruct((B, H, L, R), BF),
        scratch_shapes=[pltpu.VMEM((H * tq, 1), F32), pltpu.VMEM((H * tq, 1), F32), pltpu.VMEM((H * tq, 1), F32),
                        pltpu.VMEM((H * tq, R), F32), pltpu.VMEM((H * tq, tk), F32), pltpu.VMEM((H * tq, tk), BF)],
        compiler_params=_params(3),
        name=name,
    )(q4, kcat)


def _attn_sample_kernel(pt_ref, q_ref, knew_ref, ckv_hbm, kpe_hbm, o_ref, m_scr, l_scr, acc_scr,
                        ckv_buf, kpe_buf, sems, *, NP, T, H, R, P, scale):
    b = pl.program_id(0)
    j = pl.program_id(1)
    nj = pl.num_programs(1)
    step = b * nj + j
    n_steps = pl.num_programs(0) * nj
    n_slots = ckv_buf.shape[0]
    slot = lax.rem(step, n_slots)
    rows = T * H

    def page_copies(st):
        bb, jj, sl = lax.div(st, nj), lax.rem(st, nj), lax.rem(st, n_slots)
        copies = []
        for i in range(NP):
            pid = pt_ref[bb, jj * NP + i]
            copies.append(pltpu.make_async_copy(ckv_hbm.at[pid], ckv_buf.at[sl, i], sems.at[0, sl]))
            copies.append(pltpu.make_async_copy(kpe_hbm.at[pid], kpe_buf.at[sl, i], sems.at[1, sl]))
        return copies

    @pl.when(step == 0)
    def _():
        for ahead in range(n_slots - 1):
            @pl.when(ahead < n_steps)
            def _():
                for cp in page_copies(step + ahead):
                    cp.start()

    @pl.when(step + n_slots - 1 < n_steps)
    def _():
        for cp in page_copies(step + n_slots - 1):
            cp.start()

    for cp in page_copies(step):
        cp.wait()
    q = q_ref[...]
    q_lat = q[:, :R]
    q_pe = q[:, R:R + P]

    c2 = scale * LOG2_E

    def update(state, s, values):
        m_old, l_old, acc = state
        m_new = jnp.maximum(m_old, jnp.max(s, axis=-1, keepdims=True))
        alpha = jnp.exp2((m_old - m_new) * c2)
        p = jnp.exp2((s - m_new) * c2)
        return m_new, alpha * l_old + jnp.sum(p, axis=-1, keepdims=True), alpha * acc + values(p.astype(BF))

    @pl.when(j == 0)
    def _():
        tp = -(-T // BF16_ROWS) * BF16_ROWS
        knew = jnp.concatenate([knew_ref[...], jnp.zeros((tp - T, knew_ref.shape[1]), BF)], axis=0)
        s = _dot_nt(q, knew)
        qt = lax.broadcasted_iota(jnp.int32, (rows, tp), 0) // H
        kt = lax.broadcasted_iota(jnp.int32, (rows, tp), 1)
        s = jnp.where(kt <= qt, s, -jnp.inf)
        init = (jnp.full((rows, 1), -jnp.inf, F32), jnp.zeros((rows, 1), F32), jnp.zeros((rows, R), F32))
        m_scr[...], l_scr[...], acc_scr[...] = update(init, s, lambda p: _dot(p, knew[:, :R]))

    pg = min(ATTN_PAGE_GROUP, NP)
    n_groups = NP // pg

    def load(g):
        return [ckv_buf[slot, i].astype(BF) for i in range(g * pg, (g + 1) * pg)]

    def scores(g, pages):
        return jnp.concatenate([_dot_nt(q_lat, pages[i]) + _dot(q_pe, kpe_buf[slot, g * pg + i].astype(BF))
                                for i in range(pg)], axis=1)

    def values_of(pages):
        def values(p):
            page = pages[0].shape[0]
            out = _dot(p[:, :page], pages[0])
            for i in range(1, pg):
                out = out + _dot(p[:, i * page:(i + 1) * page], pages[i])
            return out
        return values

    state = (m_scr[...], l_scr[...], acc_scr[...])
    pages = load(0)
    s = scores(0, pages)
    for g in range(n_groups):
        if g + 1 < n_groups:
            pages_next = load(g + 1)
            s_next = scores(g + 1, pages_next)
        state = update(state, s, values_of(pages))
        if g + 1 < n_groups:
            pages, s = pages_next, s_next
    m_scr[...], l_scr[...], acc_scr[...] = state

    @pl.when(j == pl.num_programs(1) - 1)
    def _():
        o_ref[...] = (acc_scr[...] / l_scr[...]).astype(o_ref.dtype)


def attention_sample(q3, knew, cache_ckv, cache_kpe_t, page_table, *, T, H, R, P, scale, name):
    Bd, rows, W = q3.shape
    n_pages = page_table.shape[1]
    page = cache_ckv.shape[1]
    NP = _pick(n_pages, (ATTN_PAGES_PER_STEP, 16, 8, 4, 2, 1))
    any_space = pl.BlockSpec(memory_space=pl.ANY)
    grid_spec = pltpu.PrefetchScalarGridSpec(
        num_scalar_prefetch=1,
        grid=(Bd, n_pages // NP),
        in_specs=[pl.BlockSpec((None, rows, W), lambda b, j, pt: (b, 0, 0)),
                  pl.BlockSpec((None, T, W), lambda b, j, pt: (b, 0, 0)),
                  any_space, any_space],
        out_specs=pl.BlockSpec((None, rows, R), lambda b, j, pt: (b, 0, 0)),
        scratch_shapes=[pltpu.VMEM((rows, 1), F32), pltpu.VMEM((rows, 1), F32), pltpu.VMEM((rows, R), F32),
                        pltpu.VMEM((ATTN_PAGE_SLOTS, NP, page, R), F32),
                        pltpu.VMEM((ATTN_PAGE_SLOTS, NP, P, page), F32),
                        pltpu.SemaphoreType.DMA((2, ATTN_PAGE_SLOTS))],
    )
    return pl.pallas_call(
        functools.partial(_attn_sample_kernel, NP=NP, T=T, H=H, R=R, P=P, scale=scale),
        grid_spec=grid_spec,
        out_shape=jax.ShapeDtypeStruct((Bd, rows, R), BF),
        compiler_params=_params(2),
        name=name,
    )(page_table, q3, knew, cache_ckv, cache_kpe_t)


def _value_up_kernel(a_ref, w_ref, o_ref, *, H, R, V, head_major):
    for h in range(H):
        a = a_ref[h] if head_major else a_ref[:, h * R:(h + 1) * R]
        o_ref[:, h * V:(h + 1) * V] = _dot(a, w_ref[:, h * V:(h + 1) * V].astype(BF)).astype(o_ref.dtype)


def value_up(o_lat, w_uv, *, H, R, V, head_major, name):
    if head_major:
        G, _, L, _ = o_lat.shape
        M = G * L
        tm = _pick(L, (512, 256, 128, 64, 32, 16, 8))
        tiles = L // tm
        a_spec = pl.BlockSpec((None, H, tm, R), lambda i: (i // tiles, 0, i % tiles, 0))
    else:
        M = o_lat.shape[0]
        tm = _pick(M, (512, 256, 128, 64, 32, 16, 8))
        a_spec = pl.BlockSpec((tm, H * R), lambda i: (i, 0))
    return pl.pallas_call(
        functools.partial(_value_up_kernel, H=H, R=R, V=V, head_major=head_major),
        grid=(M // tm,),
        in_specs=[a_spec, pl.BlockSpec((R, H * V), lambda i: (0, 0))],
        out_specs=pl.BlockSpec((tm, H * V), lambda i: (i, 0)),
        out_shape=jax.ShapeDtypeStruct((M, H * V), BF),
        compiler_params=_params(1),
        name=name,
    )(o_lat, w_uv)


def _rope_tables(pos, P, reps):
    half = P // 2
    inv = ROPE_THETA ** (-jnp.arange(half, dtype=F32) / half)
    ang = pos.astype(F32)[:, None] * inv[None, :]
    cos, sin = jnp.cos(ang), jnp.sin(ang)
    zeros = jnp.zeros((pos.shape[0], LANES - P), F32)
    cos_t = jnp.concatenate([cos, cos, zeros], axis=1)
    sin_t = jnp.concatenate([-sin, sin, zeros], axis=1)
    return jnp.tile(cos_t, (reps, 1)), jnp.tile(sin_t, (reps, 1))


def _trunk(x3, mods, pos, conv_in, ssm_in, past, p, tag):
    G, L, D = x3.shape
    M = G * L
    grouped = L % SUBLANES == 0
    x = x3.reshape(M, D)

    def split_mods(m, n):
        arr = m[:, None, :] if grouped else jnp.repeat(m, L, axis=0)
        return [(arr, k) for k in range(n)]

    HV, DK, DV = p['HV'], p['DK'], p['DV']
    HQK = p['HQK']
    KEY, VAL = HQK * DK, HV * DV
    H, NOPE, P, R, V = p['H'], p['NOPE'], p['P'], p['R'], p['V']
    FF = p['w_down'].shape[1]
    cos_t, sin_t = _rope_tables(pos, P, G)
    scale = (NOPE + P) ** -0.5

    sh1, sc1, gt1, sh2, sc2, gt2 = split_mods(mods['l0'], 6)
    w_in_t = p['gdn_w_in_t']
    n_proj = 2 * KEY + 2 * VAL
    fused_front = conv_in is None and grouped and L % _row_tile(M, L, True) == 0
    if fused_front:
        proj, tail = gdn_front(x, p['g_mix'][0], sh1, sc1, w_in_t, p['gdn_w_conv'][0], L, KEY=KEY, VAL=VAL, DK=DK,
                               name=f'{tag}_gdn_in')
    else:
        proj = mod_matmul(x, p['g_mix'][0], sh1, sc1, w_in_t, 0, n_proj, L, w_is_nk=True, name=f'{tag}_gdn_in')
    ba = mod_matmul(x, p['g_mix'][0], sh1, sc1, p['gdn_w_ba_t'], 0, LANES, L, w_is_nk=True, name=f'{tag}_gdn_ba')
    proj3 = proj.reshape(G, L, n_proj)
    o_g, ssm_new = gdn_scan(
        proj3, ba[:, :HV], ba[:, HV:2 * HV], conv_in, ssm_in, p['gdn_w_conv'][0], p['gdn_a_log'][0],
        p['gdn_dt_bias'][0], p['gdn_g_norm'][0], B=G, L=L, HQK=HQK, HV=HV, DK=DK, DV=DV, activated=fused_front,
        name=f'{tag}_gdn_scan')
    if fused_front:
        conv_src = tail.reshape(G, -1, SUBLANES, n_proj)[:, -1]
    else:
        conv_src = proj3
    conv_new = conv_src[:, conv_src.shape[1] - (CONV_W - 1):, :2 * KEY + VAL]
    x = matmul_residual(o_g.reshape(M, VAL), p['gdn_w_out'], 0, x, gt1, L, name=f'{tag}_gdn_out')
    hff = mod_matmul(x, p['g_ffn'][0], sh2, sc2, p['w_gate_up'], 0, FF, L, swiglu=True, out_dtype=BF,
                     name=f'{tag}_ffn0_up')
    x = matmul_residual(hff, p['w_down'], 0, x, gt2, L, name=f'{tag}_ffn0_down')

    shk, sck = split_mods(mods['kv'], 2)
    ckv, kpe, kcat = shared_kv(x, p['kv_g_in'], shk, sck, p['kv_w_ext'], p['kv_g_norm'], cos_t, sin_t, L,
                               R=R, P=P, name=f'{tag}_kv')

    sh1, sc1, gt1, sh2, sc2, gt2 = split_mods(mods['l1'], 6)
    head_major = past is None
    q = mla_queries(x, p['g_mix'][1], sh1, sc1, p['mla_w_dq'][0], p['mla_g_q'][0], p['mla_w_uq_r'], p['kv_w_uk'],
                    cos_t, sin_t, L, H=H, NOPE=NOPE, P=P, R=R, head_major=head_major, name=f'{tag}_q')
    if head_major:
        o_lat = attention_prompt(q, kcat.reshape(G, L, R + LANES), R=R, scale=scale, name=f'{tag}_attn')
    else:
        cache_ckv, cache_kpe, page_table = past
        o_lat = attention_sample(q.reshape(G, L * H, R + LANES), kcat.reshape(G, L, R + LANES), cache_ckv,
                                 cache_kpe, page_table, T=L, H=H, R=R, P=P, scale=scale, name=f'{tag}_attn')
        o_lat = o_lat.reshape(M, H * R)
    o = value_up(o_lat, p['kv_w_uv'], H=H, R=R, V=V, head_major=head_major, name=f'{tag}_uv')
    x = matmul_residual(o, p['mla_w_o'], 0, x, gt1, L, name=f'{tag}_attn_out')
    hff = mod_matmul(x, p['g_ffn'][1], sh2, sc2, p['w_gate_up'], 1, FF, L, swiglu=True, out_dtype=BF,
                     name=f'{tag}_ffn1_up')
    x = matmul_residual(hff, p['w_down'], 1, x, gt2, L, name=f'{tag}_ffn1_down')

    shf, scf = split_mods(mods['final'], 2)
    y = modulate_rows(x, p['final_g'], shf, scf, L, name=f'{tag}_final')
    return (y.reshape(G, L, D), conv_new[None], ssm_new[None], ckv.reshape(G, L, R), kpe.reshape(G, L, P))


def kernel(x_prompt, x_sample, c_prompt, c_sample, cache_ckv, cache_kpe, page_table, state_ssm, state_conv, w_ada, b_ada, g_mix, g_ffn, w_gate_up, w_down, gdn_w_in, gdn_w_conv, gdn_a_log, gdn_dt_bias, gdn_g_norm, gdn_w_out, kv_w_ada, kv_b_ada, kv_g_in, kv_w_down, kv_g_norm, kv_w_uk, kv_w_uv, mla_w_dq, mla_g_q, mla_w_uq, mla_w_o, final_w_ada, final_b_ada, final_g):
    B, S, D = x_prompt.shape
    Bd, T, _ = x_sample.shape
    HV, DK, DV = state_ssm.shape[2:]
    KEY = (state_conv.shape[-1] - HV * DV) // 2
    R = cache_ckv.shape[-1]
    P = cache_kpe.shape[-1]
    QL = mla_w_dq.shape[-1]
    nope_total = kv_w_uk.shape[1]
    H = (mla_w_uq.shape[-1] - nope_total) // P
    NOPE = nope_total // H
    w_in_t = jnp.swapaxes(gdn_w_in, 1, 2)
    n_proj = 2 * KEY + 2 * HV * DV
    w_ba_t = jnp.concatenate([w_in_t[0, n_proj:], jnp.zeros((LANES - 2 * HV, D), F32)], axis=0)
    p = dict(w_ada=w_ada, b_ada=b_ada, g_mix=g_mix, g_ffn=g_ffn, w_gate_up=w_gate_up, w_down=w_down,
             gdn_w_in_t=w_in_t, gdn_w_ba_t=w_ba_t, gdn_w_conv=gdn_w_conv, gdn_a_log=gdn_a_log,
             gdn_dt_bias=gdn_dt_bias, gdn_g_norm=gdn_g_norm, gdn_w_out=gdn_w_out, kv_g_in=kv_g_in,
             kv_w_ext=kv_down_weight(kv_w_down, R), kv_g_norm=kv_g_norm, kv_w_uk=kv_w_uk, kv_w_uv=kv_w_uv,
             mla_w_dq=mla_w_dq, mla_g_q=mla_g_q, mla_w_uq_r=query_up_weight(mla_w_uq[0], H, NOPE),
             mla_w_o=mla_w_o, final_g=final_g,
             HV=HV, DK=DK, DV=DV, HQK=KEY // DK, H=H, NOPE=NOPE, P=P, R=R, V=kv_w_uv.shape[1] // H)

    n_c = B + Bd
    pad = -n_c % SUBLANES
    c_all = jnp.concatenate([c_prompt, c_sample, jnp.zeros((pad, D), F32)], axis=0)
    m_l0 = ada_dense(c_all, w_ada, b_ada, 0, name='ada_l0')
    m_l1 = ada_dense(c_all, w_ada, b_ada, 1, name='ada_l1')
    m_kv = ada_dense(c_all, kv_w_ada, kv_b_ada, 0, name='ada_kv')
    m_f = ada_dense(c_all, final_w_ada, final_b_ada, 0, name='ada_final')

    def mods(lo, hi):
        return dict(l0=m_l0[lo:hi], l1=m_l1[lo:hi], kv=m_kv[lo:hi], final=m_f[lo:hi])

    y_p, conv_p, ssm_p, ckv_p, kpe_p = _trunk(x_prompt, mods(0, B), jnp.arange(S), None, None, None, p, 'p')
    past_len = page_table.shape[1] * cache_ckv.shape[1]
    y_s, conv_s, ssm_s, ckv_s, kpe_s = _trunk(x_sample, mods(B, n_c), past_len + jnp.arange(T), state_conv[0],
                                              state_ssm[0], (cache_ckv, jnp.swapaxes(cache_kpe, 1, 2), page_table),
                                              p, 's')
    return (y_p, y_s, ssm_p, conv_p, ckv_p, kpe_p, ssm_s, conv_s, ckv_s, kpe_s)
```

```python
import functools

import jax
import jax.numpy as jnp
from jax import lax
from jax.experimental import pallas as pl
from jax.experimental.pallas import tpu as pltpu

EPS = 1e-6
ROPE_THETA = 10000.0
CONV_W = 4
GDN_CHUNK = 64
F32 = jnp.float32
BF = jnp.bfloat16

V7X_VMEM_BYTES = 64 * 1024 * 1024
VMEM_LIMIT = V7X_VMEM_BYTES - 8 * 1024 * 1024
VMEM_TILE_BUDGET = (VMEM_LIMIT * 9) // 10
LANES = 128
SUBLANES = 8
BF16_ROWS = 16
TRI_BLOCK = 16
STACK_ROWS = 128
MAX_STACK_HEADS = 8
SEQS_PER_STEP = 4
STACKS_PER_STEP = 16
FRONT_GROUP_ROWS = 256
ATTN_GROUP_ROWS = 256
ATTN_SOFTMAX_ROWS = 64
ATTN_PAGES_PER_STEP = 64
ATTN_PAGE_GROUP = 4
ATTN_PAGE_SLOTS = 2
Q_HEADS_PER_STEP = 4
LOG2_E = 1.4426950408889634


def _params(n_axes):
    return pltpu.CompilerParams(dimension_semantics=("arbitrary",) * n_axes, vmem_limit_bytes=VMEM_LIMIT)


def _pick(n, cands):
    for c in cands:
        if n % c == 0:
            return c
    return n


def _silu(x):
    return x * jax.nn.sigmoid(x)


def _contract(a, b, ca, cb):
    batch = tuple(range(a.ndim - 2))
    dims = (((a.ndim - 2 + ca,), (b.ndim - 2 + cb,)), (batch, batch))
    return lax.dot_general(a, b, dims, preferred_element_type=F32)


def _dot(a, b):
    return _contract(a, b, 1, 0)


def _dot_nt(a, b):
    return _contract(a, b, 1, 1)


def _dot_tn(a, b):
    return _contract(a, b, 0, 0)


def _split_bf16(a):
    hi = a.astype(BF)
    lo = (a - hi.astype(F32)).astype(BF)
    return hi, lo


def _dot3(a, b):
    ah, al = _split_bf16(a)
    bh, bl = _split_bf16(b)
    return _dot(ah, bh) + _dot(ah, bl) + _dot(al, bh)


def _modulate_rows(x_ref, g_ref, sh_ref, sc_ref, h_ref, rows):
    tm = x_ref.shape[0]
    per_row = sh_ref.shape[0] != 1

    def body(r, carry):
        sl = pl.ds(pl.multiple_of(r * rows, rows), rows)
        x = x_ref[sl, :]
        y = x * lax.rsqrt(jnp.mean(x * x, axis=-1, keepdims=True) + EPS) * g_ref[...]
        sc = sc_ref[sl, :] if per_row else sc_ref[...]
        sh = sh_ref[sl, :] if per_row else sh_ref[...]
        h_ref[sl, :] = (y * (1.0 + sc) + sh).astype(h_ref.dtype)
        return carry

    lax.fori_loop(0, tm // rows, body, 0)


def _grouped(mod):
    return mod[0].ndim == 3


def _mod_spec(mod, K, tm, rows_per_group):
    arr, k = mod
    if arr.ndim == 3:
        tiles_per_group = rows_per_group // tm
        return pl.BlockSpec((None, 1, K), lambda i, *_: (i // tiles_per_group, 0, k))
    return pl.BlockSpec((tm, K), lambda i, *_: (i, k))


def _w_spec(w, layer, K, tn, col_off_blocks=0):
    if w.ndim == 3:
        return pl.BlockSpec((None, K, tn), lambda i, j: (layer, 0, j + col_off_blocks))
    return pl.BlockSpec((K, tn), lambda i, j: (0, j + col_off_blocks))


def _modmm_kernel(x_ref, g_ref, sh_ref, sc_ref, w_ref, o_ref, h_ref, *, rows, w_is_nk):
    @pl.when(pl.program_id(1) == 0)
    def _():
        _modulate_rows(x_ref, g_ref, sh_ref, sc_ref, h_ref, rows)

    dot = _dot_nt if w_is_nk else _dot
    o_ref[...] = dot(h_ref[...], w_ref[...].astype(BF)).astype(o_ref.dtype)


def _modmm_swiglu_kernel(x_ref, g_ref, sh_ref, sc_ref, wg_ref, wu_ref, o_ref, h_ref, *, rows):
    @pl.when(pl.program_id(1) == 0)
    def _():
        _modulate_rows(x_ref, g_ref, sh_ref, sc_ref, h_ref, rows)

    h = h_ref[...]
    gate = _dot(h, wg_ref[...].astype(BF))
    up = _dot(h, wu_ref[...].astype(BF))
    o_ref[...] = (_silu(gate) * up).astype(o_ref.dtype)


def _gdn_front_kernel(x_ref, g_ref, sh_ref, sc_ref, w_ref, wba_ref, cw_ref, o_ref, tail_ref, ba_ref, h_ref, cb,
                      halo, *, rows, tiles_per_seq, n_qk, n_conv, DK):
    i = pl.program_id(0)
    j = pl.program_id(1)
    tm, tn = o_ref.shape
    tail_lo = SUBLANES - (CONV_W - 1)
    group = min(FRONT_GROUP_ROWS, tm)
    n_groups = tm // group
    rc = min(ATTN_SOFTMAX_ROWS, group)

    @pl.when(j == 0)
    def _():
        _modulate_rows(x_ref, g_ref, sh_ref, sc_ref, h_ref, rows)
        ba_ref[...] = _dot_nt(h_ref[...], wba_ref[...].astype(BF))

    def project(g, w_bf):
        r0 = g * group
        cb[SUBLANES + r0:SUBLANES + r0 + group, :] = _dot_nt(h_ref[r0:r0 + group, :], w_bf)

    def activate(g, kind):
        for r0 in range(g * group, (g + 1) * group, rc):
            if kind == 'z':
                act = _silu(cb[SUBLANES + r0:SUBLANES + r0 + rc, :])
            else:
                acc = cb[tail_lo + r0:tail_lo + r0 + rc, :] * cw_ref[0:1, :]
                for t in range(1, CONV_W):
                    acc = acc + cb[tail_lo + r0 + t:tail_lo + r0 + t + rc, :] * cw_ref[t:t + 1, :]
                act = _silu(acc)
                if kind in ('q', 'k'):
                    heads = []
                    for hh in range(tn // DK):
                        a = act[:, hh * DK:(hh + 1) * DK]
                        f = lax.rsqrt(jnp.sum(a * a, axis=-1, keepdims=True) + EPS)
                        heads.append(a * (f * (DK ** -0.5) if kind == 'q' else f))
                    act = jnp.concatenate(heads, axis=1)
            o_ref[r0:r0 + rc, :] = act

    def tile(kind):
        if kind != 'z':
            first = lax.rem(i, tiles_per_seq) == 0

            @pl.when(first)
            def _():
                cb[0:SUBLANES, :] = jnp.zeros((SUBLANES, tn), F32)

            @pl.when(jnp.logical_not(first))
            def _():
                cb[0:SUBLANES, :] = halo[j]

        w_bf = w_ref[...].astype(BF)
        for g in range(n_groups + 1):
            if g < n_groups:
                project(g, w_bf)
            if g >= 1:
                activate(g - 1, kind)
        last_rows = cb[tm:tm + SUBLANES, :]
        tail_ref[...] = last_rows
        if kind != 'z':
            halo[j] = last_rows

    n_q = n_qk // 2
    pl.when(j < n_q)(lambda: tile('q'))
    pl.when((j >= n_q) & (j < n_qk))(lambda: tile('k'))
    pl.when((j >= n_qk) & (j < n_conv))(lambda: tile('v'))
    pl.when(j >= n_conv)(lambda: tile('z'))


def gdn_front(x, g, shift, scale, w_t, w_ba_t, w_conv, rows_per_group, *, KEY, VAL, DK, name):
    M, K = x.shape
    n_out = 2 * KEY + 2 * VAL
    tm = _row_tile(M, rows_per_group, True)
    tn = _pick(KEY, (512, 256, 128))
    assert _grouped(shift) and rows_per_group % tm == 0 and VAL % tn == 0 and tn % DK == 0
    tiles_per_seq = rows_per_group // tm
    n_qk, n_conv = 2 * KEY // tn, (2 * KEY + VAL) // tn
    rows = _pick(tm, (128, 64, 32, 16, 8))
    mod_specs = [_mod_spec(m, K, tm, rows_per_group) for m in (shift, scale)]
    body = functools.partial(_gdn_front_kernel, rows=rows, tiles_per_seq=tiles_per_seq, n_qk=n_qk,
                             n_conv=n_conv, DK=DK)
    return pl.pallas_call(
        body,
        grid=(M // tm, n_out // tn),
        in_specs=[pl.BlockSpec((tm, K), lambda i, j: (i, 0)),
                  pl.BlockSpec((1, K), lambda i, j: (0, 0)),
                  *mod_specs,
                  pl.BlockSpec((None, tn, K), lambda i, j: (0, j, 0)),
                  pl.BlockSpec((LANES, K), lambda i, j: (0, 0)),
                  pl.BlockSpec((CONV_W, tn), lambda i, j: (0, jnp.minimum(j, n_conv - 1)))],
        out_specs=[pl.BlockSpec((tm, tn), lambda i, j: (i, j)),
                   pl.BlockSpec((None, SUBLANES, tn), lambda i, j: (i, 0, j)),
                   pl.BlockSpec((tm, LANES), lambda i, j: (i, 0))],
        out_shape=[jax.ShapeDtypeStruct((M, n_out), F32),
                   jax.ShapeDtypeStruct((M // tm, SUBLANES, n_out), F32),
                   jax.ShapeDtypeStruct((M, LANES), F32)],
        scratch_shapes=[pltpu.VMEM((tm, K), BF), pltpu.VMEM((SUBLANES + tm, tn), F32),
                        pltpu.VMEM((n_conv, SUBLANES, tn), F32)],
        compiler_params=_params(2),
        name=name,
    )(x, g.reshape(1, K), shift[0], scale[0], w_t, w_ba_t, w_conv)


def _row_tile(M, rows_per_group, grouped):
    base = rows_per_group if grouped else M
    return _pick(base, (1024, 512, 256, 128, 64, 32, 16, 8))


def _col_tile(N, fixed_bytes, bytes_per_col):
    for tn in (1024, 512, 256):
        if N % tn == 0 and fixed_bytes + bytes_per_col * tn <= VMEM_TILE_BUDGET:
            return tn
    return _pick(N, (128,))


def _weight_col_bytes(K, n_weights=1):
    return n_weights * K * (2 * 4 + 2)


def mod_matmul(x, g, shift, scale, w, layer, n_out, rows_per_group, *, swiglu=False, w_is_nk=False,
               out_dtype=F32, name):
    M, K = x.shape
    tm = _row_tile(M, rows_per_group, _grouped(shift))
    n_w = 2 if swiglu else 1
    out_bytes = jnp.dtype(out_dtype).itemsize
    tn = _col_tile(n_out, tm * K * (2 * 4 + 2),
                   _weight_col_bytes(K, n_w) + tm * (2 * out_bytes + 4 * n_w))
    rows = _pick(tm, (128, 64, 32, 16, 8))
    mod_specs = [_mod_spec(m, K, tm, rows_per_group) for m in (shift, scale)]
    if not w_is_nk:
        w_spec = _w_spec(w, layer, K, tn)
    elif w.ndim == 3:
        w_spec = pl.BlockSpec((None, tn, K), lambda i, j: (layer, j, 0))
    else:
        w_spec = pl.BlockSpec((tn, K), lambda i, j: (j, 0))
    in_specs = [pl.BlockSpec((tm, K), lambda i, j: (i, 0)),
                pl.BlockSpec((1, K), lambda i, j: (0, 0)),
                *mod_specs, w_spec]
    args = [x, g.reshape(1, K), shift[0], scale[0], w]
    if swiglu:
        in_specs.append(_w_spec(w, layer, K, tn, n_out // tn))
        args.append(w)
        body = functools.partial(_modmm_swiglu_kernel, rows=rows)
    else:
        body = functools.partial(_modmm_kernel, rows=rows, w_is_nk=w_is_nk)
    return pl.pallas_call(
        body,
        grid=(M // tm, n_out // tn),
        in_specs=in_specs,
        out_specs=pl.BlockSpec((tm, tn), lambda i, j: (i, j)),
        out_shape=jax.ShapeDtypeStruct((M, n_out), out_dtype),
        scratch_shapes=[pltpu.VMEM((tm, K), BF)],
        compiler_params=_params(2),
        name=name,
    )(*args)


def _mmres_kernel(a_ref, w_ref, res_ref, gate_ref, o_ref):
    y = _dot(a_ref[...], w_ref[...].astype(BF))
    o_ref[...] = res_ref[...] + gate_ref[...] * y


def matmul_residual(a, w, layer, res, gate, rows_per_group, *, name):
    M, K = a.shape
    N = res.shape[1]
    tm = _row_tile(M, rows_per_group, _grouped(gate))
    tn = _col_tile(N, tm * K * 2 * a.dtype.itemsize, _weight_col_bytes(K) + tm * (4 * 4 + 4))
    gate_arr, gate_k = gate
    gate_off = gate_k * (N // tn)
    if _grouped(gate):
        tiles_per_group = rows_per_group // tm
        gate_spec = pl.BlockSpec((None, 1, tn), lambda i, j: (i // tiles_per_group, 0, gate_off + j))
    else:
        gate_spec = pl.BlockSpec((tm, tn), lambda i, j: (i, gate_off + j))
    return pl.pallas_call(
        _mmres_kernel,
        grid=(M // tm, N // tn),
        in_specs=[pl.BlockSpec((tm, K), lambda i, j: (i, 0)),
                  _w_spec(w, layer, K, tn),
                  pl.BlockSpec((tm, tn), lambda i, j: (i, j)),
                  gate_spec],
        out_specs=pl.BlockSpec((tm, tn), lambda i, j: (i, j)),
        out_shape=jax.ShapeDtypeStruct((M, N), F32),
        compiler_params=_params(2),
        name=name,
    )(a, w, res, gate_arr)


def _ada_kernel(c_ref, w_ref, b_ref, o_ref):
    a = _silu(c_ref[...]).astype(BF)
    o_ref[...] = _dot(a, w_ref[...].astype(BF)) + b_ref[...]


def ada_dense(c, w, b, layer, *, name):
    M, K = c.shape
    N = w.shape[-1]
    tn = _col_tile(N, 2 * M * K * 4, _weight_col_bytes(K) + M * 3 * 4)
    if b.ndim == 2:
        b_spec = pl.BlockSpec((None, 1, tn), lambda i, j: (layer, 0, j))
        b = b.reshape(b.shape[0], 1, N)
    else:
        b_spec = pl.BlockSpec((1, tn), lambda i, j: (0, j))
        b = b.reshape(1, N)
    return pl.pallas_call(
        _ada_kernel,
        grid=(1, N // tn),
        in_specs=[pl.BlockSpec((M, K), lambda i, j: (0, 0)), _w_spec(w, layer, K, tn), b_spec],
        out_specs=pl.BlockSpec((M, tn), lambda i, j: (0, j)),
        out_shape=jax.ShapeDtypeStruct((M, N), F32),
        compiler_params=_params(2),
        name=name,
    )(c, w, b)


def _modulate_kernel(x_ref, g_ref, sh_ref, sc_ref, o_ref, *, rows):
    _modulate_rows(x_ref, g_ref, sh_ref, sc_ref, o_ref, rows)


def modulate_rows(x, g, shift, scale, rows_per_group, *, name):
    M, K = x.shape
    tm = _row_tile(M, rows_per_group, _grouped(shift))
    rows = _pick(tm, (128, 64, 32, 16, 8))
    mod_specs = [_mod_spec(m, K, tm, rows_per_group) for m in (shift, scale)]
    return pl.pallas_call(
        functools.partial(_modulate_kernel, rows=rows),
        grid=(M // tm,),
        in_specs=[pl.BlockSpec((tm, K), lambda i: (i, 0)), pl.BlockSpec((1, K), lambda i: (0, 0)),
                  *mod_specs],
        out_specs=pl.BlockSpec((tm, K), lambda i: (i, 0)),
        out_shape=jax.ShapeDtypeStruct((M, K), F32),
        compiler_params=_params(1),
        name=name,
    )(x, g.reshape(1, K), shift[0], scale[0])


def _dot1(a, b):
    return _dot(a.astype(BF), b.astype(BF))


def _tri_inverse(n_low, ii, jj, cp, nil):
    rn = n_low.shape[-1]
    eye = (ii == jj).astype(F32)
    base = min(TRI_BLOCK, cp)
    shift = base.bit_length() - 1
    nd = jnp.where((ii >> shift) == (jj >> shift), n_low, 0.0)
    p = eye - nd
    if nil > 2:
        npow = _dot1(nd, nd)
        pw = 2
        while 2 * pw - 1 < nil - 1:
            both = _dot1(jnp.concatenate([p, npow], axis=-2), npow)
            p = p + both[:, :rn]
            npow = both[:, rn:]
            pw *= 2
        p = p + _dot1(p, npow)
    size = base
    while size < cp:
        s = size.bit_length() - 1
        off = ((ii >> (s + 1)) == (jj >> (s + 1))) & (((ii >> s) & 1) == 1) & (((jj >> s) & 1) == 0)
        x = _dot1(jnp.where(off, n_low, 0.0), p)
        p = p - _dot1(p, x)
        size *= 2
    resid = (eye - p) - _dot3(n_low, p)
    return p + _dot1(p, resid)


def _delta_chunk(q_st, k_st, v_st, beta_row, g_row, load_state, store_state, *, G, Cp, nil):
    S, rn, dv = v_st.shape
    ii = lax.broadcasted_iota(jnp.int32, (1, rn, rn), 1)
    jj = lax.broadcasted_iota(jnp.int32, (1, rn, rn), 2)
    sh = Cp.bit_length() - 1
    same = (ii >> sh) == (jj >> sh)
    eye = ii == jj
    causal = same & (jj <= ii)
    g_mat = jnp.broadcast_to(g_row, (S, rn, rn))
    g_col = jnp.sum(jnp.where(eye, g_mat, 0.0), axis=2, keepdims=True)
    gc_col = jnp.sum(jnp.where(causal, g_mat, 0.0), axis=2, keepdims=True)
    gl_col = jnp.sum(jnp.where(same, g_mat, 0.0), axis=2, keepdims=True)
    gc_row = jnp.sum(jnp.where(same & (ii <= jj), jnp.broadcast_to(g_col, (S, rn, rn)), 0.0), axis=1, keepdims=True)
    beta_col = jnp.sum(jnp.where(eye, jnp.broadcast_to(beta_row, (S, rn, rn)), 0.0), axis=2, keepdims=True)
    decay = jnp.exp(jnp.where(causal, gc_col - gc_row, -jnp.inf))
    eg = jnp.exp(gc_col)
    kb = k_st * beta_col
    both = _dot_nt(jnp.concatenate([kb, q_st], axis=1).astype(BF), k_st.astype(BF))
    n_low = jnp.where(same & (jj < ii), both[:, :rn] * decay, 0.0)
    t_inv = _tri_inverse(n_low, ii, jj, Cp, nil).astype(BF)
    uw = _dot(t_inv, jnp.concatenate([v_st * beta_col, kb * eg], axis=2).astype(BF))
    u, w = uw[:, :, :dv], uw[:, :, dv:]
    qe = q_st * eg
    k_dec = k_st * jnp.exp(gl_col - gc_col)
    pad = jnp.zeros((S, -Cp % BF16_ROWS, k_dec.shape[2]), F32)

    def bf16_rows(x):
        return (jnp.concatenate([x, pad], axis=1) if pad.shape[1] else x).astype(BF)

    v_new, q_s = [], []
    for h in range(G):
        r = slice(h * Cp, (h + 1) * Cp)
        s_old = jnp.stack([load_state(s * G + h) for s in range(S)], axis=0)
        ws_qs = _dot(jnp.concatenate([w[:, r], qe[:, r]], axis=1).astype(BF), s_old.astype(BF))
        v_new_h = u[:, r] - ws_qs[:, :Cp]
        q_s.append(ws_qs[:, Cp:])
        v_new.append(v_new_h)
        s_new = (s_old * jnp.exp(gl_col[:, h * Cp:h * Cp + 1])
                 + _dot_tn(bf16_rows(k_dec[:, r]), bf16_rows(v_new_h)))
        for s in range(S):
            store_state(s * G + h, s_new[s])
    v_new = jnp.concatenate(v_new, axis=1).astype(BF)
    return jnp.concatenate(q_s, axis=1) + _dot((both[:, rn:] * decay).astype(BF), v_new)


def _gdn_kernel(*refs, C, Cp, G, NS, BB, rep, DK, DV, has_state, single_chunk, activated):
    q_ref, k_ref, v_ref, z_ref, b_ref, a_ref, alog_ref, dtb_ref, wq_ref, wk_ref, wv_ref, gn_ref = refs[:12]
    pos = 12
    if has_state:
        cq_ref, ck_ref, cv_ref, s0_ref = refs[pos:pos + 4]
        pos += 4
    o_ref, so_ref = refs[pos:pos + 2]
    s_scr, cbq, cbk, cbv = refs[pos + 2:]

    c = pl.program_id(2)
    last = pl.num_programs(2) - 1
    tail_lo = SUBLANES - (CONV_W - 1)
    direct_state = has_state and single_chunk

    @pl.when(c == 0)
    def _():
        for cb in (cbq, cbk, cbv):
            cb[...] = jnp.zeros_like(cb)
        if has_state:
            cbq[:, tail_lo:SUBLANES, :] = cq_ref[...]
            cbk[:, tail_lo:SUBLANES, :] = ck_ref[...]
            cbv[:, tail_lo:SUBLANES, :] = cv_ref[...]
            if not direct_state:
                for bb in range(BB):
                    s_scr[bb * G:(bb + 1) * G] = s0_ref[bb]
        else:
            s_scr[...] = jnp.zeros_like(s_scr)

    def conv(cb, x_ref, w_ref, bb):
        cb[bb, SUBLANES:SUBLANES + C, :] = x_ref[bb]
        acc = cb[bb, tail_lo:tail_lo + Cp, :] * w_ref[0:1, :]
        for j in range(1, CONV_W):
            acc = acc + cb[bb, tail_lo + j:tail_lo + j + Cp, :] * w_ref[j:j + 1, :]
        cb[bb, tail_lo:SUBLANES, :] = cb[bb, C + tail_lo:C + SUBLANES, :]
        return _silu(acc)

    padded = Cp != C
    if padded:
        row_ok = lax.broadcasted_iota(jnp.int32, (Cp, 1), 0) < C
        lane_ok = (lax.broadcasted_iota(jnp.int32, (1, G * Cp), 1) & (Cp - 1)) < C
    GS = G // NS
    rn = GS * Cp
    q_parts, k_parts, v_parts, beta_rows, g_rows = [], [], [], [], []
    for bb in range(BB):
        if activated:
            qc, kc, vc = q_ref[bb], k_ref[bb], v_ref[bb]
        else:
            qc = conv(cbq, q_ref, wq_ref, bb)
            kc = conv(cbk, k_ref, wk_ref, bb)
            vc = conv(cbv, v_ref, wv_ref, bb)
        for i in range(G // rep):
            qh = qc[:, i * DK:(i + 1) * DK]
            kh = kc[:, i * DK:(i + 1) * DK]
            if not activated:
                qh = qh * lax.rsqrt(jnp.sum(qh * qh, axis=-1, keepdims=True) + EPS) * (DK ** -0.5)
                kh = kh * lax.rsqrt(jnp.sum(kh * kh, axis=-1, keepdims=True) + EPS)
            if padded:
                kh = jnp.where(row_ok, kh, 0.0)
            for e in range(rep):
                hl = i * rep + e
                vh = vc[:, hl * DV:(hl + 1) * DV]
                if padded:
                    vh = jnp.where(row_ok, vh, 0.0)
                q_parts.append(qh)
                k_parts.append(kh)
                v_parts.append(vh)
        beta_row = jax.nn.sigmoid(b_ref[bb, c])
        sp_in = a_ref[bb, c] + dtb_ref[...]
        softplus = jnp.maximum(sp_in, 0.0) + jnp.log1p(jnp.exp(-jnp.abs(sp_in)))
        g_row = -jnp.exp(alog_ref[...]) * softplus
        if padded:
            beta_row = jnp.where(lane_ok, beta_row, 0.0)
            g_row = jnp.where(lane_ok, g_row, 0.0)
        beta_rows += [beta_row[:, s * rn:(s + 1) * rn] for s in range(NS)]
        g_rows += [g_row[:, s * rn:(s + 1) * rn] for s in range(NS)]

    def load_state(i):
        return s0_ref[i // G, i % G] if direct_state else s_scr[i]

    def store_state(i, value):
        if direct_state:
            so_ref[i // G, i % G] = value
        else:
            s_scr[i] = value

    def stacks(parts):
        return jnp.stack([jnp.concatenate(parts[s * GS:(s + 1) * GS], axis=0) for s in range(BB * NS)], axis=0)

    o = _delta_chunk(stacks(q_parts), stacks(k_parts), stacks(v_parts), jnp.stack(beta_rows, axis=0),
                     jnp.stack(g_rows, axis=0), load_state, store_state, G=GS, Cp=Cp, nil=min(C, TRI_BLOCK))
    on = o * lax.rsqrt(jnp.mean(o * o, axis=-1, keepdims=True) + EPS) * gn_ref[...]
    for bb in range(BB):
        for s in range(NS):
            for h in range(GS):
                hl = s * GS + h
                zg = z_ref[bb, :, hl * DV:(hl + 1) * DV]
                o_ref[bb, :, hl * DV:(hl + 1) * DV] = (
                    on[bb * NS + s, h * Cp:h * Cp + C] * (zg if activated else _silu(zg))).astype(o_ref.dtype)

    if not direct_state:
        @pl.when(c == last)
        def _():
            for bb in range(BB):
                so_ref[bb] = s_scr[bb * G:(bb + 1) * G]


def gdn_scan(proj, b_logit, a_logit, conv_prev, s0, w_conv, a_log, dt_bias, g_norm, *, B, L, HQK, HV, DK, DV,
             activated, name):
    rep = HV // HQK
    KEY, VAL = HQK * DK, HV * DV
    C = _pick(L, (GDN_CHUNK,))
    n = L // C
    Cp = -(-C // SUBLANES) * SUBLANES
    assert Cp & (Cp - 1) == 0 and L >= CONV_W - 1 and not (activated and Cp != C)
    GS = min(HV, max(rep, min(STACK_ROWS // Cp, MAX_STACK_HEADS)))
    NS = _pick(HV // GS, (STACKS_PER_STEP, 4, 2, 1))
    G = NS * GS
    assert HV % G == 0 and GS % rep == 0
    has_state = s0 is not None
    BB = _pick(B, (SEQS_PER_STEP, 1)) if n == 1 else 1
    qw, vw = (G // rep) * DK, G * DV
    k_off, v_off, z_off = KEY // qw, 2 * KEY // vw, (2 * KEY + VAL) // vw

    def stack_rows(t):
        t = t.reshape(B, n, C, HV // G, G)
        t = jnp.pad(t, ((0, 0), (0, 0), (0, Cp - C), (0, 0), (0, 0)))
        return jnp.transpose(t, (0, 3, 1, 4, 2)).reshape(B, HV // G, n, 1, G * Cp)

    def stack_heads(v):
        return jnp.repeat(v.reshape(HV // G, 1, G), Cp, axis=-1)

    gate_spec = pl.BlockSpec((BB, None, n, 1, G * Cp), lambda b, h, c: (b, h, 0, 0, 0))
    head_spec = pl.BlockSpec((None, 1, G * Cp), lambda b, h, c: (h, 0, 0))
    in_specs = [
        pl.BlockSpec((BB, C, qw), lambda b, h, c: (b, c, h)),
        pl.BlockSpec((BB, C, qw), lambda b, h, c: (b, c, k_off + h)),
        pl.BlockSpec((BB, C, vw), lambda b, h, c: (b, c, v_off + h)),
        pl.BlockSpec((BB, C, vw), lambda b, h, c: (b, c, z_off + h)),
        gate_spec, gate_spec, head_spec, head_spec,
        pl.BlockSpec((CONV_W, qw), lambda b, h, c: (0, h)),
        pl.BlockSpec((CONV_W, qw), lambda b, h, c: (0, k_off + h)),
        pl.BlockSpec((CONV_W, vw), lambda b, h, c: (0, v_off + h)),
        pl.BlockSpec((1, DV), lambda b, h, c: (0, 0)),
    ]
    args = [proj, proj, proj, proj, stack_rows(b_logit), stack_rows(a_logit), stack_heads(a_log),
            stack_heads(dt_bias), w_conv, w_conv, w_conv, g_norm.reshape(1, DV)]
    if has_state:
        in_specs += [
            pl.BlockSpec((BB, CONV_W - 1, qw), lambda b, h, c: (b, 0, h)),
            pl.BlockSpec((BB, CONV_W - 1, qw), lambda b, h, c: (b, 0, k_off + h)),
            pl.BlockSpec((BB, CONV_W - 1, vw), lambda b, h, c: (b, 0, v_off + h)),
            pl.BlockSpec((BB, G, DK, DV), lambda b, h, c: (b, h, 0, 0)),
        ]
        args += [conv_prev, conv_prev, conv_prev, s0]
    body = functools.partial(_gdn_kernel, C=C, Cp=Cp, G=G, NS=NS, BB=BB, rep=rep, DK=DK, DV=DV,
                             has_state=has_state, single_chunk=n == 1, activated=activated)
    return pl.pallas_call(
        body,
        grid=(B // BB, HV // G, n),
        in_specs=in_specs,
        out_specs=[pl.BlockSpec((BB, C, vw), lambda b, h, c: (b, c, h)),
                   pl.BlockSpec((BB, G, DK, DV), lambda b, h, c: (b, h, 0, 0))],
        out_shape=[jax.ShapeDtypeStruct((B, L, VAL), BF), jax.ShapeDtypeStruct((B, HV, DK, DV), F32)],
        scratch_shapes=[pltpu.VMEM((BB * G, DK, DV), F32),
                        pltpu.VMEM((BB, SUBLANES + Cp, qw), F32),
                        pltpu.VMEM((BB, SUBLANES + Cp, qw), F32),
                        pltpu.VMEM((BB, SUBLANES + Cp, vw), F32)],
        compiler_params=_params(3),
        name=name,
    )(*args)


def _kv_kernel(x_ref, g_ref, sh_ref, sc_ref, w_ref, gkv_ref, cos_ref, sin_ref,
               ckv_ref, kpe_ref, kcat_ref, h_ref, *, rows, R, P):
    _modulate_rows(x_ref, g_ref, sh_ref, sc_ref, h_ref, rows)
    y = _dot(h_ref[...], w_ref[...].astype(BF))
    c = y[:, :R]
    ckv = c * lax.rsqrt(jnp.mean(c * c, axis=-1, keepdims=True) + EPS) * gkv_ref[...]
    rot = y[:, R:R + LANES] * cos_ref[...] + y[:, R + LANES:R + 2 * LANES] * sin_ref[...]
    ckv_ref[...] = ckv
    kpe_ref[...] = rot[:, :P]
    kcat_ref[:, :R] = ckv.astype(BF)
    kcat_ref[:, R:] = rot.astype(BF)


def _rope_weight_cols(w_pe):
    P = w_pe.shape[-1]
    half = P // 2
    zeros = jnp.zeros(w_pe.shape[:-1] + (LANES - P,), w_pe.dtype)
    swapped = jnp.concatenate([w_pe[..., half:], w_pe[..., :half]], axis=-1)
    return jnp.concatenate([w_pe, zeros, swapped, zeros], axis=-1)


def kv_down_weight(w_down, R):
    return jnp.concatenate([w_down[:, :R], _rope_weight_cols(w_down[:, R:])], axis=1)


def query_up_weight(w_uq, H, NOPE):
    QL = w_uq.shape[0]
    w_heads = w_uq.reshape(QL, H, -1)
    per_head = jnp.concatenate([w_heads[..., :NOPE], _rope_weight_cols(w_heads[..., NOPE:])], axis=-1)
    return jnp.transpose(per_head, (1, 0, 2))


def shared_kv(x, g, shift, scale, w_ext, g_kv, cos_t, sin_t, rows_per_group, *, R, P, name):
    M, K = x.shape
    tm = _row_tile(M, rows_per_group, _grouped(shift))
    tm = min(tm, 512)
    rows = _pick(tm, (128, 64, 32, 16, 8))
    NW = R + 2 * LANES
    mod_specs = [_mod_spec(m, K, tm, rows_per_group) for m in (shift, scale)]
    row = lambda i: (i, 0)
    fixed = lambda i: (0, 0)
    return pl.pallas_call(
        functools.partial(_kv_kernel, rows=rows, R=R, P=P),
        grid=(M // tm,),
        in_specs=[pl.BlockSpec((tm, K), row), pl.BlockSpec((1, K), fixed), *mod_specs,
                  pl.BlockSpec((K, NW), fixed), pl.BlockSpec((1, R), fixed),
                  pl.BlockSpec((tm, LANES), row), pl.BlockSpec((tm, LANES), row)],
        out_specs=[pl.BlockSpec((tm, R), row), pl.BlockSpec((tm, P), row), pl.BlockSpec((tm, R + LANES), row)],
        out_shape=[jax.ShapeDtypeStruct((M, R), F32), jax.ShapeDtypeStruct((M, P), F32),
                   jax.ShapeDtypeStruct((M, R + LANES), BF)],
        scratch_shapes=[pltpu.VMEM((tm, K), BF)],
        compiler_params=_params(1),
        name=name,
    )(x, g.reshape(1, K), shift[0], scale[0], w_ext, g_kv.reshape(1, R), cos_t, sin_t)


def _q_kernel(x_ref, g_ref, sh_ref, sc_ref, wdq_ref, gq_ref, wuq_ref, wuk_ref, cos_ref, sin_ref,
              o_ref, cq_ref, h_ref, *, rows, NOPE, R, HB, head_major):
    @pl.when(pl.program_id(1) == 0)
    def _():
        _modulate_rows(x_ref, g_ref, sh_ref, sc_ref, h_ref, rows)
        c = _dot(h_ref[...], wdq_ref[...].astype(BF))
        cq_ref[...] = (c * lax.rsqrt(jnp.mean(c * c, axis=-1, keepdims=True) + EPS) * gq_ref[...]).astype(BF)

    W = R + LANES
    for hb in range(HB):
        qf = _dot(cq_ref[...], wuq_ref[hb].astype(BF))
        q_lat = _dot_nt(qf[:, :NOPE].astype(BF), wuk_ref[:, hb * NOPE:(hb + 1) * NOPE].astype(BF))
        rot = qf[:, NOPE:NOPE + LANES] * cos_ref[...] + qf[:, NOPE + LANES:NOPE + 2 * LANES] * sin_ref[...]
        if head_major:
            o_ref[hb, :, :R] = q_lat.astype(o_ref.dtype)
            o_ref[hb, :, R:] = rot.astype(o_ref.dtype)
        else:
            o_ref[:, hb * W:hb * W + R] = q_lat.astype(o_ref.dtype)
            o_ref[:, hb * W + R:(hb + 1) * W] = rot.astype(o_ref.dtype)


def mla_queries(x, g, shift, scale, w_dq, g_q, w_uq_r, w_uk, cos_t, sin_t, rows_per_group, *,
                H, NOPE, P, R, head_major, name):
    M, K = x.shape
    QL = w_dq.shape[1]
    tm = _row_tile(M, rows_per_group, True) if head_major else _row_tile(M, rows_per_group, _grouped(shift))
    tm = min(tm, 512)
    rows = _pick(tm, (128, 64, 32, 16, 8))
    NQ = NOPE + 2 * LANES
    W = R + LANES
    HB = _pick(H, (Q_HEADS_PER_STEP, 2, 1))
    mod_specs = [_mod_spec(m, K, tm, rows_per_group) for m in (shift, scale)]
    row = lambda i, h: (i, 0)
    fixed = lambda i, h: (0, 0)
    if head_major:
        tiles = rows_per_group // tm
        out_spec = pl.BlockSpec((None, HB, tm, W), lambda i, h: (i // tiles, h, i % tiles, 0))
        out_shape = jax.ShapeDtypeStruct((M // rows_per_group, H, rows_per_group, W), BF)
    else:
        out_spec = pl.BlockSpec((tm, HB * W), lambda i, h: (i, h))
        out_shape = jax.ShapeDtypeStruct((M, H * W), BF)
    return pl.pallas_call(
        functools.partial(_q_kernel, rows=rows, NOPE=NOPE, R=R, HB=HB, head_major=head_major),
        grid=(M // tm, H // HB),
        in_specs=[pl.BlockSpec((tm, K), row), pl.BlockSpec((1, K), fixed), *mod_specs,
                  pl.BlockSpec((K, QL), fixed), pl.BlockSpec((1, QL), fixed),
                  pl.BlockSpec((HB, QL, NQ), lambda i, h: (h, 0, 0)),
                  pl.BlockSpec((R, HB * NOPE), lambda i, h: (0, h)),
                  pl.BlockSpec((tm, LANES), row), pl.BlockSpec((tm, LANES), row)],
        out_specs=out_spec,
        out_shape=out_shape,
        scratch_shapes=[pltpu.VMEM((tm, QL), BF), pltpu.VMEM((tm, K), BF)],
        compiler_params=_params(2),
        name=name,
    )(x, g.reshape(1, K), shift[0], scale[0], w_dq, g_q.reshape(1, QL), w_uq_r, w_uk, cos_t, sin_t)


def _attn_prompt_kernel(q_ref, k_ref, o_ref, m_scr, l_scr, a_scr, acc_scr, s_scr, p_scr, *, tq, tk, H, R, scale):
    qi = pl.program_id(1)
    kj = pl.program_id(2)

    @pl.when(kj == 0)
    def _():
        m_scr[...] = jnp.full_like(m_scr, -jnp.inf)
        l_scr[...] = jnp.zeros_like(l_scr)
        acc_scr[...] = jnp.zeros_like(acc_scr)

    rows = H * tq
    hg = max(1, ATTN_GROUP_ROWS // tq)
    group = hg * tq
    n_groups = rows // group
    rc = min(ATTN_SOFTMAX_ROWS, group)
    c2 = scale * LOG2_E

    def scores(g):
        q = q_ref[g * hg:(g + 1) * hg].reshape(group, q_ref.shape[-1])
        s_scr[g * group:(g + 1) * group, :] = _dot_nt(q, k_ref[...])

    def softmax(g, masked):
        for r0 in range(g * group, (g + 1) * group, rc):
            sl = slice(r0, r0 + rc)
            s = s_scr[sl, :]
            if masked:
                qpos = qi * tq + lax.rem(r0 + lax.broadcasted_iota(jnp.int32, (rc, tk), 0), tq)
                kpos = kj * tk + lax.broadcasted_iota(jnp.int32, (rc, tk), 1)
                s = jnp.where(kpos <= qpos, s, -jnp.inf)
            m_old = m_scr[sl, :]
            m_new = jnp.maximum(m_old, jnp.max(s, axis=-1, keepdims=True))
            alpha = jnp.exp2((m_old - m_new) * c2)
            p = jnp.exp2((s - m_new) * c2)
            l_scr[sl, :] = alpha * l_scr[sl, :] + jnp.sum(p, axis=-1, keepdims=True)
            m_scr[sl, :] = m_new
            a_scr[sl, :] = alpha
            p_scr[sl, :] = p.astype(BF)

    def values(g):
        gs = slice(g * group, (g + 1) * group)
        acc_scr[gs, :] = a_scr[gs, :] * acc_scr[gs, :] + _dot(p_scr[gs, :], k_ref[:, :R])

    def block(masked):
        for g in range(n_groups + 2):
            if g < n_groups:
                scores(g)
            if 1 <= g <= n_groups:
                softmax(g - 1, masked)
            if g >= 2:
                values(g - 2)

    crosses_diagonal = kj * tk + tk - 1 > qi * tq

    @pl.when((kj * tk <= qi * tq + tq - 1) & crosses_diagonal)
    def _():
        block(True)

    @pl.when(jnp.logical_not(crosses_diagonal))
    def _():
        block(False)

    @pl.when(kj == pl.num_programs(2) - 1)
    def _():
        o = acc_scr[...] / l_scr[...]
        o_ref[...] = o.reshape(H, tq, R).astype(o_ref.dtype)


def attention_prompt(q4, kcat, *, R, scale, name):
    B, H, L, W = q4.shape
    tq = _pick(L, (128, 64, 32, 16))
    tk = _pick(L, (512, 256, 128, 64, 32, 16))
    nq, nk = L // tq, L // tk

    def k_index(b, i, j):
        return (b, jnp.minimum(j, (i * tq + tq - 1) // tk), 0)

    return pl.pallas_call(
        functools.partial(_attn_prompt_kernel, tq=tq, tk=tk, H=H, R=R, scale=scale),
        grid=(B, nq, nk),
        in_specs=[pl.BlockSpec((None, H, tq, W), lambda b, i, j: (b, 0, i, 0)),
                  pl.BlockSpec((None, tk, W), k_index)],
        out_specs=pl.BlockSpec((None, H, tq, R), lambda b, i, j: (b, 0, i, 0)),
        out_shape=jax.ShapeDtypeStruct((B, H, L, R), BF),
        scratch_shapes=[pltpu.VMEM((H * tq, 1), F32), pltpu.VMEM((H * tq, 1), F32), pltpu.VMEM((H * tq, 1), F32),
                        pltpu.VMEM((H * tq, R), F32), pltpu.VMEM((H * tq, tk), F32), pltpu.VMEM((H * tq, tk), BF)],
        compiler_params=_params(3),
        name=name,
    )(q4, kcat)


def _attn_sample_kernel(pt_ref, q_ref, knew_ref, ckv_hbm, kpe_hbm, o_ref, m_scr, l_scr, acc_scr,
                        ckv_buf, kpe_buf, sems, *, NP, T, H, R, P, scale):
    b = pl.program_id(0)
    j = pl.program_id(1)
    nj = pl.num_programs(1)
    step = b * nj + j
    n_steps = pl.num_programs(0) * nj
    n_slots = ckv_buf.shape[0]
    slot = lax.rem(step, n_slots)
    rows = T * H

    def page_copies(st):
        bb, jj, sl = lax.div(st, nj), lax.rem(st, nj), lax.rem(st, n_slots)
        copies = []
        for i in range(NP):
            pid = pt_ref[bb, jj * NP + i]
            copies.append(pltpu.make_async_copy(ckv_hbm.at[pid], ckv_buf.at[sl, i], sems.at[0, sl]))
            copies.append(pltpu.make_async_copy(kpe_hbm.at[pid], kpe_buf.at[sl, i], sems.at[1, sl]))
        return copies

    @pl.when(step == 0)
    def _():
        for ahead in range(n_slots - 1):
            @pl.when(ahead < n_steps)
            def _():
                for cp in page_copies(step + ahead):
                    cp.start()

    @pl.when(step + n_slots - 1 < n_steps)
    def _():
        for cp in page_copies(step + n_slots - 1):
            cp.start()

    for cp in page_copies(step):
        cp.wait()
    q = q_ref[...]
    q_lat = q[:, :R]
    q_pe = q[:, R:R + P]

    c2 = scale * LOG2_E

    def update(state, s, values):
        m_old, l_old, acc = state
        m_new = jnp.maximum(m_old, jnp.max(s, axis=-1, keepdims=True))
        alpha = jnp.exp2((m_old - m_new) * c2)
        p = jnp.exp2((s - m_new) * c2)
        return m_new, alpha * l_old + jnp.sum(p, axis=-1, keepdims=True), alpha * acc + values(p.astype(BF))

    @pl.when(j == 0)
    def _():
        tp = -(-T // BF16_ROWS) * BF16_ROWS
        knew = jnp.concatenate([knew_ref[...], jnp.zeros((tp - T, knew_ref.shape[1]), BF)], axis=0)
        s = _dot_nt(q, knew)
        qt = lax.broadcasted_iota(jnp.int32, (rows, tp), 0) // H
        kt = lax.broadcasted_iota(jnp.int32, (rows, tp), 1)
        s = jnp.where(kt <= qt, s, -jnp.inf)
        init = (jnp.full((rows, 1), -jnp.inf, F32), jnp.zeros((rows, 1), F32), jnp.zeros((rows, R), F32))
        m_scr[...], l_scr[...], acc_scr[...] = update(init, s, lambda p: _dot(p, knew[:, :R]))

    pg = min(ATTN_PAGE_GROUP, NP)
    n_groups = NP // pg

    def load(g):
        return [ckv_buf[slot, i].astype(BF) for i in range(g * pg, (g + 1) * pg)]

    def scores(g, pages):
        return jnp.concatenate([_dot_nt(q_lat, pages[i]) + _dot(q_pe, kpe_buf[slot, g * pg + i].astype(BF))
                                for i in range(pg)], axis=1)

    def values_of(pages):
        def values(p):
            page = pages[0].shape[0]
            out = _dot(p[:, :page], pages[0])
            for i in range(1, pg):
                out = out + _dot(p[:, i * page:(i + 1) * page], pages[i])
            return out
        return values

    state = (m_scr[...], l_scr[...], acc_scr[...])
    pages = load(0)
    s = scores(0, pages)
    for g in range(n_groups):
        if g + 1 < n_groups:
            pages_next = load(g + 1)
            s_next = scores(g + 1, pages_next)
        state = update(state, s, values_of(pages))
        if g + 1 < n_groups:
            pages, s = pages_next, s_next
    m_scr[...], l_scr[...], acc_scr[...] = state

    @pl.when(j == pl.num_programs(1) - 1)
    def _():
        o_ref[...] = (acc_scr[...] / l_scr[...]).astype(o_ref.dtype)


def attention_sample(q3, knew, cache_ckv, cache_kpe_t, page_table, *, T, H, R, P, scale, name):
    Bd, rows, W = q3.shape
    n_pages = page_table.shape[1]
    page = cache_ckv.shape[1]
    NP = _pick(n_pages, (ATTN_PAGES_PER_STEP, 16, 8, 4, 2, 1))
    any_space = pl.BlockSpec(memory_space=pl.ANY)
    grid_spec = pltpu.PrefetchScalarGridSpec(
        num_scalar_prefetch=1,
        grid=(Bd, n_pages // NP),
        in_specs=[pl.BlockSpec((None, rows, W), lambda b, j, pt: (b, 0, 0)),
                  pl.BlockSpec((None, T, W), lambda b, j, pt: (b, 0, 0)),
                  any_space, any_space],
        out_specs=pl.BlockSpec((None, rows, R), lambda b, j, pt: (b, 0, 0)),
        scratch_shapes=[pltpu.VMEM((rows, 1), F32), pltpu.VMEM((rows, 1), F32), pltpu.VMEM((rows, R), F32),
                        pltpu.VMEM((ATTN_PAGE_SLOTS, NP, page, R), F32),
                        pltpu.VMEM((ATTN_PAGE_SLOTS, NP, P, page), F32),
                        pltpu.SemaphoreType.DMA((2, ATTN_PAGE_SLOTS))],
    )
    return pl.pallas_call(
        functools.partial(_attn_sample_kernel, NP=NP, T=T, H=H, R=R, P=P, scale=scale),
        grid_spec=grid_spec,
        out_shape=jax.ShapeDtypeStruct((Bd, rows, R), BF),
        compiler_params=_params(2),
        name=name,
    )(page_table, q3, knew, cache_ckv, cache_kpe_t)


def _value_up_kernel(a_ref, w_ref, o_ref, *, H, R, V, head_major):
    for h in range(H):
        a = a_ref[h] if head_major else a_ref[:, h * R:(h + 1) * R]
        o_ref[:, h * V:(h + 1) * V] = _dot(a, w_ref[:, h * V:(h + 1) * V].astype(BF)).astype(o_ref.dtype)


def value_up(o_lat, w_uv, *, H, R, V, head_major, name):
    if head_major:
        G, _, L, _ = o_lat.shape
        M = G * L
        tm = _pick(L, (512, 256, 128, 64, 32, 16, 8))
        tiles = L // tm
        a_spec = pl.BlockSpec((None, H, tm, R), lambda i: (i // tiles, 0, i % tiles, 0))
    else:
        M = o_lat.shape[0]
        tm = _pick(M, (512, 256, 128, 64, 32, 16, 8))
        a_spec = pl.BlockSpec((tm, H * R), lambda i: (i, 0))
    return pl.pallas_call(
        functools.partial(_value_up_kernel, H=H, R=R, V=V, head_major=head_major),
        grid=(M // tm,),
        in_specs=[a_spec, pl.BlockSpec((R, H * V), lambda i: (0, 0))],
        out_specs=pl.BlockSpec((tm, H * V), lambda i: (i, 0)),
        out_shape=jax.ShapeDtypeStruct((M, H * V), BF),
        compiler_params=_params(1),
        name=name,
    )(o_lat, w_uv)


def _rope_tables(pos, P, reps):
    half = P // 2
    inv = ROPE_THETA ** (-jnp.arange(half, dtype=F32) / half)
    ang = pos.astype(F32)[:, None] * inv[None, :]
    cos, sin = jnp.cos(ang), jnp.sin(ang)
    zeros = jnp.zeros((pos.shape[0], LANES - P), F32)
    cos_t = jnp.concatenate([cos, cos, zeros], axis=1)
    sin_t = jnp.concatenate([-sin, sin, zeros], axis=1)
    return jnp.tile(cos_t, (reps, 1)), jnp.tile(sin_t, (reps, 1))


def _trunk(x3, mods, pos, conv_in, ssm_in, past, p, tag):
    G, L, D = x3.shape
    M = G * L
    grouped = L % SUBLANES == 0
    x = x3.reshape(M, D)

    def split_mods(m, n):
        arr = m[:, None, :] if grouped else m
        return [(arr, k) for k in range(n)]

    HV, DK, DV = p['HV'], p['DK'], p['DV']
    HQK = p['HQK']
    KEY, VAL = HQK * DK, HV * DV
    H, NOPE, P, R, V = p['H'], p['NOPE'], p['P'], p['R'], p['V']
    FF = p['w_down'].shape[1]
    cos_t, sin_t = _rope_tables(pos, P, G)
    scale = (NOPE + P) ** -0.5

    sh1, sc1, gt1, sh2, sc2, gt2 = split_mods(mods['l0'], 6)
    w_in_t = p['gdn_w_in_t']
    n_proj = 2 * KEY + 2 * VAL
    fused_front = conv_in is None and grouped and L % _row_tile(M, L, True) == 0
    if fused_front:
        proj, tail, ba = gdn_front(x, p['g_mix'][0], sh1, sc1, w_in_t, p['gdn_w_ba_t'], p['gdn_w_conv'][0], L,
                                   KEY=KEY, VAL=VAL, DK=DK, name=f'{tag}_gdn_in')
    else:
        proj = mod_matmul(x, p['g_mix'][0], sh1, sc1, w_in_t, 0, n_proj, L, w_is_nk=True, name=f'{tag}_gdn_in')
        ba = mod_matmul(x, p['g_mix'][0], sh1, sc1, p['gdn_w_ba_t'], 0, LANES, L, w_is_nk=True,
                        name=f'{tag}_gdn_ba')
    proj3 = proj.reshape(G, L, n_proj)
    o_g, ssm_new = gdn_scan(
        proj3, ba[:, :HV], ba[:, HV:2 * HV], conv_in, ssm_in, p['gdn_w_conv'][0], p['gdn_a_log'][0],
        p['gdn_dt_bias'][0], p['gdn_g_norm'][0], B=G, L=L, HQK=HQK, HV=HV, DK=DK, DV=DV, activated=fused_front,
        name=f'{tag}_gdn_scan')
    if fused_front:
        conv_src = tail.reshape(G, -1, SUBLANES, n_proj)[:, -1]
    else:
        conv_src = proj3
    conv_new = conv_src[:, conv_src.shape[1] - (CONV_W - 1):, :2 * KEY + VAL]
    x = matmul_residual(o_g.reshape(M, VAL), p['gdn_w_out'], 0, x, gt1, L, name=f'{tag}_gdn_out')
    hff = mod_matmul(x, p['g_ffn'][0], sh2, sc2, p['w_gate_up'], 0, FF, L, swiglu=True, out_dtype=BF,
                     name=f'{tag}_ffn0_up')
    x = matmul_residual(hff, p['w_down'], 0, x, gt2, L, name=f'{tag}_ffn0_down')

    shk, sck = split_mods(mods['kv'], 2)
    ckv, kpe, kcat = shared_kv(x, p['kv_g_in'], shk, sck, p['kv_w_ext'], p['kv_g_norm'], cos_t, sin_t, L,
                               R=R, P=P, name=f'{tag}_kv')

    sh1, sc1, gt1, sh2, sc2, gt2 = split_mods(mods['l1'], 6)
    head_major = past is None
    q = mla_queries(x, p['g_mix'][1], sh1, sc1, p['mla_w_dq'][0], p['mla_g_q'][0], p['mla_w_uq_r'], p['kv_w_uk'],
                    cos_t, sin_t, L, H=H, NOPE=NOPE, P=P, R=R, head_major=head_major, name=f'{tag}_q')
    if head_major:
        o_lat = attention_prompt(q, kcat.reshape(G, L, R + LANES), R=R, scale=scale, name=f'{tag}_attn')
    else:
        cache_ckv, cache_kpe, page_table = past
        o_lat = attention_sample(q.reshape(G, L * H, R + LANES), kcat.reshape(G, L, R + LANES), cache_ckv,
                                 cache_kpe, page_table, T=L, H=H, R=R, P=P, scale=scale, name=f'{tag}_attn')
        o_lat = o_lat.reshape(M, H * R)
    o = value_up(o_lat, p['kv_w_uv'], H=H, R=R, V=V, head_major=head_major, name=f'{tag}_uv')
    x = matmul_residual(o, p['mla_w_o'], 0, x, gt1, L, name=f'{tag}_attn_out')
    hff = mod_matmul(x, p['g_ffn'][1], sh2, sc2, p['w_gate_up'], 1, FF, L, swiglu=True, out_dtype=BF,
                     name=f'{tag}_ffn1_up')
    x = matmul_residual(hff, p['w_down'], 1, x, gt2, L, name=f'{tag}_ffn1_down')

    shf, scf = split_mods(mods['final'], 2)
    y = modulate_rows(x, p['final_g'], shf, scf, L, name=f'{tag}_final')
    return (y.reshape(G, L, D), conv_new[None], ssm_new[None], ckv.reshape(G, L, R), kpe.reshape(G, L, P))


def kernel(x_prompt, x_sample, c_prompt, c_sample, cache_ckv, cache_kpe, page_table, state_ssm, state_conv, w_ada, b_ada, g_mix, g_ffn, w_gate_up, w_down, gdn_w_in, gdn_w_conv, gdn_a_log, gdn_dt_bias, gdn_g_norm, gdn_w_out, kv_w_ada, kv_b_ada, kv_g_in, kv_w_down, kv_g_norm, kv_w_uk, kv_w_uv, mla_w_dq, mla_g_q, mla_w_uq, mla_w_o, final_w_ada, final_b_ada, final_g):
    B, S, D = x_prompt.shape
    Bd, T, _ = x_sample.shape
    HV, DK, DV = state_ssm.shape[2:]
    KEY = (state_conv.shape[-1] - HV * DV) // 2
    R = cache_ckv.shape[-1]
    P = cache_kpe.shape[-1]
    QL = mla_w_dq.shape[-1]
    nope_total = kv_w_uk.shape[1]
    H = (mla_w_uq.shape[-1] - nope_total) // P
    NOPE = nope_total // H
    w_in_t = jnp.swapaxes(gdn_w_in, 1, 2)
    n_proj = 2 * KEY + 2 * HV * DV
    w_ba_t = jnp.concatenate([w_in_t[0, n_proj:], jnp.zeros((LANES - 2 * HV, D), F32)], axis=0)
    p = dict(w_ada=w_ada, b_ada=b_ada, g_mix=g_mix, g_ffn=g_ffn, w_gate_up=w_gate_up, w_down=w_down,
             gdn_w_in_t=w_in_t, gdn_w_ba_t=w_ba_t, gdn_w_conv=gdn_w_conv, gdn_a_log=gdn_a_log,
             gdn_dt_bias=gdn_dt_bias, gdn_g_norm=gdn_g_norm, gdn_w_out=gdn_w_out, kv_g_in=kv_g_in,
             kv_w_ext=kv_down_weight(kv_w_down, R), kv_g_norm=kv_g_norm, kv_w_uk=kv_w_uk, kv_w_uv=kv_w_uv,
             mla_w_dq=mla_w_dq, mla_g_q=mla_g_q, mla_w_uq_r=query_up_weight(mla_w_uq[0], H, NOPE),
             mla_w_o=mla_w_o, final_g=final_g,
             HV=HV, DK=DK, DV=DV, HQK=KEY // DK, H=H, NOPE=NOPE, P=P, R=R, V=kv_w_uv.shape[1] // H)

    per_row_s = T % SUBLANES != 0
    c_s = jnp.repeat(c_sample, T, axis=0) if per_row_s else c_sample
    n_s = c_s.shape[0]
    c_all = jnp.concatenate([c_s, c_prompt, jnp.zeros((-(n_s + B) % SUBLANES, D), F32)], axis=0)
    m_l0 = ada_dense(c_all, w_ada, b_ada, 0, name='ada_l0')
    m_l1 = ada_dense(c_all, w_ada, b_ada, 1, name='ada_l1')
    m_kv = ada_dense(c_all, kv_w_ada, kv_b_ada, 0, name='ada_kv')
    m_f = ada_dense(c_all, final_w_ada, final_b_ada, 0, name='ada_final')

    m_all = dict(l0=m_l0, l1=m_l1, kv=m_kv, final=m_f)
    mods_p = {k: m[n_s:n_s + B] for k, m in m_all.items()}
    mods_s = m_all if per_row_s else {k: m[:n_s] for k, m in m_all.items()}

    y_p, conv_p, ssm_p, ckv_p, kpe_p = _trunk(x_prompt, mods_p, jnp.arange(S), None, None, None, p, 'p')
    past_len = page_table.shape[1] * cache_ckv.shape[1]
    y_s, conv_s, ssm_s, ckv_s, kpe_s = _trunk(x_sample, mods_s, past_len + jnp.arange(T), state_conv[0],
                                              state_ssm[0], (cache_ckv, jnp.swapaxes(cache_kpe, 1, 2), page_table),
                                              p, 's')
    return (y_p, y_s, ssm_p, conv_p, ckv_p, kpe_p, ssm_s, conv_s, ckv_s, kpe_s)
```

```python
import functools

import jax
import jax.numpy as jnp
from jax import lax
from jax.experimental import pallas as pl
from jax.experimental.pallas import tpu as pltpu

EPS = 1e-6
ROPE_THETA = 10000.0
CONV_W = 4
GDN_CHUNK = 64
F32 = jnp.float32
BF = jnp.bfloat16

V7X_VMEM_BYTES = 64 * 1024 * 1024
VMEM_LIMIT = V7X_VMEM_BYTES - 8 * 1024 * 1024
VMEM_TILE_BUDGET = (VMEM_LIMIT * 9) // 10
LANES = 128
SUBLANES = 8
BF16_ROWS = 16
TRI_BLOCK = 16
STACK_ROWS = 128
MAX_STACK_HEADS = 8
SEQS_PER_STEP = 4
STACKS_PER_STEP = 16
FRONT_GROUP_ROWS = 256
ATTN_GROUP_ROWS = 256
ATTN_SOFTMAX_ROWS = 64
ATTN_PAGES_PER_STEP = 64
ATTN_PAGE_GROUP = 8
ATTN_PAGE_SLOTS = 2
Q_HEADS_PER_STEP = 8
LOG2_E = 1.4426950408889634


def _params(n_axes):
    return pltpu.CompilerParams(dimension_semantics=("arbitrary",) * n_axes, vmem_limit_bytes=VMEM_LIMIT)


def _pick(n, cands):
    for c in cands:
        if n % c == 0:
            return c
    return n


def _silu(x):
    return x * jax.nn.sigmoid(x)


def _contract(a, b, ca, cb):
    batch = tuple(range(a.ndim - 2))
    dims = (((a.ndim - 2 + ca,), (b.ndim - 2 + cb,)), (batch, batch))
    return lax.dot_general(a, b, dims, preferred_element_type=F32)


def _dot(a, b):
    return _contract(a, b, 1, 0)


def _dot_nt(a, b):
    return _contract(a, b, 1, 1)


def _dot_tn(a, b):
    return _contract(a, b, 0, 0)


def _split_bf16(a):
    hi = a.astype(BF)
    lo = (a - hi.astype(F32)).astype(BF)
    return hi, lo


def _dot3(a, b):
    ah, al = _split_bf16(a)
    bh, bl = _split_bf16(b)
    return _dot(ah, bh) + _dot(ah, bl) + _dot(al, bh)


def _modulate_rows(x_ref, g_ref, sh_ref, sc_ref, h_ref, rows):
    tm = x_ref.shape[0]
    per_row = sh_ref.shape[0] != 1

    def body(r, carry):
        sl = pl.ds(pl.multiple_of(r * rows, rows), rows)
        x = x_ref[sl, :]
        y = x * lax.rsqrt(jnp.mean(x * x, axis=-1, keepdims=True) + EPS) * g_ref[...]
        sc = sc_ref[sl, :] if per_row else sc_ref[...]
        sh = sh_ref[sl, :] if per_row else sh_ref[...]
        h_ref[sl, :] = (y * (1.0 + sc) + sh).astype(h_ref.dtype)
        return carry

    lax.fori_loop(0, tm // rows, body, 0)


def _grouped(mod):
    return mod[0].ndim == 3


def _mod_spec(mod, K, tm, rows_per_group):
    arr, k = mod
    if arr.ndim == 3:
        tiles_per_group = rows_per_group // tm
        return pl.BlockSpec((None, 1, K), lambda i, *_: (i // tiles_per_group, 0, k))
    return pl.BlockSpec((tm, K), lambda i, *_: (i, k))


def _w_spec(w, layer, K, tn, col_off_blocks=0):
    if w.ndim == 3:
        return pl.BlockSpec((None, K, tn), lambda i, j: (layer, 0, j + col_off_blocks))
    return pl.BlockSpec((K, tn), lambda i, j: (0, j + col_off_blocks))


def _modmm_kernel(x_ref, g_ref, sh_ref, sc_ref, w_ref, o_ref, h_ref, *, rows, w_is_nk):
    @pl.when(pl.program_id(1) == 0)
    def _():
        _modulate_rows(x_ref, g_ref, sh_ref, sc_ref, h_ref, rows)

    dot = _dot_nt if w_is_nk else _dot
    o_ref[...] = dot(h_ref[...], w_ref[...].astype(BF)).astype(o_ref.dtype)


def _modmm_swiglu_kernel(x_ref, g_ref, sh_ref, sc_ref, wg_ref, wu_ref, o_ref, h_ref, *, rows):
    @pl.when(pl.program_id(1) == 0)
    def _():
        _modulate_rows(x_ref, g_ref, sh_ref, sc_ref, h_ref, rows)

    h = h_ref[...]
    gate = _dot(h, wg_ref[...].astype(BF))
    up = _dot(h, wu_ref[...].astype(BF))
    o_ref[...] = (_silu(gate) * up).astype(o_ref.dtype)


def _gdn_front_kernel(x_ref, g_ref, sh_ref, sc_ref, w_ref, wba_ref, cw_ref, o_ref, tail_ref, ba_ref, h_ref, cb,
                      halo, *, rows, tiles_per_seq, n_qk, n_conv, DK):
    i = pl.program_id(0)
    j = pl.program_id(1)
    tm, tn = o_ref.shape
    tail_lo = SUBLANES - (CONV_W - 1)
    group = min(FRONT_GROUP_ROWS, tm)
    rc = min(ATTN_SOFTMAX_ROWS, group)
    edges = list(range(0, tm + 1, group))
    n_groups = len(edges) - 1

    @pl.when(j == 0)
    def _():
        _modulate_rows(x_ref, g_ref, sh_ref, sc_ref, h_ref, rows)
        ba_ref[...] = _dot_nt(h_ref[...], wba_ref[...].astype(BF))

    def project(g, w_bf):
        lo, hi = edges[g], edges[g + 1]
        cb[SUBLANES + lo:SUBLANES + hi, :] = _dot_nt(h_ref[lo:hi, :], w_bf)

    def activate(g, kind):
        for r0 in range(edges[g], edges[g + 1], rc):
            if kind == 'z':
                act = _silu(cb[SUBLANES + r0:SUBLANES + r0 + rc, :])
            else:
                acc = cb[tail_lo + r0:tail_lo + r0 + rc, :] * cw_ref[0:1, :]
                for t in range(1, CONV_W):
                    acc = acc + cb[tail_lo + r0 + t:tail_lo + r0 + t + rc, :] * cw_ref[t:t + 1, :]
                act = _silu(acc)
                if kind in ('q', 'k'):
                    heads = []
                    for hh in range(tn // DK):
                        a = act[:, hh * DK:(hh + 1) * DK]
                        f = lax.rsqrt(jnp.sum(a * a, axis=-1, keepdims=True) + EPS)
                        heads.append(a * (f * (DK ** -0.5) if kind == 'q' else f))
                    act = jnp.concatenate(heads, axis=1)
            o_ref[r0:r0 + rc, :] = act

    def tile(kind):
        if kind != 'z':
            first = lax.rem(i, tiles_per_seq) == 0

            @pl.when(first)
            def _():
                cb[0:SUBLANES, :] = jnp.zeros((SUBLANES, tn), F32)

            @pl.when(jnp.logical_not(first))
            def _():
                cb[0:SUBLANES, :] = halo[j]

        w_bf = w_ref[...].astype(BF)
        for g in range(n_groups + 1):
            if g < n_groups:
                project(g, w_bf)
            if g >= 1:
                activate(g - 1, kind)
        last_rows = cb[tm:tm + SUBLANES, :]
        tail_ref[...] = last_rows
        if kind != 'z':
            halo[j] = last_rows

    n_q = n_qk // 2
    pl.when(j < n_q)(lambda: tile('q'))
    pl.when((j >= n_q) & (j < n_qk))(lambda: tile('k'))
    pl.when((j >= n_qk) & (j < n_conv))(lambda: tile('v'))
    pl.when(j >= n_conv)(lambda: tile('z'))


def gdn_front(x, g, shift, scale, w_t, w_ba_t, w_conv, rows_per_group, *, KEY, VAL, DK, name):
    M, K = x.shape
    n_out = 2 * KEY + 2 * VAL
    tm = _row_tile(M, rows_per_group, True)
    tn = _pick(KEY, (512, 256, 128))
    assert _grouped(shift) and rows_per_group % tm == 0 and VAL % tn == 0 and tn % DK == 0
    tiles_per_seq = rows_per_group // tm
    n_qk, n_conv = 2 * KEY // tn, (2 * KEY + VAL) // tn
    rows = _pick(tm, (128, 64, 32, 16, 8))
    mod_specs = [_mod_spec(m, K, tm, rows_per_group) for m in (shift, scale)]
    body = functools.partial(_gdn_front_kernel, rows=rows, tiles_per_seq=tiles_per_seq, n_qk=n_qk,
                             n_conv=n_conv, DK=DK)
    return pl.pallas_call(
        body,
        grid=(M // tm, n_out // tn),
        in_specs=[pl.BlockSpec((tm, K), lambda i, j: (i, 0)),
                  pl.BlockSpec((1, K), lambda i, j: (0, 0)),
                  *mod_specs,
                  pl.BlockSpec((None, tn, K), lambda i, j: (0, j, 0)),
                  pl.BlockSpec((LANES, K), lambda i, j: (0, 0)),
                  pl.BlockSpec((CONV_W, tn), lambda i, j: (0, jnp.minimum(j, n_conv - 1)))],
        out_specs=[pl.BlockSpec((tm, tn), lambda i, j: (i, j)),
                   pl.BlockSpec((None, SUBLANES, tn), lambda i, j: (i, 0, j)),
                   pl.BlockSpec((tm, LANES), lambda i, j: (i, 0))],
        out_shape=[jax.ShapeDtypeStruct((M, n_out), F32),
                   jax.ShapeDtypeStruct((M // tm, SUBLANES, n_out), F32),
                   jax.ShapeDtypeStruct((M, LANES), F32)],
        scratch_shapes=[pltpu.VMEM((tm, K), BF), pltpu.VMEM((SUBLANES + tm, tn), F32),
                        pltpu.VMEM((n_conv, SUBLANES, tn), F32)],
        compiler_params=_params(2),
        name=name,
    )(x, g.reshape(1, K), shift[0], scale[0], w_t, w_ba_t, w_conv)


def _row_tile(M, rows_per_group, grouped):
    base = rows_per_group if grouped else M
    return _pick(base, (1024, 512, 256, 128, 64, 32, 16, 8))


def _col_tile(N, fixed_bytes, bytes_per_col):
    for tn in (1024, 512, 256):
        if N % tn == 0 and fixed_bytes + bytes_per_col * tn <= VMEM_TILE_BUDGET:
            return tn
    return _pick(N, (128,))


def _weight_col_bytes(K, n_weights=1):
    return n_weights * K * (2 * 4 + 2)


def mod_matmul(x, g, shift, scale, w, layer, n_out, rows_per_group, *, swiglu=False, w_is_nk=False,
               out_dtype=F32, name):
    M, K = x.shape
    tm = _row_tile(M, rows_per_group, _grouped(shift))
    n_w = 2 if swiglu else 1
    out_bytes = jnp.dtype(out_dtype).itemsize
    tn = _col_tile(n_out, tm * K * (2 * 4 + 2),
                   _weight_col_bytes(K, n_w) + tm * (2 * out_bytes + 4 * n_w))
    rows = _pick(tm, (128, 64, 32, 16, 8))
    mod_specs = [_mod_spec(m, K, tm, rows_per_group) for m in (shift, scale)]
    if not w_is_nk:
        w_spec = _w_spec(w, layer, K, tn)
    elif w.ndim == 3:
        w_spec = pl.BlockSpec((None, tn, K), lambda i, j: (layer, j, 0))
    else:
        w_spec = pl.BlockSpec((tn, K), lambda i, j: (j, 0))
    in_specs = [pl.BlockSpec((tm, K), lambda i, j: (i, 0)),
                pl.BlockSpec((1, K), lambda i, j: (0, 0)),
                *mod_specs, w_spec]
    args = [x, g.reshape(1, K), shift[0], scale[0], w]
    if swiglu:
        in_specs.append(_w_spec(w, layer, K, tn, n_out // tn))
        args.append(w)
        body = functools.partial(_modmm_swiglu_kernel, rows=rows)
    else:
        body = functools.partial(_modmm_kernel, rows=rows, w_is_nk=w_is_nk)
    return pl.pallas_call(
        body,
        grid=(M // tm, n_out // tn),
        in_specs=in_specs,
        out_specs=pl.BlockSpec((tm, tn), lambda i, j: (i, j)),
        out_shape=jax.ShapeDtypeStruct((M, n_out), out_dtype),
        scratch_shapes=[pltpu.VMEM((tm, K), BF)],
        compiler_params=_params(2),
        name=name,
    )(*args)


def _mmres_kernel(a_ref, w_ref, res_ref, gate_ref, o_ref):
    y = _dot(a_ref[...], w_ref[...].astype(BF))
    o_ref[...] = res_ref[...] + gate_ref[...] * y


def matmul_residual(a, w, layer, res, gate, rows_per_group, *, name):
    M, K = a.shape
    N = res.shape[1]
    tm = _row_tile(M, rows_per_group, _grouped(gate))
    tn = _col_tile(N, tm * K * 2 * a.dtype.itemsize, _weight_col_bytes(K) + tm * (4 * 4 + 4))
    gate_arr, gate_k = gate
    gate_off = gate_k * (N // tn)
    if _grouped(gate):
        tiles_per_group = rows_per_group // tm
        gate_spec = pl.BlockSpec((None, 1, tn), lambda i, j: (i // tiles_per_group, 0, gate_off + j))
    else:
        gate_spec = pl.BlockSpec((tm, tn), lambda i, j: (i, gate_off + j))
    return pl.pallas_call(
        _mmres_kernel,
        grid=(M // tm, N // tn),
        in_specs=[pl.BlockSpec((tm, K), lambda i, j: (i, 0)),
                  _w_spec(w, layer, K, tn),
                  pl.BlockSpec((tm, tn), lambda i, j: (i, j)),
                  gate_spec],
        out_specs=pl.BlockSpec((tm, tn), lambda i, j: (i, j)),
        out_shape=jax.ShapeDtypeStruct((M, N), F32),
        compiler_params=_params(2),
        name=name,
    )(a, w, res, gate_arr)


def _ada_kernel(c_ref, w_ref, b_ref, o_ref):
    a = _silu(c_ref[...]).astype(BF)
    o_ref[...] = _dot(a, w_ref[...].astype(BF)) + b_ref[...]


def ada_dense(c, w, b, layer, *, name):
    M, K = c.shape
    N = w.shape[-1]
    tn = _col_tile(N, 2 * M * K * 4, _weight_col_bytes(K) + M * 3 * 4)
    if b.ndim == 2:
        b_spec = pl.BlockSpec((None, 1, tn), lambda i, j: (layer, 0, j))
        b = b.reshape(b.shape[0], 1, N)
    else:
        b_spec = pl.BlockSpec((1, tn), lambda i, j: (0, j))
        b = b.reshape(1, N)
    return pl.pallas_call(
        _ada_kernel,
        grid=(1, N // tn),
        in_specs=[pl.BlockSpec((M, K), lambda i, j: (0, 0)), _w_spec(w, layer, K, tn), b_spec],
        out_specs=pl.BlockSpec((M, tn), lambda i, j: (0, j)),
        out_shape=jax.ShapeDtypeStruct((M, N), F32),
        compiler_params=_params(2),
        name=name,
    )(c, w, b)


def _modulate_kernel(x_ref, g_ref, sh_ref, sc_ref, o_ref, *, rows):
    _modulate_rows(x_ref, g_ref, sh_ref, sc_ref, o_ref, rows)


def modulate_rows(x, g, shift, scale, rows_per_group, *, name):
    M, K = x.shape
    tm = _row_tile(M, rows_per_group, _grouped(shift))
    rows = _pick(tm, (128, 64, 32, 16, 8))
    mod_specs = [_mod_spec(m, K, tm, rows_per_group) for m in (shift, scale)]
    return pl.pallas_call(
        functools.partial(_modulate_kernel, rows=rows),
        grid=(M // tm,),
        in_specs=[pl.BlockSpec((tm, K), lambda i: (i, 0)), pl.BlockSpec((1, K), lambda i: (0, 0)),
                  *mod_specs],
        out_specs=pl.BlockSpec((tm, K), lambda i: (i, 0)),
        out_shape=jax.ShapeDtypeStruct((M, K), F32),
        compiler_params=_params(1),
        name=name,
    )(x, g.reshape(1, K), shift[0], scale[0])


def _dot1(a, b):
    return _dot(a.astype(BF), b.astype(BF))


def _tri_inverse(n_low, ii, jj, cp, nil):
    rn = n_low.shape[-1]
    eye = (ii == jj).astype(F32)
    base = min(TRI_BLOCK, cp)
    shift = base.bit_length() - 1
    nd = jnp.where((ii >> shift) == (jj >> shift), n_low, 0.0)
    p = eye - nd
    if nil > 2:
        npow = _dot1(nd, nd)
        pw = 2
        while 2 * pw - 1 < nil - 1:
            both = _dot1(jnp.concatenate([p, npow], axis=-2), npow)
            p = p + both[:, :rn]
            npow = both[:, rn:]
            pw *= 2
        p = p + _dot1(p, npow)
    size = base
    while size < cp:
        s = size.bit_length() - 1
        off = ((ii >> (s + 1)) == (jj >> (s + 1))) & (((ii >> s) & 1) == 1) & (((jj >> s) & 1) == 0)
        x = _dot1(jnp.where(off, n_low, 0.0), p)
        p = p - _dot1(p, x)
        size *= 2
    resid = (eye - p) - _dot3(n_low, p)
    return p + _dot1(p, resid)


def _delta_chunk(q_st, k_st, v_st, beta_row, g_row, load_state, store_state, *, G, Cp, nil):
    S, rn, dv = v_st.shape
    ii = lax.broadcasted_iota(jnp.int32, (1, rn, rn), 1)
    jj = lax.broadcasted_iota(jnp.int32, (1, rn, rn), 2)
    sh = Cp.bit_length() - 1
    same = (ii >> sh) == (jj >> sh)
    eye = ii == jj
    causal = same & (jj <= ii)
    g_mat = jnp.broadcast_to(g_row, (S, rn, rn))
    g_col = jnp.sum(jnp.where(eye, g_mat, 0.0), axis=2, keepdims=True)
    gc_col = jnp.sum(jnp.where(causal, g_mat, 0.0), axis=2, keepdims=True)
    gl_col = jnp.sum(jnp.where(same, g_mat, 0.0), axis=2, keepdims=True)
    gc_row = jnp.sum(jnp.where(same & (ii <= jj), jnp.broadcast_to(g_col, (S, rn, rn)), 0.0), axis=1, keepdims=True)
    beta_col = jnp.sum(jnp.where(eye, jnp.broadcast_to(beta_row, (S, rn, rn)), 0.0), axis=2, keepdims=True)
    decay = jnp.exp(jnp.where(causal, gc_col - gc_row, -jnp.inf))
    eg = jnp.exp(gc_col)
    kb = k_st * beta_col
    both = _dot_nt(jnp.concatenate([kb, q_st], axis=1).astype(BF), k_st.astype(BF))
    n_low = jnp.where(same & (jj < ii), both[:, :rn] * decay, 0.0)
    t_inv = _tri_inverse(n_low, ii, jj, Cp, nil).astype(BF)
    uw = _dot(t_inv, jnp.concatenate([v_st * beta_col, kb * eg], axis=2).astype(BF))
    u, w = uw[:, :, :dv], uw[:, :, dv:]
    qe = q_st * eg
    k_dec = k_st * jnp.exp(gl_col - gc_col)
    pad = jnp.zeros((S, -Cp % BF16_ROWS, k_dec.shape[2]), F32)

    def bf16_rows(x):
        return (jnp.concatenate([x, pad], axis=1) if pad.shape[1] else x).astype(BF)

    v_new, q_s = [], []
    for h in range(G):
        r = slice(h * Cp, (h + 1) * Cp)
        s_old = jnp.stack([load_state(s * G + h) for s in range(S)], axis=0)
        ws_qs = _dot(jnp.concatenate([w[:, r], qe[:, r]], axis=1).astype(BF), s_old.astype(BF))
        v_new_h = u[:, r] - ws_qs[:, :Cp]
        q_s.append(ws_qs[:, Cp:])
        v_new.append(v_new_h)
        s_new = (s_old * jnp.exp(gl_col[:, h * Cp:h * Cp + 1])
                 + _dot_tn(bf16_rows(k_dec[:, r]), bf16_rows(v_new_h)))
        for s in range(S):
            store_state(s * G + h, s_new[s])
    v_new = jnp.concatenate(v_new, axis=1).astype(BF)
    return jnp.concatenate(q_s, axis=1) + _dot((both[:, rn:] * decay).astype(BF), v_new)


def _gdn_kernel(*refs, C, Cp, G, NS, BB, rep, DK, DV, has_state, single_chunk, activated):
    q_ref, k_ref, v_ref, z_ref, b_ref, a_ref, alog_ref, dtb_ref, wq_ref, wk_ref, wv_ref, gn_ref = refs[:12]
    pos = 12
    if has_state:
        cq_ref, ck_ref, cv_ref, s0_ref = refs[pos:pos + 4]
        pos += 4
    o_ref, so_ref = refs[pos:pos + 2]
    s_scr, cbq, cbk, cbv = refs[pos + 2:]

    c = pl.program_id(2)
    last = pl.num_programs(2) - 1
    tail_lo = SUBLANES - (CONV_W - 1)
    direct_state = has_state and single_chunk

    @pl.when(c == 0)
    def _():
        for cb in (cbq, cbk, cbv):
            cb[...] = jnp.zeros_like(cb)
        if has_state:
            cbq[:, tail_lo:SUBLANES, :] = cq_ref[...]
            cbk[:, tail_lo:SUBLANES, :] = ck_ref[...]
            cbv[:, tail_lo:SUBLANES, :] = cv_ref[...]
            if not direct_state:
                for bb in range(BB):
                    s_scr[bb * G:(bb + 1) * G] = s0_ref[bb]
        else:
            s_scr[...] = jnp.zeros_like(s_scr)

    def conv(cb, x_ref, w_ref, bb):
        cb[bb, SUBLANES:SUBLANES + C, :] = x_ref[bb]
        acc = cb[bb, tail_lo:tail_lo + Cp, :] * w_ref[0:1, :]
        for j in range(1, CONV_W):
            acc = acc + cb[bb, tail_lo + j:tail_lo + j + Cp, :] * w_ref[j:j + 1, :]
        cb[bb, tail_lo:SUBLANES, :] = cb[bb, C + tail_lo:C + SUBLANES, :]
        return _silu(acc)

    padded = Cp != C
    if padded:
        row_ok = lax.broadcasted_iota(jnp.int32, (Cp, 1), 0) < C
        lane_ok = (lax.broadcasted_iota(jnp.int32, (1, G * Cp), 1) & (Cp - 1)) < C
    GS = G // NS
    rn = GS * Cp
    q_parts, k_parts, v_parts, beta_rows, g_rows = [], [], [], [], []
    for bb in range(BB):
        if activated:
            qc, kc, vc = q_ref[bb], k_ref[bb], v_ref[bb]
        else:
            qc = conv(cbq, q_ref, wq_ref, bb)
            kc = conv(cbk, k_ref, wk_ref, bb)
            vc = conv(cbv, v_ref, wv_ref, bb)
        for i in range(G // rep):
            qh = qc[:, i * DK:(i + 1) * DK]
            kh = kc[:, i * DK:(i + 1) * DK]
            if not activated:
                qh = qh * lax.rsqrt(jnp.sum(qh * qh, axis=-1, keepdims=True) + EPS) * (DK ** -0.5)
                kh = kh * lax.rsqrt(jnp.sum(kh * kh, axis=-1, keepdims=True) + EPS)
            if padded:
                kh = jnp.where(row_ok, kh, 0.0)
            for e in range(rep):
                hl = i * rep + e
                vh = vc[:, hl * DV:(hl + 1) * DV]
                if padded:
                    vh = jnp.where(row_ok, vh, 0.0)
                q_parts.append(qh)
                k_parts.append(kh)
                v_parts.append(vh)
        beta_row = jax.nn.sigmoid(b_ref[bb, c])
        sp_in = a_ref[bb, c] + dtb_ref[...]
        softplus = jnp.maximum(sp_in, 0.0) + jnp.log1p(jnp.exp(-jnp.abs(sp_in)))
        g_row = -jnp.exp(alog_ref[...]) * softplus
        if padded:
            beta_row = jnp.where(lane_ok, beta_row, 0.0)
            g_row = jnp.where(lane_ok, g_row, 0.0)
        beta_rows += [beta_row[:, s * rn:(s + 1) * rn] for s in range(NS)]
        g_rows += [g_row[:, s * rn:(s + 1) * rn] for s in range(NS)]

    def load_state(i):
        return s0_ref[i // G, i % G] if direct_state else s_scr[i]

    def store_state(i, value):
        if direct_state:
            so_ref[i // G, i % G] = value
        else:
            s_scr[i] = value

    def stacks(parts):
        return jnp.stack([jnp.concatenate(parts[s * GS:(s + 1) * GS], axis=0) for s in range(BB * NS)], axis=0)

    o = _delta_chunk(stacks(q_parts), stacks(k_parts), stacks(v_parts), jnp.stack(beta_rows, axis=0),
                     jnp.stack(g_rows, axis=0), load_state, store_state, G=GS, Cp=Cp, nil=min(C, TRI_BLOCK))
    on = o * lax.rsqrt(jnp.mean(o * o, axis=-1, keepdims=True) + EPS) * gn_ref[...]
    for bb in range(BB):
        for s in range(NS):
            for h in range(GS):
                hl = s * GS + h
                zg = z_ref[bb, :, hl * DV:(hl + 1) * DV]
                o_ref[bb, :, hl * DV:(hl + 1) * DV] = (
                    on[bb * NS + s, h * Cp:h * Cp + C] * (zg if activated else _silu(zg))).astype(o_ref.dtype)

    if not direct_state:
        @pl.when(c == last)
        def _():
            for bb in range(BB):
                so_ref[bb] = s_scr[bb * G:(bb + 1) * G]


def gdn_scan(proj, b_logit, a_logit, conv_prev, s0, w_conv, a_log, dt_bias, g_norm, *, B, L, HQK, HV, DK, DV,
             activated, name):
    rep = HV // HQK
    KEY, VAL = HQK * DK, HV * DV
    C = _pick(L, (GDN_CHUNK,))
    n = L // C
    Cp = -(-C // SUBLANES) * SUBLANES
    assert Cp & (Cp - 1) == 0 and L >= CONV_W - 1 and not (activated and Cp != C)
    GS = min(HV, max(rep, min(STACK_ROWS // Cp, MAX_STACK_HEADS)))
    NS = _pick(HV // GS, (STACKS_PER_STEP, 4, 2, 1))
    G = NS * GS
    assert HV % G == 0 and GS % rep == 0
    has_state = s0 is not None
    BB = _pick(B, (SEQS_PER_STEP, 1)) if n == 1 else 1
    qw, vw = (G // rep) * DK, G * DV
    k_off, v_off, z_off = KEY // qw, 2 * KEY // vw, (2 * KEY + VAL) // vw

    def stack_rows(t):
        t = t.reshape(B, n, C, HV // G, G)
        t = jnp.pad(t, ((0, 0), (0, 0), (0, Cp - C), (0, 0), (0, 0)))
        return jnp.transpose(t, (0, 3, 1, 4, 2)).reshape(B, HV // G, n, 1, G * Cp)

    def stack_heads(v):
        return jnp.repeat(v.reshape(HV // G, 1, G), Cp, axis=-1)

    gate_spec = pl.BlockSpec((BB, None, n, 1, G * Cp), lambda b, h, c: (b, h, 0, 0, 0))
    head_spec = pl.BlockSpec((None, 1, G * Cp), lambda b, h, c: (h, 0, 0))
    in_specs = [
        pl.BlockSpec((BB, C, qw), lambda b, h, c: (b, c, h)),
        pl.BlockSpec((BB, C, qw), lambda b, h, c: (b, c, k_off + h)),
        pl.BlockSpec((BB, C, vw), lambda b, h, c: (b, c, v_off + h)),
        pl.BlockSpec((BB, C, vw), lambda b, h, c: (b, c, z_off + h)),
        gate_spec, gate_spec, head_spec, head_spec,
        pl.BlockSpec((CONV_W, qw), lambda b, h, c: (0, h)),
        pl.BlockSpec((CONV_W, qw), lambda b, h, c: (0, k_off + h)),
        pl.BlockSpec((CONV_W, vw), lambda b, h, c: (0, v_off + h)),
        pl.BlockSpec((1, DV), lambda b, h, c: (0, 0)),
    ]
    args = [proj, proj, proj, proj, stack_rows(b_logit), stack_rows(a_logit), stack_heads(a_log),
            stack_heads(dt_bias), w_conv, w_conv, w_conv, g_norm.reshape(1, DV)]
    if has_state:
        in_specs += [
            pl.BlockSpec((BB, CONV_W - 1, qw), lambda b, h, c: (b, 0, h)),
            pl.BlockSpec((BB, CONV_W - 1, qw), lambda b, h, c: (b, 0, k_off + h)),
            pl.BlockSpec((BB, CONV_W - 1, vw), lambda b, h, c: (b, 0, v_off + h)),
            pl.BlockSpec((BB, G, DK, DV), lambda b, h, c: (b, h, 0, 0)),
        ]
        args += [conv_prev, conv_prev, conv_prev, s0]
    body = functools.partial(_gdn_kernel, C=C, Cp=Cp, G=G, NS=NS, BB=BB, rep=rep, DK=DK, DV=DV,
                             has_state=has_state, single_chunk=n == 1, activated=activated)
    return pl.pallas_call(
        body,
        grid=(B // BB, HV // G, n),
        in_specs=in_specs,
        out_specs=[pl.BlockSpec((BB, C, vw), lambda b, h, c: (b, c, h)),
                   pl.BlockSpec((BB, G, DK, DV), lambda b, h, c: (b, h, 0, 0))],
        out_shape=[jax.ShapeDtypeStruct((B, L, VAL), BF), jax.ShapeDtypeStruct((B, HV, DK, DV), F32)],
        scratch_shapes=[pltpu.VMEM((BB * G, DK, DV), F32),
                        pltpu.VMEM((BB, SUBLANES + Cp, qw), F32),
                        pltpu.VMEM((BB, SUBLANES + Cp, qw), F32),
                        pltpu.VMEM((BB, SUBLANES + Cp, vw), F32)],
        compiler_params=_params(3),
        name=name,
    )(*args)


def _kv_kernel(x_ref, g_ref, sh_ref, sc_ref, w_ref, gkv_ref, cos_ref, sin_ref,
               ckv_ref, kpe_ref, kcat_ref, h_ref, *, rows, R, P):
    _modulate_rows(x_ref, g_ref, sh_ref, sc_ref, h_ref, rows)
    y = _dot(h_ref[...], w_ref[...].astype(BF))
    c = y[:, :R]
    ckv = c * lax.rsqrt(jnp.mean(c * c, axis=-1, keepdims=True) + EPS) * gkv_ref[...]
    rot = y[:, R:R + LANES] * cos_ref[...] + y[:, R + LANES:R + 2 * LANES] * sin_ref[...]
    ckv_ref[...] = ckv
    kpe_ref[...] = rot[:, :P]
    kcat_ref[:, :R] = ckv.astype(BF)
    kcat_ref[:, R:] = rot.astype(BF)


def _rope_weight_cols(w_pe):
    P = w_pe.shape[-1]
    half = P // 2
    zeros = jnp.zeros(w_pe.shape[:-1] + (LANES - P,), w_pe.dtype)
    swapped = jnp.concatenate([w_pe[..., half:], w_pe[..., :half]], axis=-1)
    return jnp.concatenate([w_pe, zeros, swapped, zeros], axis=-1)


def kv_down_weight(w_down, R):
    return jnp.concatenate([w_down[:, :R], _rope_weight_cols(w_down[:, R:])], axis=1)


def query_up_weight(w_uq, H, NOPE):
    QL = w_uq.shape[0]
    w_heads = w_uq.reshape(QL, H, -1)
    per_head = jnp.concatenate([w_heads[..., :NOPE], _rope_weight_cols(w_heads[..., NOPE:])], axis=-1)
    return jnp.transpose(per_head, (1, 0, 2))


def shared_kv(x, g, shift, scale, w_ext, g_kv, cos_t, sin_t, rows_per_group, *, R, P, name):
    M, K = x.shape
    tm = _row_tile(M, rows_per_group, _grouped(shift))
    tm = min(tm, 512)
    rows = _pick(tm, (128, 64, 32, 16, 8))
    NW = R + 2 * LANES
    mod_specs = [_mod_spec(m, K, tm, rows_per_group) for m in (shift, scale)]
    row = lambda i: (i, 0)
    fixed = lambda i: (0, 0)
    return pl.pallas_call(
        functools.partial(_kv_kernel, rows=rows, R=R, P=P),
        grid=(M // tm,),
        in_specs=[pl.BlockSpec((tm, K), row), pl.BlockSpec((1, K), fixed), *mod_specs,
                  pl.BlockSpec((K, NW), fixed), pl.BlockSpec((1, R), fixed),
                  pl.BlockSpec((tm, LANES), row), pl.BlockSpec((tm, LANES), row)],
        out_specs=[pl.BlockSpec((tm, R), row), pl.BlockSpec((tm, P), row), pl.BlockSpec((tm, R + LANES), row)],
        out_shape=[jax.ShapeDtypeStruct((M, R), F32), jax.ShapeDtypeStruct((M, P), F32),
                   jax.ShapeDtypeStruct((M, R + LANES), BF)],
        scratch_shapes=[pltpu.VMEM((tm, K), BF)],
        compiler_params=_params(1),
        name=name,
    )(x, g.reshape(1, K), shift[0], scale[0], w_ext, g_kv.reshape(1, R), cos_t, sin_t)


def _q_kernel(x_ref, g_ref, sh_ref, sc_ref, wdq_ref, gq_ref, wuq_ref, wuk_ref, cos_ref, sin_ref,
              o_ref, cq_ref, h_ref, *, rows, NOPE, R, HB, head_major):
    @pl.when(pl.program_id(1) == 0)
    def _():
        _modulate_rows(x_ref, g_ref, sh_ref, sc_ref, h_ref, rows)
        c = _dot(h_ref[...], wdq_ref[...].astype(BF))
        cq_ref[...] = (c * lax.rsqrt(jnp.mean(c * c, axis=-1, keepdims=True) + EPS) * gq_ref[...]).astype(BF)

    W = R + LANES
    for hb in range(HB):
        qf = _dot(cq_ref[...], wuq_ref[hb].astype(BF))
        q_lat = _dot_nt(qf[:, :NOPE].astype(BF), wuk_ref[:, hb * NOPE:(hb + 1) * NOPE].astype(BF))
        rot = qf[:, NOPE:NOPE + LANES] * cos_ref[...] + qf[:, NOPE + LANES:NOPE + 2 * LANES] * sin_ref[...]
        if head_major:
            o_ref[hb, :, :R] = q_lat.astype(o_ref.dtype)
            o_ref[hb, :, R:] = rot.astype(o_ref.dtype)
        else:
            o_ref[:, hb * W:hb * W + R] = q_lat.astype(o_ref.dtype)
            o_ref[:, hb * W + R:(hb + 1) * W] = rot.astype(o_ref.dtype)


def mla_queries(x, g, shift, scale, w_dq, g_q, w_uq_r, w_uk, cos_t, sin_t, rows_per_group, *,
                H, NOPE, P, R, head_major, name):
    M, K = x.shape
    QL = w_dq.shape[1]
    tm = _row_tile(M, rows_per_group, True) if head_major else _row_tile(M, rows_per_group, _grouped(shift))
    tm = min(tm, 512)
    rows = _pick(tm, (128, 64, 32, 16, 8))
    NQ = NOPE + 2 * LANES
    W = R + LANES
    HB = _pick(H, (Q_HEADS_PER_STEP, 2, 1))
    mod_specs = [_mod_spec(m, K, tm, rows_per_group) for m in (shift, scale)]
    row = lambda i, h: (i, 0)
    fixed = lambda i, h: (0, 0)
    if head_major:
        tiles = rows_per_group // tm
        out_spec = pl.BlockSpec((None, HB, tm, W), lambda i, h: (i // tiles, h, i % tiles, 0))
        out_shape = jax.ShapeDtypeStruct((M // rows_per_group, H, rows_per_group, W), BF)
    else:
        out_spec = pl.BlockSpec((tm, HB * W), lambda i, h: (i, h))
        out_shape = jax.ShapeDtypeStruct((M, H * W), BF)
    return pl.pallas_call(
        functools.partial(_q_kernel, rows=rows, NOPE=NOPE, R=R, HB=HB, head_major=head_major),
        grid=(M // tm, H // HB),
        in_specs=[pl.BlockSpec((tm, K), row), pl.BlockSpec((1, K), fixed), *mod_specs,
                  pl.BlockSpec((K, QL), fixed), pl.BlockSpec((1, QL), fixed),
                  pl.BlockSpec((HB, QL, NQ), lambda i, h: (h, 0, 0)),
                  pl.BlockSpec((R, HB * NOPE), lambda i, h: (0, h)),
                  pl.BlockSpec((tm, LANES), row), pl.BlockSpec((tm, LANES), row)],
        out_specs=out_spec,
        out_shape=out_shape,
        scratch_shapes=[pltpu.VMEM((tm, QL), BF), pltpu.VMEM((tm, K), BF)],
        compiler_params=_params(2),
        name=name,
    )(x, g.reshape(1, K), shift[0], scale[0], w_dq, g_q.reshape(1, QL), w_uq_r, w_uk, cos_t, sin_t)


def _attn_prompt_kernel(q_ref, k_ref, o_ref, m_scr, l_scr, a_scr, acc_scr, s_scr, p_scr, *, tq, tk, H, R, scale):
    qi = pl.program_id(1)
    kj = pl.program_id(2)

    @pl.when(kj == 0)
    def _():
        m_scr[...] = jnp.full_like(m_scr, -jnp.inf)
        l_scr[...] = jnp.zeros_like(l_scr)
        acc_scr[...] = jnp.zeros_like(acc_scr)

    rows = H * tq
    hg = max(1, ATTN_GROUP_ROWS // tq)
    group = hg * tq
    n_groups = rows // group
    rc = min(ATTN_SOFTMAX_ROWS, group)
    c2 = scale * LOG2_E

    def scores(g):
        q = q_ref[g * hg:(g + 1) * hg].reshape(group, q_ref.shape[-1])
        s_scr[g * group:(g + 1) * group, :] = _dot_nt(q, k_ref[...])

    def softmax(g, masked):
        for r0 in range(g * group, (g + 1) * group, rc):
            sl = slice(r0, r0 + rc)
            s = s_scr[sl, :]
            if masked:
                qpos = qi * tq + lax.rem(r0 + lax.broadcasted_iota(jnp.int32, (rc, tk), 0), tq)
                kpos = kj * tk + lax.broadcasted_iota(jnp.int32, (rc, tk), 1)
                s = jnp.where(kpos <= qpos, s, -jnp.inf)
            m_old = m_scr[sl, :]
            m_new = jnp.maximum(m_old, jnp.max(s, axis=-1, keepdims=True))
            alpha = jnp.exp2((m_old - m_new) * c2)
            p = jnp.exp2((s - m_new) * c2)
            l_scr[sl, :] = alpha * l_scr[sl, :] + jnp.sum(p, axis=-1, keepdims=True)
            m_scr[sl, :] = m_new
            a_scr[sl, :] = alpha
            p_scr[sl, :] = p.astype(BF)

    def values(g):
        gs = slice(g * group, (g + 1) * group)
        acc_scr[gs, :] = a_scr[gs, :] * acc_scr[gs, :] + _dot(p_scr[gs, :], k_ref[:, :R])

    def block(masked):
        for g in range(n_groups + 2):
            if g < n_groups:
                scores(g)
            if 1 <= g <= n_groups:
                softmax(g - 1, masked)
            if g >= 2:
                values(g - 2)

    crosses_diagonal = kj * tk + tk - 1 > qi * tq

    @pl.when((kj * tk <= qi * tq + tq - 1) & crosses_diagonal)
    def _():
        block(True)

    @pl.when(jnp.logical_not(crosses_diagonal))
    def _():
        block(False)

    @pl.when(kj == pl.num_programs(2) - 1)
    def _():
        o = acc_scr[...] / l_scr[...]
        o_ref[...] = o.reshape(H, tq, R).astype(o_ref.dtype)


def attention_prompt(q4, kcat, *, R, scale, name):
    B, H, L, W = q4.shape
    tq = _pick(L, (128, 64, 32, 16))
    tk = _pick(L, (512, 256, 128, 64, 32, 16))
    nq, nk = L // tq, L // tk

    def k_index(b, i, j):
        return (b, jnp.minimum(j, (i * tq + tq - 1) // tk), 0)

    return pl.pallas_call(
        functools.partial(_attn_prompt_kernel, tq=tq, tk=tk, H=H, R=R, scale=scale),
        grid=(B, nq, nk),
        in_specs=[pl.BlockSpec((None, H, tq, W), lambda b, i, j: (b, 0, i, 0)),
                  pl.BlockSpec((None, tk, W), k_index)],
        out_specs=pl.BlockSpec((None, H, tq, R), lambda b, i, j: (b, 0, i, 0)),
        out_shape=jax.ShapeDtypeStruct((B, H, L, R), BF),
        scratch_shapes=[pltpu.VMEM((H * tq, 1), F32), pltpu.VMEM((H * tq, 1), F32), pltpu.VMEM((H * tq, 1), F32),
                        pltpu.VMEM((H * tq, R), F32), pltpu.VMEM((H * tq, tk), F32), pltpu.VMEM((H * tq, tk), BF)],
        compiler_params=_params(3),
        name=name,
    )(q4, kcat)


def _attn_sample_kernel(pt_ref, q_ref, knew_ref, ckv_hbm, kpe_hbm, o_ref, m_scr, l_scr, acc_scr,
                        ckv_buf, kpe_buf, sems, *, NP, T, H, R, P, scale):
    b = pl.program_id(0)
    j = pl.program_id(1)
    nj = pl.num_programs(1)
    step = b * nj + j
    n_steps = pl.num_programs(0) * nj
    n_slots = ckv_buf.shape[0]
    slot = lax.rem(step, n_slots)
    rows = T * H

    def page_copies(st):
        bb, jj, sl = lax.div(st, nj), lax.rem(st, nj), lax.rem(st, n_slots)
        copies = []
        for i in range(NP):
            pid = pt_ref[bb, jj * NP + i]
            copies.append(pltpu.make_async_copy(ckv_hbm.at[pid], ckv_buf.at[sl, i], sems.at[0, sl]))
            copies.append(pltpu.make_async_copy(kpe_hbm.at[pid], kpe_buf.at[sl, i], sems.at[1, sl]))
        return copies

    @pl.when(step == 0)
    def _():
        for ahead in range(n_slots - 1):
            @pl.when(ahead < n_steps)
            def _():
                for cp in page_copies(step + ahead):
                    cp.start()

    @pl.when(step + n_slots - 1 < n_steps)
    def _():
        for cp in page_copies(step + n_slots - 1):
            cp.start()

    for cp in page_copies(step):
        cp.wait()
    q = q_ref[...]
    q_lat = q[:, :R]
    q_pe = q[:, R:R + P]

    c2 = scale * LOG2_E

    def update(state, s, values):
        m_old, l_old, acc = state
        m_new = jnp.maximum(m_old, jnp.max(s, axis=-1, keepdims=True))
        alpha = jnp.exp2((m_old - m_new) * c2)
        p = jnp.exp2((s - m_new) * c2)
        return m_new, alpha * l_old + jnp.sum(p, axis=-1, keepdims=True), alpha * acc + values(p.astype(BF))

    @pl.when(j == 0)
    def _():
        tp = -(-T // BF16_ROWS) * BF16_ROWS
        knew = jnp.concatenate([knew_ref[...], jnp.zeros((tp - T, knew_ref.shape[1]), BF)], axis=0)
        s = _dot_nt(q, knew)
        qt = lax.broadcasted_iota(jnp.int32, (rows, tp), 0) // H
        kt = lax.broadcasted_iota(jnp.int32, (rows, tp), 1)
        s = jnp.where(kt <= qt, s, -jnp.inf)
        init = (jnp.full((rows, 1), -jnp.inf, F32), jnp.zeros((rows, 1), F32), jnp.zeros((rows, R), F32))
        m_scr[...], l_scr[...], acc_scr[...] = update(init, s, lambda p: _dot(p, knew[:, :R]))

    pg = min(ATTN_PAGE_GROUP, NP)
    n_groups = NP // pg

    def load(g):
        return [ckv_buf[slot, i].astype(BF) for i in range(g * pg, (g + 1) * pg)]

    def scores(g, pages):
        return jnp.concatenate([_dot_nt(q_lat, pages[i]) + _dot(q_pe, kpe_buf[slot, g * pg + i].astype(BF))
                                for i in range(pg)], axis=1)

    def values_of(pages):
        def values(p):
            page = pages[0].shape[0]
            out = _dot(p[:, :page], pages[0])
            for i in range(1, pg):
                out = out + _dot(p[:, i * page:(i + 1) * page], pages[i])
            return out
        return values

    state = (m_scr[...], l_scr[...], acc_scr[...])
    pages = load(0)
    s = scores(0, pages)
    for g in range(n_groups):
        if g + 1 < n_groups:
            pages_next = load(g + 1)
            s_next = scores(g + 1, pages_next)
        state = update(state, s, values_of(pages))
        if g + 1 < n_groups:
            pages, s = pages_next, s_next
    m_scr[...], l_scr[...], acc_scr[...] = state

    @pl.when(j == pl.num_programs(1) - 1)
    def _():
        o_ref[...] = (acc_scr[...] / l_scr[...]).astype(o_ref.dtype)


def attention_sample(q3, knew, cache_ckv, cache_kpe_t, page_table, *, T, H, R, P, scale, name):
    Bd, rows, W = q3.shape
    n_pages = page_table.shape[1]
    page = cache_ckv.shape[1]
    NP = _pick(n_pages, (ATTN_PAGES_PER_STEP, 16, 8, 4, 2, 1))
    any_space = pl.BlockSpec(memory_space=pl.ANY)
    grid_spec = pltpu.PrefetchScalarGridSpec(
        num_scalar_prefetch=1,
        grid=(Bd, n_pages // NP),
        in_specs=[pl.BlockSpec((None, rows, W), lambda b, j, pt: (b, 0, 0)),
                  pl.BlockSpec((None, T, W), lambda b, j, pt: (b, 0, 0)),
                  any_space, any_space],
        out_specs=pl.BlockSpec((None, rows, R), lambda b, j, pt: (b, 0, 0)),
        scratch_shapes=[pltpu.VMEM((rows, 1), F32), pltpu.VMEM((rows, 1), F32), pltpu.VMEM((rows, R), F32),
                        pltpu.VMEM((ATTN_PAGE_SLOTS, NP, page, R), F32),
                        pltpu.VMEM((ATTN_PAGE_SLOTS, NP, P, page), F32),
                        pltpu.SemaphoreType.DMA((2, ATTN_PAGE_SLOTS))],
    )
    return pl.pallas_call(
        functools.partial(_attn_sample_kernel, NP=NP, T=T, H=H, R=R, P=P, scale=scale),
        grid_spec=grid_spec,
        out_shape=jax.ShapeDtypeStruct((Bd, rows, R), BF),
        compiler_params=_params(2),
        name=name,
    )(page_table, q3, knew, cache_ckv, cache_kpe_t)


def _value_up_kernel(a_ref, w_ref, o_ref, *, H, R, V, head_major):
    for h in range(H):
        a = a_ref[h] if head_major else a_ref[:, h * R:(h + 1) * R]
        o_ref[:, h * V:(h + 1) * V] = _dot(a, w_ref[:, h * V:(h + 1) * V].astype(BF)).astype(o_ref.dtype)


def value_up(o_lat, w_uv, *, H, R, V, head_major, name):
    if head_major:
        G, _, L, _ = o_lat.shape
        M = G * L
        tm = _pick(L, (512, 256, 128, 64, 32, 16, 8))
        tiles = L // tm
        a_spec = pl.BlockSpec((None, H, tm, R), lambda i: (i // tiles, 0, i % tiles, 0))
    else:
        M = o_lat.shape[0]
        tm = _pick(M, (512, 256, 128, 64, 32, 16, 8))
        a_spec = pl.BlockSpec((tm, H * R), lambda i: (i, 0))
    return pl.pallas_call(
        functools.partial(_value_up_kernel, H=H, R=R, V=V, head_major=head_major),
        grid=(M // tm,),
        in_specs=[a_spec, pl.BlockSpec((R, H * V), lambda i: (0, 0))],
        out_specs=pl.BlockSpec((tm, H * V), lambda i: (i, 0)),
        out_shape=jax.ShapeDtypeStruct((M, H * V), BF),
        compiler_params=_params(1),
        name=name,
    )(o_lat, w_uv)


def _rope_tables(pos, P, reps):
    half = P // 2
    inv = ROPE_THETA ** (-jnp.arange(half, dtype=F32) / half)
    ang = pos.astype(F32)[:, None] * inv[None, :]
    cos, sin = jnp.cos(ang), jnp.sin(ang)
    zeros = jnp.zeros((pos.shape[0], LANES - P), F32)
    cos_t = jnp.concatenate([cos, cos, zeros], axis=1)
    sin_t = jnp.concatenate([-sin, sin, zeros], axis=1)
    return jnp.tile(cos_t, (reps, 1)), jnp.tile(sin_t, (reps, 1))


def _trunk(x3, mods, pos, conv_in, ssm_in, past, p, tag):
    G, L, D = x3.shape
    M = G * L
    grouped = L % SUBLANES == 0
    x = x3.reshape(M, D)

    def split_mods(m, n):
        arr = m[:, None, :] if grouped else m
        return [(arr, k) for k in range(n)]

    HV, DK, DV = p['HV'], p['DK'], p['DV']
    HQK = p['HQK']
    KEY, VAL = HQK * DK, HV * DV
    H, NOPE, P, R, V = p['H'], p['NOPE'], p['P'], p['R'], p['V']
    FF = p['w_down'].shape[1]
    cos_t, sin_t = _rope_tables(pos, P, G)
    scale = (NOPE + P) ** -0.5

    sh1, sc1, gt1, sh2, sc2, gt2 = split_mods(mods['l0'], 6)
    w_in_t = p['gdn_w_in_t']
    n_proj = 2 * KEY + 2 * VAL
    fused_front = conv_in is None and grouped and L % _row_tile(M, L, True) == 0
    if fused_front:
        proj, tail, ba = gdn_front(x, p['g_mix'][0], sh1, sc1, w_in_t, p['gdn_w_ba_t'], p['gdn_w_conv'][0], L,
                                   KEY=KEY, VAL=VAL, DK=DK, name=f'{tag}_gdn_in')
    else:
        proj = mod_matmul(x, p['g_mix'][0], sh1, sc1, w_in_t, 0, n_proj, L, w_is_nk=True, name=f'{tag}_gdn_in')
        ba = mod_matmul(x, p['g_mix'][0], sh1, sc1, p['gdn_w_ba_t'], 0, LANES, L, w_is_nk=True,
                        name=f'{tag}_gdn_ba')
    proj3 = proj.reshape(G, L, n_proj)
    o_g, ssm_new = gdn_scan(
        proj3, ba[:, :HV], ba[:, HV:2 * HV], conv_in, ssm_in, p['gdn_w_conv'][0], p['gdn_a_log'][0],
        p['gdn_dt_bias'][0], p['gdn_g_norm'][0], B=G, L=L, HQK=HQK, HV=HV, DK=DK, DV=DV, activated=fused_front,
        name=f'{tag}_gdn_scan')
    if fused_front:
        conv_src = tail.reshape(G, -1, SUBLANES, n_proj)[:, -1]
    else:
        conv_src = proj3
    conv_new = conv_src[:, conv_src.shape[1] - (CONV_W - 1):, :2 * KEY + VAL]
    x = matmul_residual(o_g.reshape(M, VAL), p['gdn_w_out'], 0, x, gt1, L, name=f'{tag}_gdn_out')
    hff = mod_matmul(x, p['g_ffn'][0], sh2, sc2, p['w_gate_up'], 0, FF, L, swiglu=True, out_dtype=BF,
                     name=f'{tag}_ffn0_up')
    x = matmul_residual(hff, p['w_down'], 0, x, gt2, L, name=f'{tag}_ffn0_down')

    shk, sck = split_mods(mods['kv'], 2)
    ckv, kpe, kcat = shared_kv(x, p['kv_g_in'], shk, sck, p['kv_w_ext'], p['kv_g_norm'], cos_t, sin_t, L,
                               R=R, P=P, name=f'{tag}_kv')

    sh1, sc1, gt1, sh2, sc2, gt2 = split_mods(mods['l1'], 6)
    head_major = past is None
    q = mla_queries(x, p['g_mix'][1], sh1, sc1, p['mla_w_dq'][0], p['mla_g_q'][0], p['mla_w_uq_r'], p['kv_w_uk'],
                    cos_t, sin_t, L, H=H, NOPE=NOPE, P=P, R=R, head_major=head_major, name=f'{tag}_q')
    if head_major:
        o_lat = attention_prompt(q, kcat.reshape(G, L, R + LANES), R=R, scale=scale, name=f'{tag}_attn')
    else:
        cache_ckv, cache_kpe, page_table = past
        o_lat = attention_sample(q.reshape(G, L * H, R + LANES), kcat.reshape(G, L, R + LANES), cache_ckv,
                                 cache_kpe, page_table, T=L, H=H, R=R, P=P, scale=scale, name=f'{tag}_attn')
        o_lat = o_lat.reshape(M, H * R)
    o = value_up(o_lat, p['kv_w_uv'], H=H, R=R, V=V, head_major=head_major, name=f'{tag}_uv')
    x = matmul_residual(o, p['mla_w_o'], 0, x, gt1, L, name=f'{tag}_attn_out')
    hff = mod_matmul(x, p['g_ffn'][1], sh2, sc2, p['w_gate_up'], 1, FF, L, swiglu=True, out_dtype=BF,
                     name=f'{tag}_ffn1_up')
    x = matmul_residual(hff, p['w_down'], 1, x, gt2, L, name=f'{tag}_ffn1_down')

    shf, scf = split_mods(mods['final'], 2)
    y = modulate_rows(x, p['final_g'], shf, scf, L, name=f'{tag}_final')
    return (y.reshape(G, L, D), conv_new[None], ssm_new[None], ckv.reshape(G, L, R), kpe.reshape(G, L, P))


def kernel(x_prompt, x_sample, c_prompt, c_sample, cache_ckv, cache_kpe, page_table, state_ssm, state_conv, w_ada, b_ada, g_mix, g_ffn, w_gate_up, w_down, gdn_w_in, gdn_w_conv, gdn_a_log, gdn_dt_bias, gdn_g_norm, gdn_w_out, kv_w_ada, kv_b_ada, kv_g_in, kv_w_down, kv_g_norm, kv_w_uk, kv_w_uv, mla_w_dq, mla_g_q, mla_w_uq, mla_w_o, final_w_ada, final_b_ada, final_g):
    B, S, D = x_prompt.shape
    Bd, T, _ = x_sample.shape
    HV, DK, DV = state_ssm.shape[2:]
    KEY = (state_conv.shape[-1] - HV * DV) // 2
    R = cache_ckv.shape[-1]
    P = cache_kpe.shape[-1]
    QL = mla_w_dq.shape[-1]
    nope_total = kv_w_uk.shape[1]
    H = (mla_w_uq.shape[-1] - nope_total) // P
    NOPE = nope_total // H
    w_in_t = jnp.swapaxes(gdn_w_in, 1, 2)
    n_proj = 2 * KEY + 2 * HV * DV
    w_ba_t = jnp.concatenate([w_in_t[0, n_proj:], jnp.zeros((LANES - 2 * HV, D), F32)], axis=0)
    p = dict(w_ada=w_ada, b_ada=b_ada, g_mix=g_mix, g_ffn=g_ffn, w_gate_up=w_gate_up, w_down=w_down,
             gdn_w_in_t=w_in_t, gdn_w_ba_t=w_ba_t, gdn_w_conv=gdn_w_conv, gdn_a_log=gdn_a_log,
             gdn_dt_bias=gdn_dt_bias, gdn_g_norm=gdn_g_norm, gdn_w_out=gdn_w_out, kv_g_in=kv_g_in,
             kv_w_ext=kv_down_weight(kv_w_down, R), kv_g_norm=kv_g_norm, kv_w_uk=kv_w_uk, kv_w_uv=kv_w_uv,
             mla_w_dq=mla_w_dq, mla_g_q=mla_g_q, mla_w_uq_r=query_up_weight(mla_w_uq[0], H, NOPE),
             mla_w_o=mla_w_o, final_g=final_g,
             HV=HV, DK=DK, DV=DV, HQK=KEY // DK, H=H, NOPE=NOPE, P=P, R=R, V=kv_w_uv.shape[1] // H)

    per_row_s = T % SUBLANES != 0
    c_s = jnp.repeat(c_sample, T, axis=0) if per_row_s else c_sample
    n_s = c_s.shape[0]
    c_all = jnp.concatenate([c_s, c_prompt, jnp.zeros((-(n_s + B) % SUBLANES, D), F32)], axis=0)
    m_l0 = ada_dense(c_all, w_ada, b_ada, 0, name='ada_l0')
    m_l1 = ada_dense(c_all, w_ada, b_ada, 1, name='ada_l1')
    m_kv = ada_dense(c_all, kv_w_ada, kv_b_ada, 0, name='ada_kv')
    m_f = ada_dense(c_all, final_w_ada, final_b_ada, 0, name='ada_final')

    m_all = dict(l0=m_l0, l1=m_l1, kv=m_kv, final=m_f)
    mods_p = {k: m[n_s:n_s + B] for k, m in m_all.items()}
    mods_s = m_all if per_row_s else {k: m[:n_s] for k, m in m_all.items()}

    y_p, conv_p, ssm_p, ckv_p, kpe_p = _trunk(x_prompt, mods_p, jnp.arange(S), None, None, None, p, 'p')
    past_len = page_table.shape[1] * cache_ckv.shape[1]
    y_s, conv_s, ssm_s, ckv_s, kpe_s = _trunk(x_sample, mods_s, past_len + jnp.arange(T), state_conv[0],
                                              state_ssm[0], (cache_ckv, jnp.swapaxes(cache_kpe, 1, 2), page_table),
                                              p, 's')
    return (y_p, y_s, ssm_p, conv_p, ckv_p, kpe_p, ssm_s, conv_s, ckv_s, kpe_s)
```
